```python
import math
import jax
import jax.numpy as jnp
from jax import lax
import numpy as np

D_MODEL = 1024
BATCH = 8
SEQ = 8192
DEPTH = 1
DEC_BATCH = 16
DEC_SEQ = 32
PAST_LEN = 1024

CHUNK = 64
Q_BLOCK = 128
N_MEM = 256
EPS = 1e-6
SSM_WIDTH = D_MODEL
SSM_GROUP = 16
SSM_GROUPS = SSM_WIDTH // SSM_GROUP
SSM_STATE = 64
DSA_HEADS = 8
DSA_LATENT = 256
DSA_HEAD_DIM = D_MODEL // DSA_HEADS
IDX_HEADS = 8
IDX_DIM = 64
DSA_TOPK = 256
MEM_HEADS = 4
MEM_HEAD_DIM = D_MODEL // MEM_HEADS
MEM_WIDTH = MEM_HEADS * MEM_HEAD_DIM
PEER_HEADS = 8
PEER_NKEYS = 128
PEER_EXPERTS = PEER_NKEYS * PEER_NKEYS
PEER_QDIM = 256
PEER_HALF = PEER_QDIM // 2
PEER_TOPK = 16
PEER_TOK_BLOCK = 256
N_BRANCH = 3
IN_SPLITS = (SSM_WIDTH, DSA_HEADS * DSA_LATENT, DSA_LATENT, IDX_HEADS * IDX_DIM, IDX_DIM, IDX_HEADS, MEM_WIDTH, N_BRANCH * D_MODEL)
IN_WIDTH = sum(IN_SPLITS)

kernel_name = "hybrid_s5_dsa_peer_stream_step"

F32 = jnp.float32


def rmsnorm(x, g):
    xf = x.astype(F32)
    y = xf * lax.rsqrt(jnp.mean(xf * xf, axis=-1, keepdims=True) + EPS)
    return (y * g.astype(F32)).astype(x.dtype)


def split_in(z):
    offs = np.cumsum(np.array(IN_SPLITS))[:-1].tolist()
    return jnp.split(z, offs, axis=-1)


def ssm_discretise(lam_re, lam_im, log_dt, b_re, b_im):
    lr, li = lam_re.astype(F32), lam_im.astype(F32)
    dt = jnp.exp(log_dt.astype(F32))[:, None]
    mag = jnp.exp(lr * dt)
    ar, ai = mag * jnp.cos(li * dt), mag * jnp.sin(li * dt)
    den = lr * lr + li * li
    nr, ni = ar - 1.0, ai
    cr = (nr * lr + ni * li) / den
    ci = (ni * lr - nr * li) / den
    br, bi = b_re.astype(F32), b_im.astype(F32)
    bbr = cr[..., None] * br - ci[..., None] * bi
    bbi = cr[..., None] * bi + ci[..., None] * br
    return ar, ai, bbr, bbi


def _complex_affine_combine(e1, e2):
    a1r, a1i, b1r, b1i = e1
    a2r, a2i, b2r, b2i = e2
    return (a2r * a1r - a2i * a1i, a2r * a1i + a2i * a1r,
            a2r * b1r - a2i * b1i + b2r, a2r * b1i + a2i * b1r + b2i)


def ssm_branch(u, st_re, st_im, p):
    bsz, L, _ = u.shape
    blk = CHUNK if L % CHUNK == 0 else L
    nblk = L // blk
    ar, ai, bbr, bbi = ssm_discretise(p["ssm_lam_re"], p["ssm_lam_im"], p["ssm_log_dt"], p["ssm_b_re"], p["ssm_b_im"])
    cr, ci = p["ssm_c_re"].astype(F32), p["ssm_c_im"].astype(F32)
    uf = u.astype(F32)
    ub = uf.reshape(bsz, nblk, blk, SSM_GROUPS, SSM_GROUP).swapaxes(0, 1)

    def step(carry, u_blk):
        sr, si = carry
        bur = jnp.einsum('btgc,gpc->btgp', u_blk, bbr)
        bui = jnp.einsum('btgc,gpc->btgp', u_blk, bbi)
        a_r = jnp.broadcast_to(ar, bur.shape)
        a_i = jnp.broadcast_to(ai, bur.shape)
        acr, aci, bcr, bci = lax.associative_scan(_complex_affine_combine, (a_r, a_i, bur, bui), axis=1)
        xr = acr * sr[:, None] - aci * si[:, None] + bcr
        xi = acr * si[:, None] + aci * sr[:, None] + bci
        y = jnp.einsum('btgp,gcp->btgc', xr, cr) - jnp.einsum('btgp,gcp->btgc', xi, ci)
        return (xr[:, -1], xi[:, -1]), y

    (sr, si), ys = lax.scan(step, (st_re.astype(F32), st_im.astype(F32)), ub)
    y = ys.swapaxes(0, 1).reshape(bsz, L, SSM_WIDTH) + p["ssm_d"].astype(F32) * uf
    z = jax.nn.gelu(y)
    out = z * jax.nn.sigmoid(z @ p["w_glu"].astype(F32) + p["b_glu"].astype(F32))
    return out.astype(u.dtype), sr, si


def dsa_attend(q_lat, q_idx, w_idx, c_all, k_idx_all, q_pos, n_sel):
    s_pos = jnp.arange(c_all.shape[1])
    visible = s_pos[None, :] < (q_pos[:, None] // CHUNK + 1) * CHUNK
    rel = jax.nn.relu(jnp.einsum('bqhd,bsd->bqhs', q_idx, k_idx_all).astype(F32))
    score = jnp.einsum('bqh,bqhs->bqs', w_idx.astype(F32), rel)
    score = jnp.where(visible[None], score, -jnp.inf)
    top_val, top_idx = lax.top_k(score, n_sel)
    c_sel = jax.vmap(lambda cb, ib: cb[ib])(c_all, top_idx)
    keep = jnp.isfinite(top_val)
    logits = jnp.einsum('bqhc,bqkc->bqhk', q_lat, c_sel).astype(F32) * DSA_LATENT ** -0.5
    logits = jnp.where(keep[:, :, None, :], logits, -jnp.inf)
    prob = jax.nn.softmax(logits, axis=-1)
    return jnp.einsum('bqhk,bqkc->bqhc', prob.astype(c_sel.dtype), c_sel)


def dsa_prompt(q_lat, q_idx, w_idx, c, k_idx):
    bsz, L = q_lat.shape[:2]
    nqb = L // Q_BLOCK
    n_sel = min(DSA_TOPK, L // 4)

    def blockify(a):
        return a.reshape(bsz, nqb, Q_BLOCK, *a.shape[2:]).swapaxes(0, 1)

    pos = jnp.arange(L).reshape(nqb, Q_BLOCK)
    o = lax.map(lambda a: dsa_attend(a[0], a[1], a[2], c, k_idx, a[3], n_sel),
                (blockify(q_lat), blockify(q_idx), blockify(w_idx), pos))
    return o.swapaxes(0, 1).reshape(bsz, L, DSA_HEADS, DSA_LATENT)


def memory_kv(mem, g_mem, w_mem_kv):
    bsz = mem.shape[0]
    m = rmsnorm(mem, g_mem) @ w_mem_kv
    k, v = jnp.split(m, 2, axis=-1)
    return (k.reshape(bsz, N_MEM, MEM_HEADS, MEM_HEAD_DIM), v.reshape(bsz, N_MEM, MEM_HEADS, MEM_HEAD_DIM))


def mem_attend(zq, mk, mv):
    bsz, L = zq.shape[:2]
    q = zq.reshape(bsz, L, MEM_HEADS, MEM_HEAD_DIM)
    logits = jnp.einsum('blhd,bmhd->bhlm', q, mk.astype(q.dtype)).astype(F32) * MEM_HEAD_DIM ** -0.5
    prob = jax.nn.softmax(logits, axis=-1)
    o = jnp.einsum('bhlm,bmhd->blhd', prob.astype(q.dtype), mv.astype(q.dtype))
    return o.reshape(bsz, L, MEM_WIDTH)


def peer_block(xb, w_q, sub_k1, sub_k2, u_tab, v_tab):
    n = xb.shape[0]
    q = (xb @ w_q).reshape(n, PEER_HEADS, 2, PEER_HALF)
    s1 = jnp.einsum('nhd,hkd->nhk', q[:, :, 0], sub_k1).astype(F32)
    s2 = jnp.einsum('nhd,hkd->nhk', q[:, :, 1], sub_k2).astype(F32)
    v1, i1 = lax.top_k(s1, PEER_TOPK)
    v2, i2 = lax.top_k(s2, PEER_TOPK)
    cand = (v1[..., :, None] + v2[..., None, :]).reshape(n, PEER_HEADS, PEER_TOPK * PEER_TOPK)
    cand_idx = (i1[..., :, None] * PEER_NKEYS + i2[..., None, :]).reshape(n, PEER_HEADS, PEER_TOPK * PEER_TOPK)
    top_s, pos = lax.top_k(cand, PEER_TOPK)
    expert = jnp.take_along_axis(cand_idx, pos, axis=-1)
    gate = jax.nn.softmax(top_s, axis=-1)
    u = u_tab[expert]
    v = v_tab[expert]
    act = jax.nn.gelu(jnp.einsum('nd,nhkd->nhk', xb, u).astype(F32))
    return jnp.einsum('nhk,nhkd->nd', (gate * act).astype(v.dtype), v)


def peer(x, p):
    shp = x.shape
    xf = x.reshape(-1, D_MODEL)
    n = xf.shape[0]
    nb = -(-n // PEER_TOK_BLOCK)
    xb = jnp.pad(xf, ((0, nb * PEER_TOK_BLOCK - n), (0, 0))).reshape(nb, PEER_TOK_BLOCK, D_MODEL)
    y = lax.map(lambda b: peer_block(b, p["w_peer_q"], p["peer_sub_k1"], p["peer_sub_k2"], p["peer_u"], p["peer_v"]), xb)
    return y.reshape(nb * PEER_TOK_BLOCK, D_MODEL)[:n].reshape(shp)


def token_mix_inputs(h, p):
    bsz, L = h.shape[:2]
    z = rmsnorm(h, p["g_norm1"]) @ p["w_in"]
    z_u, z_q, z_c, z_qi, z_ki, z_wi, z_mq, z_g = split_in(z)
    q_lat = z_q.reshape(bsz, L, DSA_HEADS, DSA_LATENT)
    c = rmsnorm(z_c, p["g_kv"])
    q_idx = z_qi.reshape(bsz, L, IDX_HEADS, IDX_DIM)
    w_idx = z_wi * IDX_HEADS ** -0.5
    return z_u, q_lat, c, q_idx, z_ki, w_idx, z_mq, z_g


def merge_and_channel_mix(h, y_ssm, o_dsa, y_mem, z_g, p):
    bsz, L = h.shape[:2]
    y_dsa = jnp.einsum('blhc,hcd->blhd', o_dsa, p["w_uv"]).reshape(bsz, L, DSA_HEADS * DSA_HEAD_DIM)
    g = jax.nn.sigmoid(z_g.astype(F32)).reshape(bsz, L, N_BRANCH, D_MODEL)
    merged = (g[..., 0, :] * (y_ssm @ p["w_br_ssm"]) + g[..., 1, :] * (y_dsa @ p["w_br_dsa"])
              + g[..., 2, :] * (y_mem @ p["w_br_mem"]))
    h = h + merged.astype(h.dtype) @ p["w_out"]
    return h + peer(rmsnorm(h, p["g_norm2"]), p)


def layer_prompt(h, mem, p):
    z_u, q_lat, c, q_idx, k_idx, w_idx, z_mq, z_g = token_mix_inputs(h, p)
    zeros = jnp.zeros((h.shape[0], SSM_GROUPS, SSM_STATE), F32)
    y_ssm, s_re, s_im = ssm_branch(z_u, zeros, zeros, p)
    o_dsa = dsa_prompt(q_lat, q_idx, w_idx, c, k_idx)
    mk, mv = memory_kv(mem, p["g_mem"], p["w_mem_kv"])
    y_mem = mem_attend(z_mq, mk, mv)
    h = merge_and_channel_mix(h, y_ssm, o_dsa, y_mem, z_g, p)
    return h, c, k_idx, s_re, s_im, mk, mv


def layer_sample(h, c_cache, ki_cache, s_re, s_im, mk, mv, p):
    z_u, q_lat, c, q_idx, k_idx, w_idx, z_mq, z_g = token_mix_inputs(h, p)
    y_ssm, n_re, n_im = ssm_branch(z_u, s_re, s_im, p)
    past, T = c_cache.shape[1], h.shape[1]
    c_all = jnp.concatenate([c_cache.astype(c.dtype), c], axis=1)
    k_all = jnp.concatenate([ki_cache.astype(k_idx.dtype), k_idx], axis=1)
    n_sel = min(DSA_TOPK, (past + T) // 4)
    o_dsa = dsa_attend(q_lat, q_idx, w_idx, c_all, k_all, past + jnp.arange(T), n_sel)
    y_mem = mem_attend(z_mq, mk, mv)
    h = merge_and_channel_mix(h, y_ssm, o_dsa, y_mem, z_g, p)
    return h, c, k_idx, n_re, n_im


def setup_inputs(seed: int = 0) -> dict:
    key = jax.random.key(seed)
    keys = iter(jax.random.split(key, 48))

    def nrm(shape, scale):
        return scale * jax.random.normal(next(keys), shape, F32)

    def gain(n):
        return 1.0 + nrm((DEPTH, n), 0.02)

    G, P = SSM_GROUPS, SSM_STATE
    return {
        "x_prompt": nrm((BATCH, SEQ, D_MODEL), 1.0),
        "x_sample": nrm((DEC_BATCH, DEC_SEQ, D_MODEL), 1.0),
        "mem_prompt": nrm((BATCH, N_MEM, D_MODEL), 1.0),
        "cache_dsa_latent": nrm((DEPTH, DEC_BATCH, PAST_LEN, DSA_LATENT), 1.0),
        "cache_dsa_idx_k": nrm((DEPTH, DEC_BATCH, PAST_LEN, IDX_DIM), 1.0),
        "state_ssm_re": nrm((DEPTH, DEC_BATCH, G, P), 0.1),
        "state_ssm_im": nrm((DEPTH, DEC_BATCH, G, P), 0.1),
        "cache_mem_k": nrm((DEPTH, DEC_BATCH, N_MEM, MEM_HEADS, MEM_HEAD_DIM), 1.0),
        "cache_mem_v": nrm((DEPTH, DEC_BATCH, N_MEM, MEM_HEADS, MEM_HEAD_DIM), 1.0),
        "g_norm1": gain(D_MODEL),
        "w_in": nrm((DEPTH, D_MODEL, IN_WIDTH), D_MODEL ** -0.5),
        "g_kv": gain(DSA_LATENT),
        "w_uv": nrm((DEPTH, DSA_HEADS, DSA_LATENT, DSA_HEAD_DIM), DSA_LATENT ** -0.5),
        "ssm_lam_re": -0.5 * jnp.exp(nrm((DEPTH, G, P), 0.05)),
        "ssm_lam_im": jnp.pi * jnp.arange(P, dtype=F32) + nrm((DEPTH, G, P), 0.05),
        "ssm_log_dt": jax.random.uniform(next(keys), (DEPTH, G), F32, math.log(1e-3), math.log(1e-1)),
        "ssm_b_re": nrm((DEPTH, G, P, SSM_GROUP), (2 * SSM_GROUP) ** -0.5),
        "ssm_b_im": nrm((DEPTH, G, P, SSM_GROUP), (2 * SSM_GROUP) ** -0.5),
        "ssm_c_re": nrm((DEPTH, G, SSM_GROUP, P), 0.5),
        "ssm_c_im": nrm((DEPTH, G, SSM_GROUP, P), 0.5),
        "ssm_d": nrm((DEPTH, SSM_WIDTH), 1.0),
        "w_glu": nrm((DEPTH, SSM_WIDTH, SSM_WIDTH), SSM_WIDTH ** -0.5),
        "b_glu": nrm((DEPTH, SSM_WIDTH), 0.01),
        "g_mem": gain(D_MODEL),
        "w_mem_kv": nrm((DEPTH, D_MODEL, 2 * MEM_WIDTH), D_MODEL ** -0.5),
        "w_br_ssm": nrm((DEPTH, SSM_WIDTH, D_MODEL), SSM_WIDTH ** -0.5),
        "w_br_dsa": nrm((DEPTH, DSA_HEADS * DSA_HEAD_DIM, D_MODEL), (DSA_HEADS * DSA_HEAD_DIM) ** -0.5),
        "w_br_mem": nrm((DEPTH, MEM_WIDTH, D_MODEL), MEM_WIDTH ** -0.5),
        "w_out": nrm((DEPTH, D_MODEL, D_MODEL), D_MODEL ** -0.5),
        "g_norm2": gain(D_MODEL),
        "w_peer_q": nrm((DEPTH, D_MODEL, PEER_HEADS * PEER_QDIM), D_MODEL ** -0.5),
        "peer_sub_k1": nrm((DEPTH, PEER_HEADS, PEER_NKEYS, PEER_HALF), PEER_HALF ** -0.5),
        "peer_sub_k2": nrm((DEPTH, PEER_HEADS, PEER_NKEYS, PEER_HALF), PEER_HALF ** -0.5),
        "peer_u": nrm((DEPTH, PEER_EXPERTS, D_MODEL), D_MODEL ** -0.5),
        "peer_v": nrm((DEPTH, PEER_EXPERTS, D_MODEL), 0.25),
        "g_final": 1.0 + nrm((D_MODEL,), 0.02),
    }


def reference(x_prompt, x_sample, mem_prompt, cache_dsa_latent, cache_dsa_idx_k, state_ssm_re, state_ssm_im,
              cache_mem_k, cache_mem_v, g_norm1, w_in, g_kv, w_uv, ssm_lam_re, ssm_lam_im, ssm_log_dt,
              ssm_b_re, ssm_b_im, ssm_c_re, ssm_c_im, ssm_d, w_glu, b_glu, g_mem, w_mem_kv,
              w_br_ssm, w_br_dsa, w_br_mem, w_out, g_norm2, w_peer_q, peer_sub_k1, peer_sub_k2,
              peer_u, peer_v, g_final):
    hp, hs = x_prompt, x_sample
    lat_p, kid_p, sre_p, sim_p, mk_p, mv_p = [], [], [], [], [], []
    lat_s, kid_s, sre_s, sim_s = [], [], [], []
    for l in range(DEPTH):
        p = {
            "g_norm1": g_norm1[l], "w_in": w_in[l], "g_kv": g_kv[l], "w_uv": w_uv[l],
            "ssm_lam_re": ssm_lam_re[l], "ssm_lam_im": ssm_lam_im[l], "ssm_log_dt": ssm_log_dt[l],
            "ssm_b_re": ssm_b_re[l], "ssm_b_im": ssm_b_im[l], "ssm_c_re": ssm_c_re[l], "ssm_c_im": ssm_c_im[l],
            "ssm_d": ssm_d[l], "w_glu": w_glu[l], "b_glu": b_glu[l], "g_mem": g_mem[l], "w_mem_kv": w_mem_kv[l],
            "w_br_ssm": w_br_ssm[l], "w_br_dsa": w_br_dsa[l], "w_br_mem": w_br_mem[l], "w_out": w_out[l],
            "g_norm2": g_norm2[l], "w_peer_q": w_peer_q[l], "peer_sub_k1": peer_sub_k1[l],
            "peer_sub_k2": peer_sub_k2[l], "peer_u": peer_u[l], "peer_v": peer_v[l],
        }
        hp, c_p, k_p, r_p, i_p, mkp, mvp = layer_prompt(hp, mem_prompt, p)
        hs, c_s, k_s, r_s, i_s = layer_sample(hs, cache_dsa_latent[l], cache_dsa_idx_k[l], state_ssm_re[l],
                                              state_ssm_im[l], cache_mem_k[l], cache_mem_v[l], p)
        lat_p.append(c_p); kid_p.append(k_p); sre_p.append(r_p); sim_p.append(i_p); mk_p.append(mkp); mv_p.append(mvp)
        lat_s.append(c_s); kid_s.append(k_s); sre_s.append(r_s); sim_s.append(i_s)
    y_prompt = rmsnorm(hp, g_final)
    y_sample = rmsnorm(hs, g_final)
    return (y_prompt, y_sample,
            jnp.stack(lat_p), jnp.stack(kid_p), jnp.stack(sre_p), jnp.stack(sim_p), jnp.stack(mk_p), jnp.stack(mv_p),
            jnp.stack(lat_s), jnp.stack(kid_s), jnp.stack(sre_s), jnp.stack(sim_s))
```

```python
import functools
import math

import jax
import jax.numpy as jnp
import numpy as np
from jax import lax
from jax.experimental import pallas as pl
from jax.experimental.pallas import tpu as pltpu

F32 = jnp.float32
BF16 = jnp.bfloat16
I32 = jnp.int32

EPS = 1e-6
CHUNK = 64
SSM_GROUP = 16
SSM_STATE = 64
SSM_T = 32
DSA_HEADS = 8
DSA_LATENT = 256
DSA_HEAD_DIM = 128
IDX_HEADS = 8
IDX_DIM = 64
DSA_TOPK = 256
MEM_HEADS = 4
MEM_HEAD_DIM = 256
PEER_HEADS = 8
PEER_NKEYS = 128
PEER_HALF = 128
PEER_TOPK = 16
N_BRANCH = 3

LANES = 128
VMEM_LIMIT = 56 * 1024 * 1024
INT_MIN = -2 ** 31
NEG_BIG = -1e30


def _pick_tile(n, target):
    if n <= target:
        return n
    for t in range(target, 7, -1):
        if n % t == 0 and t % 8 == 0:
            return t
    return n


def _params(sem):
    return pltpu.CompilerParams(dimension_semantics=sem, vmem_limit_bytes=VMEM_LIMIT)


def _const_spec(shape):
    nd = len(shape)
    return pl.BlockSpec(shape, lambda *_: (0,) * nd, pipeline_mode=pl.Buffered(1))


def _rms(x, g):
    return x * lax.rsqrt(jnp.mean(x * x, axis=-1, keepdims=True) + EPS) * g


def _gelu(x):
    return 0.5 * x * (1.0 + jnp.tanh(0.7978845608028654 * (x + 0.044715 * x * x * x)))


def _sigmoid(x):
    return 1.0 / (1.0 + jnp.exp(-x))


def _dot_nt(a, b):
    return lax.dot_general(a, b, (((1,), (1,)), ((), ())), preferred_element_type=F32)


def _dot(a, b):
    return jnp.dot(a, b, preferred_element_type=F32)


def _inproj_kernel(x_ref, g1_ref, gkv_ref, kws_ref, wu_ref, wq_ref, wc_ref, wqi_ref, wkw_ref, wmq_ref, wg_ref,
                   u_ref, q_ref, c_ref, qi_ref, kw_ref, mq_ref, g_ref):
    xn = _rms(x_ref[...], g1_ref[...]).astype(BF16)
    u_ref[...] = _dot(xn, wu_ref[...]).astype(BF16)
    q_ref[...] = _dot(xn, wq_ref[...]).astype(BF16)
    c_ref[...] = _rms(_dot(xn, wc_ref[...]), gkv_ref[...])
    qi_ref[...] = _dot(xn, wqi_ref[...]).astype(BF16)
    kw_ref[...] = _dot(xn, wkw_ref[...]) * kws_ref[...]
    mq_ref[...] = _dot(xn, wmq_ref[...]).astype(BF16)
    g_ref[...] = _sigmoid(_dot(xn, wg_ref[...])).astype(BF16)


def _inproj(h, g1, gkv, kw_scale, ws):
    n, d = h.shape
    tm = _pick_tile(n, 256)
    widths = [w.shape[1] for w in ws]
    dts = [BF16, BF16, F32, BF16, F32, BF16, BF16]
    row = lambda w: pl.BlockSpec((tm, w), lambda i: (i, 0))
    return pl.pallas_call(
        _inproj_kernel,
        grid=(n // tm,),
        in_specs=[row(d), _const_spec(g1.shape), _const_spec(gkv.shape), _const_spec(kw_scale.shape)]
        + [_const_spec(w.shape) for w in ws],
        out_specs=[row(w) for w in widths],
        out_shape=[jax.ShapeDtypeStruct((n, w), dt) for w, dt in zip(widths, dts)],
        compiler_params=_params(("parallel",)),
        name="inproj",
    )(h, g1, gkv, kw_scale, *ws)


def _memkv_kernel(x_ref, g_ref, w_ref, o_ref):
    xn = _rms(x_ref[...], g_ref[...]).astype(BF16)
    o_ref[...] = _dot(xn, w_ref[...])


def _memkv(mem, g, w):
    n, d = mem.shape
    tm = _pick_tile(n, 256)
    return pl.pallas_call(
        _memkv_kernel,
        grid=(n // tm,),
        in_specs=[pl.BlockSpec((tm, d), lambda i: (i, 0)), _const_spec(g.shape), _const_spec(w.shape)],
        out_specs=pl.BlockSpec((tm, w.shape[1]), lambda i: (i, 0)),
        out_shape=jax.ShapeDtypeStruct((n, w.shape[1]), F32),
        compiler_params=_params(("parallel",)),
        name="memkv",
    )(mem, g, w)


def _memattn_kernel(q_ref, k_ref, v_ref, o_ref):
    for hd in range(MEM_HEADS):
        sl = slice(hd * MEM_HEAD_DIM, (hd + 1) * MEM_HEAD_DIM)
        logits = _dot_nt(q_ref[0, :, sl], k_ref[0, :, sl])
        m = jnp.max(logits, axis=-1, keepdims=True)
        p = jnp.exp(logits - m)
        l = jnp.sum(p, axis=-1, keepdims=True)
        o = _dot(p.astype(BF16), v_ref[0, :, sl]) / l
        o_ref[0, :, sl] = o.astype(BF16)


def _memattn(q, k, v):
    b, l, w = q.shape
    tl = _pick_tile(l, 512)
    nm = k.shape[1]
    return pl.pallas_call(
        _memattn_kernel,
        grid=(b, l // tl),
        in_specs=[pl.BlockSpec((1, tl, w), lambda i, j: (i, j, 0)),
                  pl.BlockSpec((1, nm, w), lambda i, j: (i, 0, 0)),
                  pl.BlockSpec((1, nm, w), lambda i, j: (i, 0, 0))],
        out_specs=pl.BlockSpec((1, tl, w), lambda i, j: (i, j, 0)),
        out_shape=jax.ShapeDtypeStruct((b, l, w), BF16),
        compiler_params=_params(("parallel", "parallel")),
        name="memattn",
    )(q, k, v)


def _merge_kernel(h_ref, ys_ref, yd_ref, ym_ref, g_ref, wglu_ref, bglu_ref, wbs_ref, wbd_ref, wbm_ref, wout_ref,
                  o_ref):
    d = h_ref.shape[1]
    z = _gelu(ys_ref[...].astype(F32))
    gate = _sigmoid(_dot(z.astype(BF16), wglu_ref[...]) + bglu_ref[...])
    a = _dot((z * gate).astype(BF16), wbs_ref[...])
    b = _dot(yd_ref[...], wbd_ref[...])
    c = _dot(ym_ref[...], wbm_ref[...])
    g = g_ref[...].astype(F32)
    merged = g[:, 0:d] * a + g[:, d:2 * d] * b + g[:, 2 * d:3 * d] * c
    o_ref[...] = h_ref[...] + _dot(merged.astype(BF16), wout_ref[...])


def _merge(h, ys, yd, ym, g, wglu, bglu, wbs, wbd, wbm, wout):
    n, d = h.shape
    tm = _pick_tile(n, 512)
    row = lambda w: pl.BlockSpec((tm, w), lambda i: (i, 0))
    consts = [wglu, bglu, wbs, wbd, wbm, wout]
    return pl.pallas_call(
        _merge_kernel,
        grid=(n // tm,),
        in_specs=[row(d), row(d), row(d), row(d), row(3 * d)] + [_const_spec(c.shape) for c in consts],
        out_specs=row(d),
        out_shape=jax.ShapeDtypeStruct((n, d), F32),
        compiler_params=_params(("parallel",)),
        name="merge",
    )(h, ys, yd, ym, g, *consts)


def _ssm_matrices(lam_re, lam_im, log_dt, b_re, b_im, c_re, c_im, d, t_len):
    hi = lax.Precision.HIGHEST
    g_n, p_n = lam_re.shape
    dt = jnp.exp(log_dt)[:, None]
    mag = jnp.exp(lam_re * dt)
    ar, ai = mag * jnp.cos(lam_im * dt), mag * jnp.sin(lam_im * dt)
    den = lam_re * lam_re + lam_im * lam_im
    nr, ni = ar - 1.0, ai
    kr = ((nr * lam_re + ni * lam_im) / den)[..., None]
    ki = ((ni * lam_re - nr * lam_im) / den)[..., None]
    bbr, bbi = kr * b_re - ki * b_im, kr * b_im + ki * b_re
    j = jnp.arange(t_len + 1, dtype=F32)[:, None, None]
    pmag = jnp.exp(j * (lam_re * dt))
    pr, pi = pmag * jnp.cos(j * (lam_im * dt)), pmag * jnp.sin(j * (lam_im * dt))
    mr = pr[:t_len, ..., None] * bbr - pi[:t_len, ..., None] * bbi
    mi = pr[:t_len, ..., None] * bbi + pi[:t_len, ..., None] * bbr
    kern = (jnp.einsum('gdp,jgpc->jgdc', c_re, mr, precision=hi)
            - jnp.einsum('gdp,jgpc->jgdc', c_im, mi, precision=hi))
    s_i = jnp.arange(t_len)[:, None]
    t_i = jnp.arange(t_len)[None, :]
    lag = t_i - s_i
    kg = jnp.where((lag >= 0)[:, :, None, None, None], kern[jnp.clip(lag, 0)], 0.0)
    tz = kg.transpose(2, 0, 4, 1, 3).reshape(g_n, t_len * SSM_GROUP, t_len * SSM_GROUP)
    pad = ((0, 0), (0, 0), (0, LANES - p_n))
    vr = jnp.pad(mr[::-1].transpose(1, 0, 3, 2).reshape(g_n, t_len * SSM_GROUP, p_n), pad)
    vi = jnp.pad(mi[::-1].transpose(1, 0, 3, 2).reshape(g_n, t_len * SSM_GROUP, p_n), pad)
    tzv = jnp.concatenate([tz, vr, vi], axis=-1).astype(BF16)
    pr1, pi1 = pr[1:].transpose(1, 2, 0), pi[1:].transpose(1, 2, 0)
    crt, cit = c_re.transpose(0, 2, 1), c_im.transpose(0, 2, 1)
    wre = crt[:, :, None, :] * pr1[..., None] - cit[:, :, None, :] * pi1[..., None]
    wim = -(crt[:, :, None, :] * pi1[..., None] + cit[:, :, None, :] * pr1[..., None])
    rpad = ((0, 0), (0, LANES - p_n), (0, 0))
    wre = jnp.pad(wre.reshape(g_n, p_n, -1), rpad).astype(BF16)
    wim = jnp.pad(wim.reshape(g_n, p_n, -1), rpad).astype(BF16)
    atr = jnp.pad(pr[t_len], ((0, 0), (0, LANES - p_n)))[:, None, :]
    ati = jnp.pad(pi[t_len], ((0, 0), (0, LANES - p_n)))[:, None, :]
    drow = jnp.tile(d.reshape(g_n, 1, SSM_GROUP), (1, t_len, 1)).reshape(g_n, 1, t_len * SSM_GROUP)
    return tzv, wre, wim, atr, ati, drow


def _ssm_kernel(nk, nb, u_ref, tzv_ref, wre_ref, wim_ref, atr_ref, ati_ref, d_ref, ire_ref, iim_ref,
                y_ref, fre_ref, fim_ref, yi_ref, sr_ref, si_ref, xr_ref, xi_ref):
    tc = u_ref.shape[2]
    u = u_ref[0]
    full = _dot(u, tzv_ref[0])
    yi_ref[...] = full[:, :tc]
    sr_ref[...] = full[:, tc:tc + LANES]
    si_ref[...] = full[:, tc + LANES:]
    atr, ati = atr_ref[0], ati_ref[0]

    def step(k, carry):
        xr, xi = carry
        rows = pl.ds(pl.multiple_of(k * nb, nb), nb)
        xr_ref[rows, :] = xr
        xi_ref[rows, :] = xi
        return (atr * xr - ati * xi + sr_ref[rows, :], atr * xi + ati * xr + si_ref[rows, :])

    xr, xi = lax.fori_loop(0, nk, step, (ire_ref[0], iim_ref[0]))
    fre_ref[0] = xr
    fim_ref[0] = xi
    y = (yi_ref[...] + _dot(xr_ref[...].astype(BF16), wre_ref[0]) + _dot(xi_ref[...].astype(BF16), wim_ref[0])
         + d_ref[0] * u.astype(F32))
    y_ref[0] = y.astype(BF16)


def _ssm(u, mats, init_re, init_im, nk, nb):
    tzv, wre, wim, atr, ati, drow = mats
    g_n, r, tc = u.shape
    per_g = lambda a: pl.BlockSpec((1,) + a.shape[1:], lambda g: (g, 0, 0))
    ins = [u, tzv, wre, wim, atr, ati, drow, init_re, init_im]
    st = jax.ShapeDtypeStruct((g_n, nb, LANES), F32)
    return pl.pallas_call(
        functools.partial(_ssm_kernel, nk, nb),
        grid=(g_n,),
        in_specs=[per_g(a) for a in ins],
        out_specs=[per_g(u), per_g(init_re), per_g(init_re)],
        out_shape=[jax.ShapeDtypeStruct(u.shape, BF16), st, st],
        scratch_shapes=[pltpu.VMEM((r, tc), F32)] + [pltpu.VMEM((r, LANES), F32)] * 4,
        compiler_params=_params(("parallel",)),
        name="ssm",
    )(*ins)


def _ssm_branch(zu, mats, st_re, st_im):
    b, l, w = zu.shape
    g_n, p_n = st_re.shape[1], st_re.shape[2]
    nk = l // SSM_T
    ug = zu.reshape(b, nk, SSM_T, g_n, SSM_GROUP).transpose(3, 1, 0, 2, 4).reshape(g_n, nk * b, SSM_T * SSM_GROUP)
    pad = ((0, 0), (0, 0), (0, LANES - p_n))
    ire = jnp.pad(st_re.transpose(1, 0, 2), pad)
    iim = jnp.pad(st_im.transpose(1, 0, 2), pad)
    y, fre, fim = _ssm(ug, mats, ire, iim, nk, b)
    y = y.reshape(g_n, nk, b, SSM_T, SSM_GROUP).transpose(2, 1, 3, 0, 4).reshape(b, l, w)
    return y, fre[:, :, :p_n].transpose(1, 0, 2), fim[:, :, :p_n].transpose(1, 0, 2)


def _dsa_kernel(tq, sk, pos0, s_valid, n_sel, qlat_ref, qidx_ref, w_ref, ka_ref, kb_ref, c_ref, wuv_ref,
                o_ref, key_ref, bias_ref, j_ref):
    s_pad = key_ref.shape[1]
    q0 = pos0 + pl.program_id(1) * tq
    rowpos = q0 + lax.broadcasted_iota(I32, (tq, 1), 0)
    vis = jnp.minimum((rowpos // CHUNK + 1) * CHUNK, s_valid)
    vis_max = jnp.minimum(((q0 + tq - 1) // CHUNK + 1) * CHUNK, s_valid)
    nch = (vis_max + sk - 1) // sk
    lane = lax.broadcasted_iota(I32, (tq, sk), 1)
    kslice = lambda j: pl.ds(pl.multiple_of(j * sk, sk), sk)

    def score_chunk(j, carry):
        ka, kb = ka_ref[0, kslice(j), :], kb_ref[0, kslice(j), :]
        acc = jnp.zeros((tq, sk), F32)
        for hd in range(IDX_HEADS):
            qh = qidx_ref[0, :, (hd // 2) * LANES:(hd // 2 + 1) * LANES]
            rel = jnp.maximum(_dot_nt(qh, ka if hd % 2 == 0 else kb), 0.0)
            acc = acc + w_ref[0, :, hd:hd + 1] * rel
        bits = pltpu.bitcast(acc, I32)
        key = bits ^ ((bits >> 31) & 0x7FFFFFFF)
        key_ref[:, kslice(j)] = jnp.where(j * sk + lane < vis, key, INT_MIN)
        return carry

    lax.fori_loop(0, nch, score_chunk, 0)

    def count(pred):
        def body(j, c):
            hit = pred(key_ref[:, kslice(j)], j * sk + lane)
            return c + jnp.sum(jnp.where(hit, 1.0, 0.0), axis=1, keepdims=True)
        return lax.fori_loop(0, nch, body, jnp.zeros((tq, 1), F32))

    def bit_step(it, thr):
        cand = thr ^ lax.shift_left(jnp.int32(1), 31 - it)
        return jnp.where(count(lambda k, col: k >= cand) >= n_sel, cand, thr)

    thr = lax.fori_loop(0, 32, bit_step, jnp.full((tq, 1), INT_MIN, I32))
    thr = jnp.maximum(thr, INT_MIN + 1)
    excess = count(lambda k, col: k >= thr) > n_sel

    j_ref[...] = jnp.full(j_ref.shape, s_pad, I32)

    @pl.when(jnp.max(jnp.where(excess, 1.0, 0.0)) > 0.0)
    def _():
        need = n_sel - count(lambda k, col: k > thr)

        def idx_step(_, lohi):
            lo, hi = lohi
            mid = (lo + hi) >> 1
            ok = count(lambda k, col: (k == thr) & (col < mid)) >= need
            return jnp.where(ok, lo, mid), jnp.where(ok, mid, hi)

        steps = int(math.ceil(math.log2(s_pad))) + 1
        _, hi = lax.fori_loop(0, steps, idx_step, (jnp.zeros((tq, 1), I32), jnp.full((tq, 1), s_pad, I32)))
        j_ref[...] = jnp.broadcast_to(jnp.where(excess, hi, s_pad), j_ref.shape)

    jlim = j_ref[:, 0:1]

    def bias_chunk(j, carry):
        k = key_ref[:, kslice(j)]
        sel = (k > thr) | ((k == thr) & (j * sk + lane < jlim))
        bias_ref[:, kslice(j)] = jnp.where(sel, 0.0, NEG_BIG)
        return carry

    lax.fori_loop(0, nch, bias_chunk, 0)

    rows = DSA_HEADS * tq
    q = jnp.concatenate([qlat_ref[0, :, hd * DSA_LATENT:(hd + 1) * DSA_LATENT] for hd in range(DSA_HEADS)], axis=0)

    def attend_chunk(j, carry):
        m, l, acc = carry
        cc = c_ref[0, kslice(j), :]
        s = _dot_nt(q, cc)
        s = (s.reshape(DSA_HEADS, tq, sk) + bias_ref[:, kslice(j)][None]).reshape(rows, sk)
        m_new = jnp.maximum(m, jnp.max(s, axis=1, keepdims=True))
        alpha = jnp.exp(m - m_new)
        p = jnp.exp(s - m_new)
        return m_new, alpha * l + jnp.sum(p, axis=1, keepdims=True), alpha * acc + _dot(p.astype(BF16), cc)

    init = (jnp.full((rows, 1), NEG_BIG, F32), jnp.zeros((rows, 1), F32), jnp.zeros((rows, DSA_LATENT), F32))
    _, l, acc = lax.fori_loop(0, nch, attend_chunk, init)
    o = (acc / l).astype(BF16)
    for hd in range(DSA_HEADS):
        o_ref[0, :, hd * DSA_HEAD_DIM:(hd + 1) * DSA_HEAD_DIM] = _dot(o[hd * tq:(hd + 1) * tq], wuv_ref[hd]).astype(BF16)


def _dsa(qlat, qidx, w, kidx, c, wuv, pos0, s_valid, n_sel, tq):
    b, l, _ = qlat.shape
    s_pad = c.shape[1]
    sk = min(1024, s_pad)
    ka = jnp.pad(kidx, ((0, 0), (0, 0), (0, LANES - IDX_DIM)))
    kb = jnp.pad(kidx, ((0, 0), (0, 0), (LANES - IDX_DIM, 0)))
    qspec = lambda a: pl.BlockSpec((1, tq, a.shape[2]), lambda i, j: (i, j, 0))
    kspec = lambda a: pl.BlockSpec((1,) + a.shape[1:], lambda i, j: (i, 0, 0))
    dh = wuv.shape[0] * wuv.shape[2]
    return pl.pallas_call(
        functools.partial(_dsa_kernel, tq, sk, pos0, s_valid, n_sel),
        grid=(b, l // tq),
        in_specs=[qspec(qlat), qspec(qidx), qspec(w), kspec(ka), kspec(kb), kspec(c), _const_spec(wuv.shape)],
        out_specs=pl.BlockSpec((1, tq, dh), lambda i, j: (i, j, 0)),
        out_shape=jax.ShapeDtypeStruct((b, l, dh), BF16),
        scratch_shapes=[pltpu.VMEM((tq, s_pad), I32), pltpu.VMEM((tq, s_pad), F32), pltpu.VMEM((tq, LANES), I32)],
        compiler_params=_params(("parallel", "arbitrary")),
        name="dsa",
    )(qlat, qidx, w, ka, kb, c, wuv)


_PEER_PAIRS = [(r, c) for r in range(PEER_TOPK) for c in range(PEER_TOPK) if (r + 1) * (c + 1) <= PEER_TOPK]


def _top_rows(s, k):
    row = lax.broadcasted_iota(I32, s.shape, 0)
    vals, idxs = [], []
    for _ in range(k):
        m = jnp.max(s, axis=0, keepdims=True)
        first = jnp.min(jnp.where(s == m, row, s.shape[0]), axis=0, keepdims=True)
        s = jnp.where(row == first, -jnp.inf, s)
        vals.append(m)
        idxs.append(first)
    return jnp.concatenate(vals, axis=0), jnp.concatenate(idxs, axis=0)


def _peer_route_kernel(h_ref, g_ref, wq_ref, k1_ref, k2_ref, xn_ref, i1_ref, i2_ref, gate_ref):
    tn = h_ref.shape[0]
    xn = _rms(h_ref[...], g_ref[...]).astype(BF16)
    xn_ref[...] = xn
    q = _dot(xn, wq_ref[...]).astype(BF16)
    out1, out2, outg = [], [], []
    for hd in range(PEER_HEADS):
        qa = q[:, (2 * hd) * PEER_HALF:(2 * hd + 1) * PEER_HALF]
        qb = q[:, (2 * hd + 1) * PEER_HALF:(2 * hd + 2) * PEER_HALF]
        v1, i1 = _top_rows(_dot_nt(k1_ref[hd], qa), PEER_TOPK)
        v2, i2 = _top_rows(_dot_nt(k2_ref[hd], qb), PEER_TOPK)
        cand = jnp.concatenate([v1[r:r + 1] + v2[c:c + 1] for r, c in _PEER_PAIRS], axis=0)
        c1 = jnp.concatenate([i1[r:r + 1] for r, _ in _PEER_PAIRS], axis=0)
        c2 = jnp.concatenate([i2[c:c + 1] for _, c in _PEER_PAIRS], axis=0)
        row = lax.broadcasted_iota(I32, cand.shape, 0)
        tops, e1, e2 = [], [], []
        for _ in range(PEER_TOPK):
            m = jnp.max(cand, axis=0, keepdims=True)
            first = jnp.min(jnp.where(cand == m, row, cand.shape[0]), axis=0, keepdims=True)
            hit = row == first
            e1.append(jnp.sum(jnp.where(hit, c1, 0), axis=0, keepdims=True))
            e2.append(jnp.sum(jnp.where(hit, c2, 0), axis=0, keepdims=True))
            cand = jnp.where(hit, -jnp.inf, cand)
            tops.append(m)
        top = jnp.concatenate(tops, axis=0)
        ex = jnp.exp(top - top[0:1])
        outg.append(ex / jnp.sum(ex, axis=0, keepdims=True))
        out1.append(jnp.concatenate(e1, axis=0))
        out2.append(jnp.concatenate(e2, axis=0))
    as_f32 = lambda parts: pltpu.bitcast(jnp.concatenate(parts, axis=0), F32)
    i1_ref[...] = pltpu.bitcast(as_f32(out1).T, I32)
    i2_ref[...] = pltpu.bitcast(as_f32(out2).T, I32)
    gate_ref[...] = jnp.concatenate(outg, axis=0).T


def _peer_route(h, g, wq, k1, k2):
    n, d = h.shape
    tn = LANES
    slots = PEER_HEADS * PEER_TOPK
    row = lambda w: pl.BlockSpec((tn, w), lambda i: (i, 0))
    return pl.pallas_call(
        _peer_route_kernel,
        grid=(n // tn,),
        in_specs=[row(d), _const_spec(g.shape), _const_spec(wq.shape), _const_spec(k1.shape), _const_spec(k2.shape)],
        out_specs=[row(d), row(slots), row(slots), row(slots)],
        out_shape=[jax.ShapeDtypeStruct((n, d), BF16), jax.ShapeDtypeStruct((n, slots), I32),
                   jax.ShapeDtypeStruct((n, slots), I32), jax.ShapeDtypeStruct((n, slots), F32)],
        compiler_params=_params(("parallel",)),
        name="peer_route",
    )(h, g, wq, k1, k2)


_PLANE_PAD = 8


def _peer_gates_kernel(i1_ref, i2_ref, gate_ref, a_ref, planes_ref):
    tn, slots = i1_ref.shape
    plane = tn + _PLANE_PAD
    sub = lax.broadcasted_iota(I32, (PEER_NKEYS, slots), 0)

    def token(n, carry):
        r = pl.ds(n, 1)
        pt = jnp.where(sub == i1_ref[r, :], gate_ref[r, :], 0.0).astype(BF16)
        qt = jnp.where(sub == i2_ref[r, :], 1.0, 0.0).astype(BF16)
        planes_ref[pl.ds(n, PEER_NKEYS, stride=plane), :] = _dot_nt(pt, qt)
        return carry

    lax.fori_loop(0, tn, token, 0)
    for k in range(PEER_NKEYS):
        a_ref[:, k * PEER_NKEYS:(k + 1) * PEER_NKEYS] = planes_ref[pl.ds(k * plane, tn), :].astype(BF16)


def _peer_gates(i1, i2, gate):
    n, slots = i1.shape
    tn = _pick_tile(n, 128)
    ne = PEER_NKEYS * PEER_NKEYS
    row = lambda w: pl.BlockSpec((tn, w), lambda i: (i, 0))
    return pl.pallas_call(
        _peer_gates_kernel,
        grid=(n // tn,),
        in_specs=[row(slots)] * 3,
        out_specs=row(ne),
        out_shape=jax.ShapeDtypeStruct((n, ne), BF16),
        scratch_shapes=[pltpu.VMEM((PEER_NKEYS * (tn + _PLANE_PAD), PEER_NKEYS), F32)],
        compiler_params=_params(("parallel",)),
        name="peer_gates",
    )(i1, i2, gate)


def _peer_apply_kernel(final_norm, xn_ref, a_ref, u_ref, v_ref, h_ref, g_ref, o_ref, acc_ref):
    j = pl.program_id(1)

    @pl.when(j == 0)
    def _():
        acc_ref[...] = jnp.zeros_like(acc_ref)

    act = _gelu(_dot_nt(xn_ref[...], u_ref[...]))
    acc_ref[...] += _dot((act * a_ref[...].astype(F32)).astype(BF16), v_ref[...])

    @pl.when(j == pl.num_programs(1) - 1)
    def _():
        y = h_ref[...] + acc_ref[...]
        o_ref[...] = _rms(y, g_ref[...]) if final_norm else y


def _peer_apply(xn, a, u, v, h, g, final_norm):
    n, d = xn.shape
    ne = u.shape[0]
    tn = _pick_tile(n, 512)
    te = 512
    return pl.pallas_call(
        functools.partial(_peer_apply_kernel, final_norm),
        grid=(n // tn, ne // te),
        in_specs=[pl.BlockSpec((tn, d), lambda i, j: (i, 0)), pl.BlockSpec((tn, te), lambda i, j: (i, j)),
                  pl.BlockSpec((te, d), lambda i, j: (j, 0)), pl.BlockSpec((te, d), lambda i, j: (j, 0)),
                  pl.BlockSpec((tn, d), lambda i, j: (i, 0)), _const_spec(g.shape)],
        out_specs=pl.BlockSpec((tn, d), lambda i, j: (i, 0)),
        out_shape=jax.ShapeDtypeStruct((n, d), F32),
        scratch_shapes=[pltpu.VMEM((tn, d), F32)],
        compiler_params=_params(("parallel", "arbitrary")),
        name="peer_apply",
    )(xn, a, u, v, h, g)


def _split_w_in(w_in, d):
    widths = (d, DSA_HEADS * DSA_LATENT, DSA_LATENT, IDX_HEADS * IDX_DIM, IDX_DIM, IDX_HEADS,
              MEM_HEADS * MEM_HEAD_DIM, N_BRANCH * d)
    offs = np.cumsum((0,) + widths)
    wu, wq, wc, wqi, wki, wwi, wmq, wg = [w_in[:, offs[i]:offs[i + 1]] for i in range(8)]
    wkw = jnp.pad(jnp.concatenate([wki, wwi], axis=1), ((0, 0), (0, LANES - IDX_DIM - IDX_HEADS)))
    kw_scale = jnp.concatenate([jnp.ones((IDX_DIM,), F32), jnp.full((IDX_HEADS,), IDX_HEADS ** -0.5, F32),
                                jnp.zeros((LANES - IDX_DIM - IDX_HEADS,), F32)])[None, :]
    ws = [wu, wq * DSA_LATENT ** -0.5, wc, wqi, wkw, wmq * MEM_HEAD_DIM ** -0.5, wg]
    return [w.astype(BF16) for w in ws], kw_scale


def _pad_rows(a, n):
    return a if a.shape[0] == n else jnp.pad(a, ((0, n - a.shape[0]),) + ((0, 0),) * (a.ndim - 1))


def _pad_keys(a, s_pad):
    return jnp.pad(a, ((0, 0), (0, s_pad - a.shape[1]), (0, 0)))


def kernel(x_prompt, x_sample, mem_prompt, cache_dsa_latent, cache_dsa_idx_k, state_ssm_re, state_ssm_im, cache_mem_k, cache_mem_v, g_norm1, w_in, g_kv, w_uv, ssm_lam_re, ssm_lam_im, ssm_log_dt, ssm_b_re, ssm_b_im, ssm_c_re, ssm_c_im, ssm_d, w_glu, b_glu, g_mem, w_mem_kv, w_br_ssm, w_br_dsa, w_br_mem, w_out, g_norm2, w_peer_q, peer_sub_k1, peer_sub_k2, peer_u, peer_v, g_final):
    depth = w_in.shape[0]
    bp, lp, d = x_prompt.shape
    bs, ls, _ = x_sample.shape
    past = cache_dsa_latent.shape[2]
    n_p, n_s = bp * lp, bs * ls
    n_tok = -(-(n_p + n_s) // LANES) * LANES
    mem_w = MEM_HEADS * MEM_HEAD_DIM
    n_mem = mem_prompt.shape[1]
    bf = lambda a: a.astype(BF16)
    row = lambda a: a[None, :]

    h = _pad_rows(jnp.concatenate([x_prompt.reshape(n_p, d), x_sample.reshape(n_s, d)], axis=0), n_tok)
    outs = [[] for _ in range(10)]
    for l in range(depth):
        ws, kw_scale = _split_w_in(w_in[l], d)
        u, q, c, qi, kw, mq, g = _inproj(h, row(g_norm1[l]), row(g_kv[l]), kw_scale, ws)
        split = lambda a: (a[:n_p].reshape(bp, lp, -1), a[n_p:n_p + n_s].reshape(bs, ls, -1))
        (u_p, u_s), (q_p, q_s), (c_p, c_s), (qi_p, qi_s), (mq_p, mq_s) = map(split, (u, q, c, qi, mq))
        ki_p, ki_s = split(kw[:, :IDX_DIM])
        wi_p, wi_s = split(kw[:, IDX_DIM:IDX_DIM + IDX_HEADS])

        mats = _ssm_matrices(ssm_lam_re[l], ssm_lam_im[l], ssm_log_dt[l], ssm_b_re[l], ssm_b_im[l],
                             ssm_c_re[l], ssm_c_im[l], ssm_d[l], SSM_T)
        zeros = jnp.zeros((bp,) + state_ssm_re.shape[2:], F32)
        ys_p, sre_p, sim_p = _ssm_branch(u_p, mats, zeros, zeros)
        ys_s, sre_s, sim_s = _ssm_branch(u_s, mats, state_ssm_re[l], state_ssm_im[l])

        wuv = bf(w_uv[l])
        yd_p = _dsa(q_p, qi_p, wi_p, bf(ki_p), bf(c_p), wuv, 0, lp, min(DSA_TOPK, lp // 4), _pick_tile(lp, 128))
        s_all = past + ls
        s_pad = -(-s_all // 1024) * 1024
        c_all = _pad_keys(jnp.concatenate([bf(cache_dsa_latent[l]), bf(c_s)], axis=1), s_pad)
        k_all = _pad_keys(jnp.concatenate([bf(cache_dsa_idx_k[l]), bf(ki_s)], axis=1), s_pad)
        yd_s = _dsa(q_s, qi_s, wi_s, k_all, c_all, wuv, past, s_all, min(DSA_TOPK, s_all // 4), ls)

        kv = _memkv(mem_prompt.reshape(bp * n_mem, d), row(g_mem[l]), bf(w_mem_kv[l]))
        mk_p, mv_p = kv[:, :mem_w].reshape(bp, n_mem, mem_w), kv[:, mem_w:].reshape(bp, n_mem, mem_w)
        ym_p = _memattn(mq_p, bf(mk_p), bf(mv_p))
        ym_s = _memattn(mq_s, bf(cache_mem_k[l].reshape(bs, n_mem, mem_w)), bf(cache_mem_v[l].reshape(bs, n_mem, mem_w)))

        join = lambda a_p, a_s: _pad_rows(jnp.concatenate([a_p.reshape(n_p, -1), a_s.reshape(n_s, -1)], axis=0), n_tok)
        h2 = _merge(h, join(ys_p, ys_s), join(yd_p, yd_s), join(ym_p, ym_s), g, bf(w_glu[l]), row(b_glu[l]),
                    bf(w_br_ssm[l]), bf(w_br_dsa[l]), bf(w_br_mem[l]), bf(w_out[l]))

        xn, i1, i2, gate = _peer_route(h2, row(g_norm2[l]), bf(w_peer_q[l]), bf(peer_sub_k1[l]), bf(peer_sub_k2[l]))
        a = _peer_gates(i1, i2, gate)
        h = _peer_apply(xn, a, bf(peer_u[l]), bf(peer_v[l]), h2, row(g_final), l == depth - 1)

        heads = (n_mem, MEM_HEADS, MEM_HEAD_DIM)
        for lst, val in zip(outs, (c_p, ki_p, sre_p, sim_p, mk_p.reshape((bp,) + heads), mv_p.reshape((bp,) + heads),
                                   c_s, ki_s, sre_s, sim_s)):
            lst.append(val)
    y_p = h[:n_p].reshape(bp, lp, d)
    y_s = h[n_p:n_p + n_s].reshape(bs, ls, d)
    return (y_p, y_s) + tuple(jnp.stack(o) for o in outs)
```

```python
import functools
import math

import jax
import jax.numpy as jnp
import numpy as np
from jax import lax
from jax.experimental import pallas as pl
from jax.experimental.pallas import tpu as pltpu

F32 = jnp.float32
BF16 = jnp.bfloat16
I32 = jnp.int32

EPS = 1e-6
CHUNK = 64
SSM_GROUP = 16
SSM_STATE = 64
SSM_T = 32
DSA_HEADS = 8
DSA_LATENT = 256
DSA_HEAD_DIM = 128
IDX_HEADS = 8
IDX_DIM = 64
DSA_TOPK = 256
MEM_HEADS = 4
MEM_HEAD_DIM = 256
PEER_HEADS = 8
PEER_NKEYS = 128
PEER_HALF = 128
PEER_TOPK = 16
N_BRANCH = 3

LANES = 128
VMEM_LIMIT = 56 * 1024 * 1024
INT_MIN = -2 ** 31
NEG_BIG = -1e30
LOG2E = 1.4426950408889634


def _pick_tile(n, target):
    if n <= target:
        return n
    for t in range(target, 7, -1):
        if n % t == 0 and t % 8 == 0:
            return t
    return n


def _params(sem):
    return pltpu.CompilerParams(dimension_semantics=sem, vmem_limit_bytes=VMEM_LIMIT)


def _const_spec(shape):
    nd = len(shape)
    return pl.BlockSpec(shape, lambda *_: (0,) * nd, pipeline_mode=pl.Buffered(1))


def _rms(x, g):
    return x * lax.rsqrt(jnp.mean(x * x, axis=-1, keepdims=True) + EPS) * g


def _gelu(x):
    return 0.5 * x * (1.0 + jnp.tanh(0.7978845608028654 * (x + 0.044715 * x * x * x)))


def _sigmoid(x):
    return 1.0 / (1.0 + jnp.exp(-x))


def _dot_nt(a, b):
    return lax.dot_general(a, b, (((1,), (1,)), ((), ())), preferred_element_type=F32)


def _dot(a, b):
    return jnp.dot(a, b, preferred_element_type=F32)


def _inproj_kernel(x_ref, g1_ref, gkv_ref, kws_ref, wu_ref, wq_ref, wc_ref, wqi_ref, wkw_ref, wmq_ref, wg_ref,
                   u_ref, q_ref, c_ref, qi_ref, kw_ref, mq_ref, g_ref):
    xn = _rms(x_ref[...], g1_ref[...]).astype(BF16)
    u_ref[...] = _dot(xn, wu_ref[...]).astype(BF16)
    q_ref[...] = _dot(xn, wq_ref[...]).astype(BF16)
    c_ref[...] = _rms(_dot(xn, wc_ref[...]), gkv_ref[...])
    qi_ref[...] = _dot(xn, wqi_ref[...]).astype(BF16)
    kw_ref[...] = _dot(xn, wkw_ref[...]) * kws_ref[...]
    mq_ref[...] = _dot(xn, wmq_ref[...]).astype(BF16)
    g_ref[...] = _sigmoid(_dot(xn, wg_ref[...])).astype(BF16)


def _inproj(h, g1, gkv, kw_scale, ws):
    n, d = h.shape
    tm = _pick_tile(n, 256)
    widths = [w.shape[1] for w in ws]
    dts = [BF16, BF16, F32, BF16, F32, BF16, BF16]
    row = lambda w: pl.BlockSpec((tm, w), lambda i: (i, 0))
    return pl.pallas_call(
        _inproj_kernel,
        grid=(n // tm,),
        in_specs=[row(d), _const_spec(g1.shape), _const_spec(gkv.shape), _const_spec(kw_scale.shape)]
        + [_const_spec(w.shape) for w in ws],
        out_specs=[row(w) for w in widths],
        out_shape=[jax.ShapeDtypeStruct((n, w), dt) for w, dt in zip(widths, dts)],
        compiler_params=_params(("parallel",)),
        name="inproj",
    )(h, g1, gkv, kw_scale, *ws)


def _memkv_kernel(x_ref, g_ref, w_ref, o_ref):
    xn = _rms(x_ref[...], g_ref[...]).astype(BF16)
    o_ref[...] = _dot(xn, w_ref[...])


def _memkv(mem, g, w):
    n, d = mem.shape
    tm = _pick_tile(n, 256)
    return pl.pallas_call(
        _memkv_kernel,
        grid=(n // tm,),
        in_specs=[pl.BlockSpec((tm, d), lambda i: (i, 0)), _const_spec(g.shape), _const_spec(w.shape)],
        out_specs=pl.BlockSpec((tm, w.shape[1]), lambda i: (i, 0)),
        out_shape=jax.ShapeDtypeStruct((n, w.shape[1]), F32),
        compiler_params=_params(("parallel",)),
        name="memkv",
    )(mem, g, w)


def _memattn_kernel(q_ref, k_ref, v_ref, o_ref):
    for hd in range(MEM_HEADS):
        sl = slice(hd * MEM_HEAD_DIM, (hd + 1) * MEM_HEAD_DIM)
        logits = _dot_nt(q_ref[0, :, sl], k_ref[0, :, sl])
        m = jnp.max(logits, axis=-1, keepdims=True)
        p = jnp.exp(logits - m)
        l = jnp.sum(p, axis=-1, keepdims=True)
        o = _dot(p.astype(BF16), v_ref[0, :, sl]) / l
        o_ref[0, :, sl] = o.astype(BF16)


def _memattn(q, k, v):
    b, l, w = q.shape
    tl = _pick_tile(l, 512)
    nm = k.shape[1]
    return pl.pallas_call(
        _memattn_kernel,
        grid=(b, l // tl),
        in_specs=[pl.BlockSpec((1, tl, w), lambda i, j: (i, j, 0)),
                  pl.BlockSpec((1, nm, w), lambda i, j: (i, 0, 0)),
                  pl.BlockSpec((1, nm, w), lambda i, j: (i, 0, 0))],
        out_specs=pl.BlockSpec((1, tl, w), lambda i, j: (i, j, 0)),
        out_shape=jax.ShapeDtypeStruct((b, l, w), BF16),
        compiler_params=_params(("parallel", "parallel")),
        name="memattn",
    )(q, k, v)


def _merge_kernel(h_ref, ys_ref, yd_ref, ym_ref, g_ref, wglu_ref, bglu_ref, wbs_ref, wbd_ref, wbm_ref, wout_ref,
                  o_ref):
    d = h_ref.shape[1]
    z = _gelu(ys_ref[...].astype(F32))
    gate = _sigmoid(_dot(z.astype(BF16), wglu_ref[...]) + bglu_ref[...])
    a = _dot((z * gate).astype(BF16), wbs_ref[...])
    b = _dot(yd_ref[...], wbd_ref[...])
    c = _dot(ym_ref[...], wbm_ref[...])
    g = g_ref[...].astype(F32)
    merged = g[:, 0:d] * a + g[:, d:2 * d] * b + g[:, 2 * d:3 * d] * c
    o_ref[...] = h_ref[...] + _dot(merged.astype(BF16), wout_ref[...])


def _merge(h, ys, yd, ym, g, wglu, bglu, wbs, wbd, wbm, wout):
    n, d = h.shape
    tm = _pick_tile(n, 512)
    row = lambda w: pl.BlockSpec((tm, w), lambda i: (i, 0))
    consts = [wglu, bglu, wbs, wbd, wbm, wout]
    return pl.pallas_call(
        _merge_kernel,
        grid=(n // tm,),
        in_specs=[row(d), row(d), row(d), row(d), row(3 * d)] + [_const_spec(c.shape) for c in consts],
        out_specs=row(d),
        out_shape=jax.ShapeDtypeStruct((n, d), F32),
        compiler_params=_params(("parallel",)),
        name="merge",
    )(h, ys, yd, ym, g, *consts)


def _ssm_matrices(lam_re, lam_im, log_dt, b_re, b_im, c_re, c_im, d, t_len):
    hi = lax.Precision.HIGHEST
    g_n, p_n = lam_re.shape
    dt = jnp.exp(log_dt)[:, None]
    mag = jnp.exp(lam_re * dt)
    ar, ai = mag * jnp.cos(lam_im * dt), mag * jnp.sin(lam_im * dt)
    den = lam_re * lam_re + lam_im * lam_im
    nr, ni = ar - 1.0, ai
    kr = ((nr * lam_re + ni * lam_im) / den)[..., None]
    ki = ((ni * lam_re - nr * lam_im) / den)[..., None]
    bbr, bbi = kr * b_re - ki * b_im, kr * b_im + ki * b_re
    j = jnp.arange(t_len + 1, dtype=F32)[:, None, None]
    pmag = jnp.exp(j * (lam_re * dt))
    pr, pi = pmag * jnp.cos(j * (lam_im * dt)), pmag * jnp.sin(j * (lam_im * dt))
    mr = pr[:t_len, ..., None] * bbr - pi[:t_len, ..., None] * bbi
    mi = pr[:t_len, ..., None] * bbi + pi[:t_len, ..., None] * bbr
    kern = (jnp.einsum('gdp,jgpc->jgdc', c_re, mr, precision=hi)
            - jnp.einsum('gdp,jgpc->jgdc', c_im, mi, precision=hi))
    s_i = jnp.arange(t_len)[:, None]
    t_i = jnp.arange(t_len)[None, :]
    lag = t_i - s_i
    kg = jnp.where((lag >= 0)[:, :, None, None, None], kern[jnp.clip(lag, 0)], 0.0)
    tz = kg.transpose(2, 0, 4, 1, 3).reshape(g_n, t_len * SSM_GROUP, t_len * SSM_GROUP)
    pad = ((0, 0), (0, 0), (0, LANES - p_n))
    vr = jnp.pad(mr[::-1].transpose(1, 0, 3, 2).reshape(g_n, t_len * SSM_GROUP, p_n), pad)
    vi = jnp.pad(mi[::-1].transpose(1, 0, 3, 2).reshape(g_n, t_len * SSM_GROUP, p_n), pad)
    tzv = jnp.concatenate([tz, vr, vi], axis=-1).astype(BF16)
    pr1, pi1 = pr[1:].transpose(1, 2, 0), pi[1:].transpose(1, 2, 0)
    crt, cit = c_re.transpose(0, 2, 1), c_im.transpose(0, 2, 1)
    wre = crt[:, :, None, :] * pr1[..., None] - cit[:, :, None, :] * pi1[..., None]
    wim = -(crt[:, :, None, :] * pi1[..., None] + cit[:, :, None, :] * pr1[..., None])
    rpad = ((0, 0), (0, LANES - p_n), (0, 0))
    wre = jnp.pad(wre.reshape(g_n, p_n, -1), rpad).astype(BF16)
    wim = jnp.pad(wim.reshape(g_n, p_n, -1), rpad).astype(BF16)
    atr = jnp.pad(pr[t_len], ((0, 0), (0, LANES - p_n)))[:, None, :]
    ati = jnp.pad(pi[t_len], ((0, 0), (0, LANES - p_n)))[:, None, :]
    drow = jnp.tile(d.reshape(g_n, 1, SSM_GROUP), (1, t_len, 1)).reshape(g_n, 1, t_len * SSM_GROUP)
    return tzv, wre, wim, atr, ati, drow


def _ssm_kernel(nk, nb, u_ref, tzv_ref, wre_ref, wim_ref, atr_ref, ati_ref, d_ref, ire_ref, iim_ref,
                y_ref, fre_ref, fim_ref, yi_ref, sr_ref, si_ref, xr_ref, xi_ref):
    tc = u_ref.shape[2]
    u = u_ref[0]
    full = _dot(u, tzv_ref[0])
    yi_ref[...] = full[:, :tc]
    sr_ref[...] = full[:, tc:tc + LANES]
    si_ref[...] = full[:, tc + LANES:]
    atr, ati = atr_ref[0], ati_ref[0]

    def step(k, carry):
        xr, xi = carry
        rows = pl.ds(pl.multiple_of(k * nb, nb), nb)
        xr_ref[rows, :] = xr
        xi_ref[rows, :] = xi
        return (atr * xr - ati * xi + sr_ref[rows, :], atr * xi + ati * xr + si_ref[rows, :])

    xr, xi = lax.fori_loop(0, nk, step, (ire_ref[0], iim_ref[0]))
    fre_ref[0] = xr
    fim_ref[0] = xi
    y = (yi_ref[...] + _dot(xr_ref[...].astype(BF16), wre_ref[0]) + _dot(xi_ref[...].astype(BF16), wim_ref[0])
         + d_ref[0] * u.astype(F32))
    y_ref[0] = y.astype(BF16)


def _ssm(u, mats, init_re, init_im, nk, nb):
    tzv, wre, wim, atr, ati, drow = mats
    g_n, r, tc = u.shape
    per_g = lambda a: pl.BlockSpec((1,) + a.shape[1:], lambda g: (g, 0, 0))
    ins = [u, tzv, wre, wim, atr, ati, drow, init_re, init_im]
    st = jax.ShapeDtypeStruct((g_n, nb, LANES), F32)
    return pl.pallas_call(
        functools.partial(_ssm_kernel, nk, nb),
        grid=(g_n,),
        in_specs=[per_g(a) for a in ins],
        out_specs=[per_g(u), per_g(init_re), per_g(init_re)],
        out_shape=[jax.ShapeDtypeStruct(u.shape, BF16), st, st],
        scratch_shapes=[pltpu.VMEM((r, tc), F32)] + [pltpu.VMEM((r, LANES), F32)] * 4,
        compiler_params=_params(("parallel",)),
        name="ssm",
    )(*ins)


def _ssm_branch(zu, mats, st_re, st_im):
    b, l, w = zu.shape
    g_n, p_n = st_re.shape[1], st_re.shape[2]
    nk = l // SSM_T
    ug = zu.reshape(b, nk, SSM_T, g_n, SSM_GROUP).transpose(3, 1, 0, 2, 4).reshape(g_n, nk * b, SSM_T * SSM_GROUP)
    pad = ((0, 0), (0, 0), (0, LANES - p_n))
    ire = jnp.pad(st_re.transpose(1, 0, 2), pad)
    iim = jnp.pad(st_im.transpose(1, 0, 2), pad)
    y, fre, fim = _ssm(ug, mats, ire, iim, nk, b)
    y = y.reshape(g_n, nk, b, SSM_T, SSM_GROUP).transpose(2, 1, 3, 0, 4).reshape(b, l, w)
    return y, fre[:, :, :p_n].transpose(1, 0, 2), fim[:, :, :p_n].transpose(1, 0, 2)


def _dsa_kernel(tq, sk, pos0, s_valid, n_sel, qlat_ref, qidx_ref, w_ref, ka_ref, kb_ref, c_ref, wuv_ref,
                o_ref, key_ref, bias_ref, j_ref):
    s_pad = key_ref.shape[1]
    q0 = pos0 + pl.program_id(1) * tq
    rowpos = q0 + lax.broadcasted_iota(I32, (tq, 1), 0)
    vis = jnp.minimum((rowpos // CHUNK + 1) * CHUNK, s_valid)
    vis_max = jnp.minimum(((q0 + tq - 1) // CHUNK + 1) * CHUNK, s_valid)
    nch = (vis_max + sk - 1) // sk
    lane = lax.broadcasted_iota(I32, (tq, sk), 1)
    kslice = lambda j: pl.ds(pl.multiple_of(j * sk, sk), sk)

    def score_chunk(j, carry):
        ka, kb = ka_ref[0, kslice(j), :], kb_ref[0, kslice(j), :]
        acc = jnp.zeros((tq, sk), F32)
        for hd in range(IDX_HEADS):
            qh = qidx_ref[0, :, (hd // 2) * LANES:(hd // 2 + 1) * LANES]
            rel = jnp.maximum(_dot_nt(qh, ka if hd % 2 == 0 else kb), 0.0)
            acc = acc + w_ref[0, :, hd:hd + 1] * rel
        bits = pltpu.bitcast(acc, I32)
        key = bits ^ ((bits >> 31) & 0x7FFFFFFF)
        key_ref[:, kslice(j)] = jnp.where(j * sk + lane < vis, key, INT_MIN)
        return carry

    lax.fori_loop(0, nch, score_chunk, 0)

    nblk = sk // LANES

    def count(pred):
        def body(j, c):
            hit = jnp.where(pred(key_ref[:, kslice(j)], j * sk + lane), 1.0, 0.0)
            for i in range(nblk):
                c = c + hit[:, i * LANES:(i + 1) * LANES]
            return c
        return jnp.sum(lax.fori_loop(0, nch, body, jnp.zeros((tq, LANES), F32)), axis=1, keepdims=True)

    def bit_step(state):
        it, thr, cnt, _ = state
        cand = thr ^ lax.shift_left(jnp.int32(1), 31 - it)
        c = count(lambda k, col: k >= cand)
        ok = c >= n_sel
        cnt = jnp.where(ok, c, cnt)
        return it + 1, jnp.where(ok, cand, thr), cnt, (jnp.max(cnt) > n_sel).astype(I32)

    cnt0 = count(lambda k, col: k > INT_MIN)
    state = (jnp.int32(0), jnp.full((tq, 1), INT_MIN, I32), cnt0, (jnp.max(cnt0) > n_sel).astype(I32))
    _, thr, cnt, _ = lax.while_loop(lambda st: (st[0] < 32) & (st[3] > 0), bit_step, state)
    thr = jnp.maximum(thr, INT_MIN + 1)
    excess = cnt > n_sel

    j_ref[...] = jnp.full(j_ref.shape, s_pad, I32)

    @pl.when(jnp.max(jnp.where(excess, 1.0, 0.0)) > 0.0)
    def _():
        need = n_sel - count(lambda k, col: k > thr)

        def idx_step(_, lohi):
            lo, hi = lohi
            mid = (lo + hi) >> 1
            ok = count(lambda k, col: (k == thr) & (col < mid)) >= need
            return jnp.where(ok, lo, mid), jnp.where(ok, mid, hi)

        steps = int(math.ceil(math.log2(s_pad))) + 1
        _, hi = lax.fori_loop(0, steps, idx_step, (jnp.zeros((tq, 1), I32), jnp.full((tq, 1), s_pad, I32)))
        j_ref[...] = jnp.broadcast_to(jnp.where(excess, hi, s_pad), j_ref.shape)

    jlim = j_ref[:, 0:1]

    def bias_chunk(j, carry):
        k = key_ref[:, kslice(j)]
        sel = (k > thr) | ((k == thr) & (j * sk + lane < jlim))
        bias_ref[:, kslice(j)] = jnp.where(sel, 0.0, NEG_BIG)
        return carry

    lax.fori_loop(0, nch, bias_chunk, 0)

    rows = DSA_HEADS * tq
    q = jnp.concatenate([qlat_ref[0, :, hd * DSA_LATENT:(hd + 1) * DSA_LATENT] for hd in range(DSA_HEADS)], axis=0)

    wb = min(sk, 2 * LANES)

    def logits(j, i):
        blk = pl.ds(pl.multiple_of(j * sk + i * wb, wb), wb)
        s = _dot_nt(q, c_ref[0, blk, :])
        return (s.reshape(DSA_HEADS, tq, wb) + bias_ref[:, blk][None]).reshape(rows, wb)

    def max_chunk(j, mx):
        for i in range(sk // wb):
            s = logits(j, i)
            for t in range(wb // LANES):
                mx = jnp.maximum(mx, s[:, t * LANES:(t + 1) * LANES])
        return mx

    mx = lax.fori_loop(0, nch, max_chunk, jnp.full((rows, LANES), NEG_BIG, F32))
    m = jnp.broadcast_to(jnp.max(mx, axis=1, keepdims=True), (rows, LANES))

    def acc_chunk(j, carry):
        l, acc = carry
        ps = []
        for i in range(sk // wb):
            s = logits(j, i)
            for t in range(wb // LANES):
                p = jnp.exp2(s[:, t * LANES:(t + 1) * LANES] - m)
                l = l + p
                ps.append(p.astype(BF16))
        return l, acc + _dot(jnp.concatenate(ps, axis=1), c_ref[0, kslice(j), :])

    l, acc = lax.fori_loop(0, nch, acc_chunk, (jnp.zeros((rows, LANES), F32), jnp.zeros((rows, DSA_LATENT), F32)))
    o = (acc / jnp.sum(l, axis=1, keepdims=True)).astype(BF16)
    for hd in range(DSA_HEADS):
        o_ref[0, :, hd * DSA_HEAD_DIM:(hd + 1) * DSA_HEAD_DIM] = _dot(o[hd * tq:(hd + 1) * tq], wuv_ref[hd]).astype(BF16)


def _dsa(qlat, qidx, w, kidx, c, wuv, pos0, s_valid, n_sel, tq):
    b, l, _ = qlat.shape
    s_pad = c.shape[1]
    sk = min(1024, s_pad)
    ka = jnp.pad(kidx, ((0, 0), (0, 0), (0, LANES - IDX_DIM)))
    kb = jnp.pad(kidx, ((0, 0), (0, 0), (LANES - IDX_DIM, 0)))
    qspec = lambda a: pl.BlockSpec((1, tq, a.shape[2]), lambda i, j: (i, j, 0))
    kspec = lambda a: pl.BlockSpec((1,) + a.shape[1:], lambda i, j: (i, 0, 0))
    dh = wuv.shape[0] * wuv.shape[2]
    return pl.pallas_call(
        functools.partial(_dsa_kernel, tq, sk, pos0, s_valid, n_sel),
        grid=(b, l // tq),
        in_specs=[qspec(qlat), qspec(qidx), qspec(w), kspec(ka), kspec(kb), kspec(c), _const_spec(wuv.shape)],
        out_specs=pl.BlockSpec((1, tq, dh), lambda i, j: (i, j, 0)),
        out_shape=jax.ShapeDtypeStruct((b, l, dh), BF16),
        scratch_shapes=[pltpu.VMEM((tq, s_pad), I32), pltpu.VMEM((tq, s_pad), F32), pltpu.VMEM((tq, LANES), I32)],
        compiler_params=_params(("parallel", "arbitrary")),
        name="dsa",
    )(qlat, qidx, w, ka, kb, c, wuv)


_PEER_PAIRS = [(r, c) for r in range(PEER_TOPK) for c in range(PEER_TOPK) if (r + 1) * (c + 1) <= PEER_TOPK]


def _top_rows(s, k):
    row = lax.broadcasted_iota(I32, s.shape, 0)
    vals, idxs = [], []
    for _ in range(k):
        m = jnp.max(s, axis=0, keepdims=True)
        first = jnp.min(jnp.where(s == m, row, s.shape[0]), axis=0, keepdims=True)
        s = jnp.where(row == first, -jnp.inf, s)
        vals.append(m)
        idxs.append(first)
    return jnp.concatenate(vals, axis=0), jnp.concatenate(idxs, axis=0)


def _peer_route_kernel(h_ref, g_ref, wq_ref, k1_ref, k2_ref, xn_ref, i1_ref, i2_ref, gate_ref):
    tn = h_ref.shape[0]
    xn = _rms(h_ref[...], g_ref[...]).astype(BF16)
    xn_ref[...] = xn
    q = _dot(xn, wq_ref[...]).astype(BF16)
    out1, out2, outg = [], [], []
    for hd in range(PEER_HEADS):
        qa = q[:, (2 * hd) * PEER_HALF:(2 * hd + 1) * PEER_HALF]
        qb = q[:, (2 * hd + 1) * PEER_HALF:(2 * hd + 2) * PEER_HALF]
        v1, i1 = _top_rows(_dot_nt(k1_ref[hd], qa), PEER_TOPK)
        v2, i2 = _top_rows(_dot_nt(k2_ref[hd], qb), PEER_TOPK)
        cand = jnp.concatenate([v1[r:r + 1] + v2[c:c + 1] for r, c in _PEER_PAIRS], axis=0)
        c1 = jnp.concatenate([i1[r:r + 1] for r, _ in _PEER_PAIRS], axis=0)
        c2 = jnp.concatenate([i2[c:c + 1] for _, c in _PEER_PAIRS], axis=0)
        row = lax.broadcasted_iota(I32, cand.shape, 0)
        tops, e1, e2 = [], [], []
        for _ in range(PEER_TOPK):
            m = jnp.max(cand, axis=0, keepdims=True)
            first = jnp.min(jnp.where(cand == m, row, cand.shape[0]), axis=0, keepdims=True)
            hit = row == first
            e1.append(jnp.sum(jnp.where(hit, c1, 0), axis=0, keepdims=True))
            e2.append(jnp.sum(jnp.where(hit, c2, 0), axis=0, keepdims=True))
            cand = jnp.where(hit, -jnp.inf, cand)
            tops.append(m)
        top = jnp.concatenate(tops, axis=0)
        ex = jnp.exp(top - top[0:1])
        outg.append(ex / jnp.sum(ex, axis=0, keepdims=True))
        out1.append(jnp.concatenate(e1, axis=0))
        out2.append(jnp.concatenate(e2, axis=0))
    as_f32 = lambda parts: pltpu.bitcast(jnp.concatenate(parts, axis=0), F32)
    i1_ref[...] = pltpu.bitcast(as_f32(out1).T, I32)
    i2_ref[...] = pltpu.bitcast(as_f32(out2).T, I32)
    gate_ref[...] = jnp.concatenate(outg, axis=0).T


def _peer_route(h, g, wq, k1, k2):
    n, d = h.shape
    tn = LANES
    slots = PEER_HEADS * PEER_TOPK
    row = lambda w: pl.BlockSpec((tn, w), lambda i: (i, 0))
    return pl.pallas_call(
        _peer_route_kernel,
        grid=(n // tn,),
        in_specs=[row(d), _const_spec(g.shape), _const_spec(wq.shape), _const_spec(k1.shape), _const_spec(k2.shape)],
        out_specs=[row(d), row(slots), row(slots), row(slots)],
        out_shape=[jax.ShapeDtypeStruct((n, d), BF16), jax.ShapeDtypeStruct((n, slots), I32),
                   jax.ShapeDtypeStruct((n, slots), I32), jax.ShapeDtypeStruct((n, slots), F32)],
        compiler_params=_params(("parallel",)),
        name="peer_route",
    )(h, g, wq, k1, k2)


_PLANE_PAD = 8


def _peer_gates_kernel(i1_ref, i2_ref, gate_ref, a_ref, planes_ref):
    tn, slots = i1_ref.shape
    plane = tn + _PLANE_PAD
    sub = lax.broadcasted_iota(I32, (PEER_NKEYS, slots), 0)

    def token(n, carry):
        r = pl.ds(n, 1)
        pt = jnp.where(sub == i1_ref[r, :], gate_ref[r, :], 0.0).astype(BF16)
        qt = jnp.where(sub == i2_ref[r, :], 1.0, 0.0).astype(BF16)
        planes_ref[pl.ds(n, PEER_NKEYS, stride=plane), :] = _dot_nt(pt, qt)
        return carry

    lax.fori_loop(0, tn, token, 0, unroll=8)
    for k in range(PEER_NKEYS):
        a_ref[:, k * PEER_NKEYS:(k + 1) * PEER_NKEYS] = planes_ref[pl.ds(k * plane, tn), :].astype(BF16)


def _peer_gates(i1, i2, gate):
    n, slots = i1.shape
    tn = _pick_tile(n, 128)
    ne = PEER_NKEYS * PEER_NKEYS
    row = lambda w: pl.BlockSpec((tn, w), lambda i: (i, 0))
    return pl.pallas_call(
        _peer_gates_kernel,
        grid=(n // tn,),
        in_specs=[row(slots)] * 3,
        out_specs=row(ne),
        out_shape=jax.ShapeDtypeStruct((n, ne), BF16),
        scratch_shapes=[pltpu.VMEM((PEER_NKEYS * (tn + _PLANE_PAD), PEER_NKEYS), F32)],
        compiler_params=_params(("parallel",)),
        name="peer_gates",
    )(i1, i2, gate)


def _peer_apply_kernel(final_norm, xn_ref, a_ref, u_ref, v_ref, h_ref, g_ref, o_ref, acc_ref):
    j = pl.program_id(1)

    @pl.when(j == 0)
    def _():
        acc_ref[...] = jnp.zeros_like(acc_ref)

    act = _gelu(_dot_nt(xn_ref[...], u_ref[...]))
    acc_ref[...] += _dot((act * a_ref[...].astype(F32)).astype(BF16), v_ref[...])

    @pl.when(j == pl.num_programs(1) - 1)
    def _():
        y = h_ref[...] + acc_ref[...]
        o_ref[...] = _rms(y, g_ref[...]) if final_norm else y


def _peer_apply(xn, a, u, v, h, g, final_norm):
    n, d = xn.shape
    ne = u.shape[0]
    tn = _pick_tile(n, 512)
    te = 512
    return pl.pallas_call(
        functools.partial(_peer_apply_kernel, final_norm),
        grid=(n // tn, ne // te),
        in_specs=[pl.BlockSpec((tn, d), lambda i, j: (i, 0)), pl.BlockSpec((tn, te), lambda i, j: (i, j)),
                  pl.BlockSpec((te, d), lambda i, j: (j, 0)), pl.BlockSpec((te, d), lambda i, j: (j, 0)),
                  pl.BlockSpec((tn, d), lambda i, j: (i, 0)), _const_spec(g.shape)],
        out_specs=pl.BlockSpec((tn, d), lambda i, j: (i, 0)),
        out_shape=jax.ShapeDtypeStruct((n, d), F32),
        scratch_shapes=[pltpu.VMEM((tn, d), F32)],
        compiler_params=_params(("parallel", "arbitrary")),
        name="peer_apply",
    )(xn, a, u, v, h, g)


def _split_w_in(w_in, d):
    widths = (d, DSA_HEADS * DSA_LATENT, DSA_LATENT, IDX_HEADS * IDX_DIM, IDX_DIM, IDX_HEADS,
              MEM_HEADS * MEM_HEAD_DIM, N_BRANCH * d)
    offs = np.cumsum((0,) + widths)
    wu, wq, wc, wqi, wki, wwi, wmq, wg = [w_in[:, offs[i]:offs[i + 1]] for i in range(8)]
    wkw = jnp.pad(jnp.concatenate([wki, wwi], axis=1), ((0, 0), (0, LANES - IDX_DIM - IDX_HEADS)))
    kw_scale = jnp.concatenate([jnp.ones((IDX_DIM,), F32), jnp.full((IDX_HEADS,), IDX_HEADS ** -0.5, F32),
                                jnp.zeros((LANES - IDX_DIM - IDX_HEADS,), F32)])[None, :]
    ws = [wu, wq * (DSA_LATENT ** -0.5 * LOG2E), wc, wqi, wkw, wmq * MEM_HEAD_DIM ** -0.5, wg]
    return [w.astype(BF16) for w in ws], kw_scale


def _pad_rows(a, n):
    return a if a.shape[0] == n else jnp.pad(a, ((0, n - a.shape[0]),) + ((0, 0),) * (a.ndim - 1))


def _pad_keys(a, s_pad):
    return jnp.pad(a, ((0, 0), (0, s_pad - a.shape[1]), (0, 0)))


def kernel(x_prompt, x_sample, mem_prompt, cache_dsa_latent, cache_dsa_idx_k, state_ssm_re, state_ssm_im, cache_mem_k, cache_mem_v, g_norm1, w_in, g_kv, w_uv, ssm_lam_re, ssm_lam_im, ssm_log_dt, ssm_b_re, ssm_b_im, ssm_c_re, ssm_c_im, ssm_d, w_glu, b_glu, g_mem, w_mem_kv, w_br_ssm, w_br_dsa, w_br_mem, w_out, g_norm2, w_peer_q, peer_sub_k1, peer_sub_k2, peer_u, peer_v, g_final):
    depth = w_in.shape[0]
    bp, lp, d = x_prompt.shape
    bs, ls, _ = x_sample.shape
    past = cache_dsa_latent.shape[2]
    n_p, n_s = bp * lp, bs * ls
    n_tok = -(-(n_p + n_s) // LANES) * LANES
    mem_w = MEM_HEADS * MEM_HEAD_DIM
    n_mem = mem_prompt.shape[1]
    bf = lambda a: a.astype(BF16)
    row = lambda a: a[None, :]

    h = _pad_rows(jnp.concatenate([x_prompt.reshape(n_p, d), x_sample.reshape(n_s, d)], axis=0), n_tok)
    outs = [[] for _ in range(10)]
    for l in range(depth):
        ws, kw_scale = _split_w_in(w_in[l], d)
        u, q, c, qi, kw, mq, g = _inproj(h, row(g_norm1[l]), row(g_kv[l]), kw_scale, ws)
        split = lambda a: (a[:n_p].reshape(bp, lp, -1), a[n_p:n_p + n_s].reshape(bs, ls, -1))
        (u_p, u_s), (q_p, q_s), (c_p, c_s), (qi_p, qi_s), (mq_p, mq_s) = map(split, (u, q, c, qi, mq))
        ki_p, ki_s = split(kw[:, :IDX_DIM])
        wi_p, wi_s = split(kw[:, IDX_DIM:IDX_DIM + IDX_HEADS])

        mats = _ssm_matrices(ssm_lam_re[l], ssm_lam_im[l], ssm_log_dt[l], ssm_b_re[l], ssm_b_im[l],
                             ssm_c_re[l], ssm_c_im[l], ssm_d[l], SSM_T)
        zeros = jnp.zeros((bp,) + state_ssm_re.shape[2:], F32)
        ys_p, sre_p, sim_p = _ssm_branch(u_p, mats, zeros, zeros)
        ys_s, sre_s, sim_s = _ssm_branch(u_s, mats, state_ssm_re[l], state_ssm_im[l])

        wuv = bf(w_uv[l])
        yd_p = _dsa(q_p, qi_p, wi_p, bf(ki_p), bf(c_p), wuv, 0, lp, min(DSA_TOPK, lp // 4), _pick_tile(lp, 128))
        s_all = past + ls
        s_pad = -(-s_all // 1024) * 1024
        c_all = _pad_keys(jnp.concatenate([bf(cache_dsa_latent[l]), bf(c_s)], axis=1), s_pad)
        k_all = _pad_keys(jnp.concatenate([bf(cache_dsa_idx_k[l]), bf(ki_s)], axis=1), s_pad)
        yd_s = _dsa(q_s, qi_s, wi_s, k_all, c_all, wuv, past, s_all, min(DSA_TOPK, s_all // 4), ls)

        kv = _memkv(mem_prompt.reshape(bp * n_mem, d), row(g_mem[l]), bf(w_mem_kv[l]))
        mk_p, mv_p = kv[:, :mem_w].reshape(bp, n_mem, mem_w), kv[:, mem_w:].reshape(bp, n_mem, mem_w)
        ym_p = _memattn(mq_p, bf(mk_p), bf(mv_p))
        ym_s = _memattn(mq_s, bf(cache_mem_k[l].reshape(bs, n_mem, mem_w)), bf(cache_mem_v[l].reshape(bs, n_mem, mem_w)))

        join = lambda a_p, a_s: _pad_rows(jnp.concatenate([a_p.reshape(n_p, -1), a_s.reshape(n_s, -1)], axis=0), n_tok)
        h2 = _merge(h, join(ys_p, ys_s), join(yd_p, yd_s), join(ym_p, ym_s), g, bf(w_glu[l]), row(b_glu[l]),
                    bf(w_br_ssm[l]), bf(w_br_dsa[l]), bf(w_br_mem[l]), bf(w_out[l]))

        xn, i1, i2, gate = _peer_route(h2, row(g_norm2[l]), bf(w_peer_q[l]), bf(peer_sub_k1[l]), bf(peer_sub_k2[l]))
        a = _peer_gates(i1, i2, gate)
        h = _peer_apply(xn, a, bf(peer_u[l]), bf(peer_v[l]), h2, row(g_final), l == depth - 1)

        heads = (n_mem, MEM_HEADS, MEM_HEAD_DIM)
        for lst, val in zip(outs, (c_p, ki_p, sre_p, sim_p, mk_p.reshape((bp,) + heads), mv_p.reshape((bp,) + heads),
                                   c_s, ki_s, sre_s, sim_s)):
            lst.append(val)
    y_p = h[:n_p].reshape(bp, lp, d)
    y_s = h[n_p:n_p + n_s].reshape(bs, ls, d)
    return (y_p, y_s) + tuple(jnp.stack(o) for o in outs)
```

```python
import functools
import math

import jax
import jax.numpy as jnp
import numpy as np
from jax import lax
from jax.experimental import pallas as pl
from jax.experimental.pallas import tpu as pltpu

F32 = jnp.float32
BF16 = jnp.bfloat16
I32 = jnp.int32

EPS = 1e-6
CHUNK = 64
SSM_GROUP = 16
SSM_STATE = 64
SSM_T = 32
DSA_HEADS = 8
DSA_LATENT = 256
DSA_HEAD_DIM = 128
IDX_HEADS = 8
IDX_DIM = 64
DSA_TOPK = 256
MEM_HEADS = 4
MEM_HEAD_DIM = 256
PEER_HEADS = 8
PEER_NKEYS = 128
PEER_HALF = 128
PEER_TOPK = 16
N_BRANCH = 3

LANES = 128
VMEM_LIMIT = 56 * 1024 * 1024
INT_MIN = -2 ** 31
NEG_BIG = -1e30
LOG2E = 1.4426950408889634


def _pick_tile(n, target):
    if n <= target:
        return n
    for t in range(target, 7, -1):
        if n % t == 0 and t % 8 == 0:
            return t
    return n


def _params(sem):
    return pltpu.CompilerParams(dimension_semantics=sem, vmem_limit_bytes=VMEM_LIMIT)


def _const_spec(shape):
    nd = len(shape)
    return pl.BlockSpec(shape, lambda *_: (0,) * nd, pipeline_mode=pl.Buffered(1))


def _rms(x, g):
    return x * lax.rsqrt(jnp.mean(x * x, axis=-1, keepdims=True) + EPS) * g


def _gelu(x):
    return 0.5 * x * (1.0 + jnp.tanh(0.7978845608028654 * (x + 0.044715 * x * x * x)))


def _sigmoid(x):
    return 1.0 / (1.0 + jnp.exp(-x))


def _dot_nt(a, b):
    return lax.dot_general(a, b, (((1,), (1,)), ((), ())), preferred_element_type=F32)


def _dot(a, b):
    return jnp.dot(a, b, preferred_element_type=F32)


def _inproj_kernel(x_ref, g1_ref, gkv_ref, kws_ref, wu_ref, wq_ref, wc_ref, wqi_ref, wkw_ref, wmq_ref, wg_ref,
                   u_ref, q_ref, c_ref, qi_ref, kw_ref, mq_ref, g_ref):
    xn = _rms(x_ref[...], g1_ref[...]).astype(BF16)
    u_ref[...] = _dot(xn, wu_ref[...]).astype(BF16)
    q_ref[...] = _dot(xn, wq_ref[...]).astype(BF16)
    c_ref[...] = _rms(_dot(xn, wc_ref[...]), gkv_ref[...])
    qi_ref[...] = _dot(xn, wqi_ref[...]).astype(BF16)
    kw_ref[...] = _dot(xn, wkw_ref[...]) * kws_ref[...]
    mq_ref[...] = _dot(xn, wmq_ref[...]).astype(BF16)
    g_ref[...] = _sigmoid(_dot(xn, wg_ref[...])).astype(BF16)


def _inproj(h, g1, gkv, kw_scale, ws):
    n, d = h.shape
    tm = _pick_tile(n, 256)
    widths = [w.shape[1] for w in ws]
    dts = [BF16, BF16, F32, BF16, F32, BF16, BF16]
    row = lambda w: pl.BlockSpec((tm, w), lambda i: (i, 0))
    return pl.pallas_call(
        _inproj_kernel,
        grid=(n // tm,),
        in_specs=[row(d), _const_spec(g1.shape), _const_spec(gkv.shape), _const_spec(kw_scale.shape)]
        + [_const_spec(w.shape) for w in ws],
        out_specs=[row(w) for w in widths],
        out_shape=[jax.ShapeDtypeStruct((n, w), dt) for w, dt in zip(widths, dts)],
        compiler_params=_params(("parallel",)),
        name="inproj",
    )(h, g1, gkv, kw_scale, *ws)


def _memkv_kernel(x_ref, g_ref, w_ref, o_ref):
    xn = _rms(x_ref[...], g_ref[...]).astype(BF16)
    o_ref[...] = _dot(xn, w_ref[...])


def _memkv(mem, g, w):
    n, d = mem.shape
    tm = _pick_tile(n, 256)
    return pl.pallas_call(
        _memkv_kernel,
        grid=(n // tm,),
        in_specs=[pl.BlockSpec((tm, d), lambda i: (i, 0)), _const_spec(g.shape), _const_spec(w.shape)],
        out_specs=pl.BlockSpec((tm, w.shape[1]), lambda i: (i, 0)),
        out_shape=jax.ShapeDtypeStruct((n, w.shape[1]), F32),
        compiler_params=_params(("parallel",)),
        name="memkv",
    )(mem, g, w)


def _memattn_kernel(q_ref, k_ref, v_ref, o_ref):
    for hd in range(MEM_HEADS):
        sl = slice(hd * MEM_HEAD_DIM, (hd + 1) * MEM_HEAD_DIM)
        logits = _dot_nt(q_ref[0, :, sl], k_ref[0, :, sl])
        m = jnp.max(logits, axis=-1, keepdims=True)
        p = jnp.exp(logits - m)
        l = jnp.sum(p, axis=-1, keepdims=True)
        o = _dot(p.astype(BF16), v_ref[0, :, sl]) / l
        o_ref[0, :, sl] = o.astype(BF16)


def _memattn(q, k, v):
    b, l, w = q.shape
    tl = _pick_tile(l, 512)
    nm = k.shape[1]
    return pl.pallas_call(
        _memattn_kernel,
        grid=(b, l // tl),
        in_specs=[pl.BlockSpec((1, tl, w), lambda i, j: (i, j, 0)),
                  pl.BlockSpec((1, nm, w), lambda i, j: (i, 0, 0)),
                  pl.BlockSpec((1, nm, w), lambda i, j: (i, 0, 0))],
        out_specs=pl.BlockSpec((1, tl, w), lambda i, j: (i, j, 0)),
        out_shape=jax.ShapeDtypeStruct((b, l, w), BF16),
        compiler_params=_params(("parallel", "parallel")),
        name="memattn",
    )(q, k, v)


def _merge_kernel(h_ref, ys_ref, yd_ref, ym_ref, g_ref, wglu_ref, bglu_ref, wbs_ref, wbd_ref, wbm_ref, wout_ref,
                  o_ref):
    d = h_ref.shape[1]
    z = _gelu(ys_ref[...].astype(F32))
    gate = _sigmoid(_dot(z.astype(BF16), wglu_ref[...]) + bglu_ref[...])
    a = _dot((z * gate).astype(BF16), wbs_ref[...])
    b = _dot(yd_ref[...], wbd_ref[...])
    c = _dot(ym_ref[...], wbm_ref[...])
    g = g_ref[...].astype(F32)
    merged = g[:, 0:d] * a + g[:, d:2 * d] * b + g[:, 2 * d:3 * d] * c
    o_ref[...] = h_ref[...] + _dot(merged.astype(BF16), wout_ref[...])


def _merge(h, ys, yd, ym, g, wglu, bglu, wbs, wbd, wbm, wout):
    n, d = h.shape
    tm = _pick_tile(n, 512)
    row = lambda w: pl.BlockSpec((tm, w), lambda i: (i, 0))
    consts = [wglu, bglu, wbs, wbd, wbm, wout]
    return pl.pallas_call(
        _merge_kernel,
        grid=(n // tm,),
        in_specs=[row(d), row(d), row(d), row(d), row(3 * d)] + [_const_spec(c.shape) for c in consts],
        out_specs=row(d),
        out_shape=jax.ShapeDtypeStruct((n, d), F32),
        compiler_params=_params(("parallel",)),
        name="merge",
    )(h, ys, yd, ym, g, *consts)


def _ssm_matrices(lam_re, lam_im, log_dt, b_re, b_im, c_re, c_im, d, t_len):
    hi = lax.Precision.HIGHEST
    g_n, p_n = lam_re.shape
    dt = jnp.exp(log_dt)[:, None]
    mag = jnp.exp(lam_re * dt)
    ar, ai = mag * jnp.cos(lam_im * dt), mag * jnp.sin(lam_im * dt)
    den = lam_re * lam_re + lam_im * lam_im
    nr, ni = ar - 1.0, ai
    kr = ((nr * lam_re + ni * lam_im) / den)[..., None]
    ki = ((ni * lam_re - nr * lam_im) / den)[..., None]
    bbr, bbi = kr * b_re - ki * b_im, kr * b_im + ki * b_re
    j = jnp.arange(t_len + 1, dtype=F32)[:, None, None]
    pmag = jnp.exp(j * (lam_re * dt))
    pr, pi = pmag * jnp.cos(j * (lam_im * dt)), pmag * jnp.sin(j * (lam_im * dt))
    mr = pr[:t_len, ..., None] * bbr - pi[:t_len, ..., None] * bbi
    mi = pr[:t_len, ..., None] * bbi + pi[:t_len, ..., None] * bbr
    kern = (jnp.einsum('gdp,jgpc->jgdc', c_re, mr, precision=hi)
            - jnp.einsum('gdp,jgpc->jgdc', c_im, mi, precision=hi))
    s_i = jnp.arange(t_len)[:, None]
    t_i = jnp.arange(t_len)[None, :]
    lag = t_i - s_i
    kg = jnp.where((lag >= 0)[:, :, None, None, None], kern[jnp.clip(lag, 0)], 0.0)
    tz = kg.transpose(2, 0, 4, 1, 3).reshape(g_n, t_len * SSM_GROUP, t_len * SSM_GROUP)
    pad = ((0, 0), (0, 0), (0, LANES - p_n))
    vr = jnp.pad(mr[::-1].transpose(1, 0, 3, 2).reshape(g_n, t_len * SSM_GROUP, p_n), pad)
    vi = jnp.pad(mi[::-1].transpose(1, 0, 3, 2).reshape(g_n, t_len * SSM_GROUP, p_n), pad)
    tzv = jnp.concatenate([tz, vr, vi], axis=-1).astype(BF16)
    pr1, pi1 = pr[1:].transpose(1, 2, 0), pi[1:].transpose(1, 2, 0)
    crt, cit = c_re.transpose(0, 2, 1), c_im.transpose(0, 2, 1)
    wre = crt[:, :, None, :] * pr1[..., None] - cit[:, :, None, :] * pi1[..., None]
    wim = -(crt[:, :, None, :] * pi1[..., None] + cit[:, :, None, :] * pr1[..., None])
    rpad = ((0, 0), (0, LANES - p_n), (0, 0))
    wre = jnp.pad(wre.reshape(g_n, p_n, -1), rpad).astype(BF16)
    wim = jnp.pad(wim.reshape(g_n, p_n, -1), rpad).astype(BF16)
    atr = jnp.pad(pr[t_len], ((0, 0), (0, LANES - p_n)))[:, None, :]
    ati = jnp.pad(pi[t_len], ((0, 0), (0, LANES - p_n)))[:, None, :]
    drow = jnp.tile(d.reshape(g_n, 1, SSM_GROUP), (1, t_len, 1)).reshape(g_n, 1, t_len * SSM_GROUP)
    return tzv, wre, wim, atr, ati, drow


def _ssm_kernel(nk, nb, u_ref, tzv_ref, wre_ref, wim_ref, atr_ref, ati_ref, d_ref, ire_ref, iim_ref,
                y_ref, fre_ref, fim_ref, yi_ref, sr_ref, si_ref, xr_ref, xi_ref):
    tc = u_ref.shape[2]
    u = u_ref[0]
    full = _dot(u, tzv_ref[0])
    yi_ref[...] = full[:, :tc]
    sr_ref[...] = full[:, tc:tc + LANES]
    si_ref[...] = full[:, tc + LANES:]
    atr, ati = atr_ref[0], ati_ref[0]

    def step(k, carry):
        xr, xi = carry
        rows = pl.ds(pl.multiple_of(k * nb, nb), nb)
        xr_ref[rows, :] = xr
        xi_ref[rows, :] = xi
        return (atr * xr - ati * xi + sr_ref[rows, :], atr * xi + ati * xr + si_ref[rows, :])

    xr, xi = lax.fori_loop(0, nk, step, (ire_ref[0], iim_ref[0]))
    fre_ref[0] = xr
    fim_ref[0] = xi
    y = (yi_ref[...] + _dot(xr_ref[...].astype(BF16), wre_ref[0]) + _dot(xi_ref[...].astype(BF16), wim_ref[0])
         + d_ref[0] * u.astype(F32))
    y_ref[0] = y.astype(BF16)


def _ssm(u, mats, init_re, init_im, nk, nb):
    tzv, wre, wim, atr, ati, drow = mats
    g_n, r, tc = u.shape
    per_g = lambda a: pl.BlockSpec((1,) + a.shape[1:], lambda g: (g, 0, 0))
    ins = [u, tzv, wre, wim, atr, ati, drow, init_re, init_im]
    st = jax.ShapeDtypeStruct((g_n, nb, LANES), F32)
    return pl.pallas_call(
        functools.partial(_ssm_kernel, nk, nb),
        grid=(g_n,),
        in_specs=[per_g(a) for a in ins],
        out_specs=[per_g(u), per_g(init_re), per_g(init_re)],
        out_shape=[jax.ShapeDtypeStruct(u.shape, BF16), st, st],
        scratch_shapes=[pltpu.VMEM((r, tc), F32)] + [pltpu.VMEM((r, LANES), F32)] * 4,
        compiler_params=_params(("parallel",)),
        name="ssm",
    )(*ins)


def _ssm_branch(zu, mats, st_re, st_im):
    b, l, w = zu.shape
    g_n, p_n = st_re.shape[1], st_re.shape[2]
    nk = l // SSM_T
    ug = zu.reshape(b, nk, SSM_T, g_n, SSM_GROUP).transpose(3, 1, 0, 2, 4).reshape(g_n, nk * b, SSM_T * SSM_GROUP)
    pad = ((0, 0), (0, 0), (0, LANES - p_n))
    ire = jnp.pad(st_re.transpose(1, 0, 2), pad)
    iim = jnp.pad(st_im.transpose(1, 0, 2), pad)
    y, fre, fim = _ssm(ug, mats, ire, iim, nk, b)
    y = y.reshape(g_n, nk, b, SSM_T, SSM_GROUP).transpose(2, 1, 3, 0, 4).reshape(b, l, w)
    return y, fre[:, :, :p_n].transpose(1, 0, 2), fim[:, :, :p_n].transpose(1, 0, 2)


_DSA_TQ = 256


def _dsa_kernel(tq, sk, pos0, s_valid, n_sel, qlat_ref, qidx_ref, w_ref, ka_ref, c_ref, wuv_ref,
                o_ref, key_ref, bias_ref, j_ref):
    s_pad = key_ref.shape[1]
    q0 = pos0 + pl.program_id(1) * tq
    rowpos = q0 + lax.broadcasted_iota(I32, (tq, 1), 0)
    vis = jnp.minimum((rowpos // CHUNK + 1) * CHUNK, s_valid)
    vis_max = jnp.minimum(((q0 + tq - 1) // CHUNK + 1) * CHUNK, s_valid)
    nch = (vis_max + sk - 1) // sk
    lane = lax.broadcasted_iota(I32, (tq, sk), 1)
    kslice = lambda j: pl.ds(pl.multiple_of(j * sk, sk), sk)

    qs = jnp.concatenate([qidx_ref[0, :, hd * LANES:(hd + 1) * LANES] for hd in range(IDX_HEADS)], axis=0)

    def score_chunk(j, carry):
        rel = jnp.maximum(_dot_nt(qs, ka_ref[0, kslice(j), :]), 0.0)
        acc = jnp.zeros((tq, sk), F32)
        for hd in range(IDX_HEADS):
            acc = acc + w_ref[0, :, hd:hd + 1] * rel[hd * tq:(hd + 1) * tq]
        bits = pltpu.bitcast(acc, I32)
        key = bits ^ ((bits >> 31) & 0x7FFFFFFF)
        key_ref[:, kslice(j)] = jnp.where(j * sk + lane < vis, key, INT_MIN)
        return carry

    lax.fori_loop(0, nch, score_chunk, 0)

    nblk = sk // LANES

    def count(pred):
        def body(j, c):
            hit = jnp.where(pred(key_ref[:, kslice(j)], j * sk + lane), 1.0, 0.0)
            for i in range(nblk):
                c = c + hit[:, i * LANES:(i + 1) * LANES]
            return c
        return jnp.sum(lax.fori_loop(0, nch, body, jnp.zeros((tq, LANES), F32)), axis=1, keepdims=True)

    def bit_step(state):
        it, thr, cnt, _ = state
        cand = thr ^ lax.shift_left(jnp.int32(1), 31 - it)
        c = count(lambda k, col: k >= cand)
        ok = c >= n_sel
        cnt = jnp.where(ok, c, cnt)
        return it + 1, jnp.where(ok, cand, thr), cnt, (jnp.max(cnt) > n_sel).astype(I32)

    cnt0 = count(lambda k, col: k > INT_MIN)
    state = (jnp.int32(0), jnp.full((tq, 1), INT_MIN, I32), cnt0, (jnp.max(cnt0) > n_sel).astype(I32))
    _, thr, cnt, _ = lax.while_loop(lambda st: (st[0] < 32) & (st[3] > 0), bit_step, state)
    thr = jnp.maximum(thr, INT_MIN + 1)
    excess = cnt > n_sel

    j_ref[...] = jnp.full(j_ref.shape, s_pad, I32)

    @pl.when(jnp.max(jnp.where(excess, 1.0, 0.0)) > 0.0)
    def _():
        need = n_sel - count(lambda k, col: k > thr)

        def idx_step(_, lohi):
            lo, hi = lohi
            mid = (lo + hi) >> 1
            ok = count(lambda k, col: (k == thr) & (col < mid)) >= need
            return jnp.where(ok, lo, mid), jnp.where(ok, mid, hi)

        steps = int(math.ceil(math.log2(s_pad))) + 1
        _, hi = lax.fori_loop(0, steps, idx_step, (jnp.zeros((tq, 1), I32), jnp.full((tq, 1), s_pad, I32)))
        j_ref[...] = jnp.broadcast_to(jnp.where(excess, hi, s_pad), j_ref.shape)

    jlim = j_ref[:, 0:1]

    rows = DSA_HEADS * tq
    q = jnp.concatenate([qlat_ref[0, :, hd * DSA_LATENT:(hd + 1) * DSA_LATENT] for hd in range(DSA_HEADS)], axis=0)

    wb = min(sk, 2 * LANES)

    lane_wb = lax.broadcasted_iota(I32, (tq, wb), 1)

    def logits(j, i, first):
        blk = pl.ds(pl.multiple_of(j * sk + i * wb, wb), wb)
        if first:
            k = key_ref[:, blk]
            sel = (k > thr) | ((k == thr) & (j * sk + i * wb + lane_wb < jlim))
            bias = jnp.where(sel, 0.0, NEG_BIG)
            bias_ref[:, blk] = bias
        else:
            bias = bias_ref[:, blk]
        s = _dot_nt(q, c_ref[0, blk, :])
        return (s.reshape(DSA_HEADS, tq, wb) + bias[None]).reshape(rows, wb)

    def max_chunk(j, mx):
        for i in range(sk // wb):
            s = logits(j, i, True)
            for t in range(wb // LANES):
                mx = jnp.maximum(mx, s[:, t * LANES:(t + 1) * LANES])
        return mx

    mx = lax.fori_loop(0, nch, max_chunk, jnp.full((rows, LANES), NEG_BIG, F32))
    m = jnp.broadcast_to(jnp.max(mx, axis=1, keepdims=True), (rows, LANES))

    def acc_chunk(j, carry):
        l, acc = carry
        ps = []
        for i in range(sk // wb):
            s = logits(j, i, False)
            for t in range(wb // LANES):
                p = jnp.exp2(s[:, t * LANES:(t + 1) * LANES] - m)
                l = l + p
                ps.append(p.astype(BF16))
        return l, acc + _dot(jnp.concatenate(ps, axis=1), c_ref[0, kslice(j), :])

    l, acc = lax.fori_loop(0, nch, acc_chunk, (jnp.zeros((rows, LANES), F32), jnp.zeros((rows, DSA_LATENT), F32)))
    o = (acc / jnp.sum(l, axis=1, keepdims=True)).astype(BF16)
    for hd in range(DSA_HEADS):
        o_ref[0, :, hd * DSA_HEAD_DIM:(hd + 1) * DSA_HEAD_DIM] = _dot(o[hd * tq:(hd + 1) * tq], wuv_ref[hd]).astype(BF16)


def _dsa(qlat, qidx, w, kidx, c, wuv, pos0, s_valid, n_sel, tq):
    b, l, _ = qlat.shape
    s_pad = c.shape[1]
    sk = min(1024, s_pad)
    ka = jnp.pad(kidx, ((0, 0), (0, 0), (0, LANES - IDX_DIM)))
    qspec = lambda a: pl.BlockSpec((1, tq, a.shape[2]), lambda i, j: (i, j, 0))
    kspec = lambda a: pl.BlockSpec((1,) + a.shape[1:], lambda i, j: (i, 0, 0))
    dh = wuv.shape[0] * wuv.shape[2]
    return pl.pallas_call(
        functools.partial(_dsa_kernel, tq, sk, pos0, s_valid, n_sel),
        grid=(b, l // tq),
        in_specs=[qspec(qlat), qspec(qidx), qspec(w), kspec(ka), kspec(c), _const_spec(wuv.shape)],
        out_specs=pl.BlockSpec((1, tq, dh), lambda i, j: (i, j, 0)),
        out_shape=jax.ShapeDtypeStruct((b, l, dh), BF16),
        scratch_shapes=[pltpu.VMEM((tq, s_pad), I32), pltpu.VMEM((tq, s_pad), F32), pltpu.VMEM((tq, LANES), I32)],
        compiler_params=_params(("parallel", "arbitrary")),
        name="dsa",
    )(qlat, qidx, w, ka, c, wuv)


_PEER_PAIRS = [(r, c) for r in range(PEER_TOPK) for c in range(PEER_TOPK) if (r + 1) * (c + 1) <= PEER_TOPK]


def _top_rows(s, k):
    row = lax.broadcasted_iota(I32, s.shape, 0)
    vals, idxs = [], []
    for _ in range(k):
        m = jnp.max(s, axis=0, keepdims=True)
        first = jnp.min(jnp.where(s == m, row, s.shape[0]), axis=0, keepdims=True)
        s = jnp.where(row == first, -jnp.inf, s)
        vals.append(m)
        idxs.append(first)
    return jnp.concatenate(vals, axis=0), jnp.concatenate(idxs, axis=0)


def _peer_route_kernel(h_ref, g_ref, wq_ref, k1_ref, k2_ref, xn_ref, i1_ref, i2_ref, gate_ref):
    tn = h_ref.shape[0]
    xn = _rms(h_ref[...], g_ref[...]).astype(BF16)
    xn_ref[...] = xn
    q = _dot(xn, wq_ref[...]).astype(BF16)
    out1, out2, outg = [], [], []
    for hd in range(PEER_HEADS):
        qa = q[:, (2 * hd) * PEER_HALF:(2 * hd + 1) * PEER_HALF]
        qb = q[:, (2 * hd + 1) * PEER_HALF:(2 * hd + 2) * PEER_HALF]
        v1, i1 = _top_rows(_dot_nt(k1_ref[hd], qa), PEER_TOPK)
        v2, i2 = _top_rows(_dot_nt(k2_ref[hd], qb), PEER_TOPK)
        cand = jnp.concatenate([v1[r:r + 1] + v2[c:c + 1] for r, c in _PEER_PAIRS], axis=0)
        c1 = jnp.concatenate([i1[r:r + 1] for r, _ in _PEER_PAIRS], axis=0)
        c2 = jnp.concatenate([i2[c:c + 1] for _, c in _PEER_PAIRS], axis=0)
        row = lax.broadcasted_iota(I32, cand.shape, 0)
        tops, e1, e2 = [], [], []
        for _ in range(PEER_TOPK):
            m = jnp.max(cand, axis=0, keepdims=True)
            first = jnp.min(jnp.where(cand == m, row, cand.shape[0]), axis=0, keepdims=True)
            hit = row == first
            e1.append(jnp.sum(jnp.where(hit, c1, 0), axis=0, keepdims=True))
            e2.append(jnp.sum(jnp.where(hit, c2, 0), axis=0, keepdims=True))
            cand = jnp.where(hit, -jnp.inf, cand)
            tops.append(m)
        top = jnp.concatenate(tops, axis=0)
        ex = jnp.exp(top - top[0:1])
        outg.append(ex / jnp.sum(ex, axis=0, keepdims=True))
        out1.append(jnp.concatenate(e1, axis=0))
        out2.append(jnp.concatenate(e2, axis=0))
    as_f32 = lambda parts: pltpu.bitcast(jnp.concatenate(parts, axis=0), F32)
    i1_ref[...] = pltpu.bitcast(as_f32(out1).T, I32)
    i2_ref[...] = pltpu.bitcast(as_f32(out2).T, I32)
    gate_ref[...] = jnp.concatenate(outg, axis=0).T


def _peer_route(h, g, wq, k1, k2):
    n, d = h.shape
    tn = LANES
    slots = PEER_HEADS * PEER_TOPK
    row = lambda w: pl.BlockSpec((tn, w), lambda i: (i, 0))
    return pl.pallas_call(
        _peer_route_kernel,
        grid=(n // tn,),
        in_specs=[row(d), _const_spec(g.shape), _const_spec(wq.shape), _const_spec(k1.shape), _const_spec(k2.shape)],
        out_specs=[row(d), row(slots), row(slots), row(slots)],
        out_shape=[jax.ShapeDtypeStruct((n, d), BF16), jax.ShapeDtypeStruct((n, slots), I32),
                   jax.ShapeDtypeStruct((n, slots), I32), jax.ShapeDtypeStruct((n, slots), F32)],
        compiler_params=_params(("parallel",)),
        name="peer_route",
    )(h, g, wq, k1, k2)


_PLANE_PAD = 8


def _peer_gates_kernel(i1_ref, i2_ref, gate_ref, a_ref, planes_ref):
    tn, slots = i1_ref.shape
    plane = tn + _PLANE_PAD
    sub = lax.broadcasted_iota(I32, (PEER_NKEYS, slots), 0)

    def token(n, carry):
        r = pl.ds(n, 1)
        pt = jnp.where(sub == i1_ref[r, :], gate_ref[r, :], 0.0).astype(BF16)
        qt = jnp.where(sub == i2_ref[r, :], 1.0, 0.0).astype(BF16)
        planes_ref[pl.ds(n, PEER_NKEYS, stride=plane), :] = _dot_nt(pt, qt)
        return carry

    lax.fori_loop(0, tn, token, 0, unroll=8)
    for k in range(PEER_NKEYS):
        a_ref[:, k * PEER_NKEYS:(k + 1) * PEER_NKEYS] = planes_ref[pl.ds(k * plane, tn), :].astype(BF16)


def _peer_gates(i1, i2, gate):
    n, slots = i1.shape
    tn = _pick_tile(n, 128)
    ne = PEER_NKEYS * PEER_NKEYS
    row = lambda w: pl.BlockSpec((tn, w), lambda i: (i, 0))
    return pl.pallas_call(
        _peer_gates_kernel,
        grid=(n // tn,),
        in_specs=[row(slots)] * 3,
        out_specs=row(ne),
        out_shape=jax.ShapeDtypeStruct((n, ne), BF16),
        scratch_shapes=[pltpu.VMEM((PEER_NKEYS * (tn + _PLANE_PAD), PEER_NKEYS), F32)],
        compiler_params=_params(("parallel",)),
        name="peer_gates",
    )(i1, i2, gate)


_PEER_SUB = 256
def _peer_apply_kernel(final_norm, xn_ref, a_ref, u_ref, v_ref, h_ref, g_ref, o_ref, acc_ref):
    j = pl.program_id(1)

    @pl.when(j == 0)
    def _():
        acc_ref[...] = jnp.zeros_like(acc_ref)

    xn = xn_ref[...]
    total = None
    for s in range(u_ref.shape[0] // _PEER_SUB):
        sub = slice(s * _PEER_SUB, (s + 1) * _PEER_SUB)
        act = _gelu(_dot_nt(xn, u_ref[sub, :]))
        part = _dot((act * a_ref[:, sub].astype(F32)).astype(BF16), v_ref[sub, :])
        total = part if total is None else total + part
    acc_ref[...] += total

    @pl.when(j == pl.num_programs(1) - 1)
    def _():
        y = h_ref[...] + acc_ref[...]
        o_ref[...] = _rms(y, g_ref[...]) if final_norm else y


def _peer_apply(xn, a, u, v, h, g, final_norm):
    n, d = xn.shape
    ne = u.shape[0]
    tn = _pick_tile(n, 512)
    te = 8 * _PEER_SUB
    return pl.pallas_call(
        functools.partial(_peer_apply_kernel, final_norm),
        grid=(n // tn, ne // te),
        in_specs=[pl.BlockSpec((tn, d), lambda i, j: (i, 0)), pl.BlockSpec((tn, te), lambda i, j: (i, j)),
                  pl.BlockSpec((te, d), lambda i, j: (j, 0)), pl.BlockSpec((te, d), lambda i, j: (j, 0)),
                  pl.BlockSpec((tn, d), lambda i, j: (i, 0)), _const_spec(g.shape)],
        out_specs=pl.BlockSpec((tn, d), lambda i, j: (i, 0)),
        out_shape=jax.ShapeDtypeStruct((n, d), F32),
        scratch_shapes=[pltpu.VMEM((tn, d), F32)],
        compiler_params=_params(("parallel", "arbitrary")),
        name="peer_apply",
    )(xn, a, u, v, h, g)


def _split_w_in(w_in, d):
    widths = (d, DSA_HEADS * DSA_LATENT, DSA_LATENT, IDX_HEADS * IDX_DIM, IDX_DIM, IDX_HEADS,
              MEM_HEADS * MEM_HEAD_DIM, N_BRANCH * d)
    offs = np.cumsum((0,) + widths)
    wu, wq, wc, wqi, wki, wwi, wmq, wg = [w_in[:, offs[i]:offs[i + 1]] for i in range(8)]
    wkw = jnp.pad(jnp.concatenate([wki, wwi], axis=1), ((0, 0), (0, LANES - IDX_DIM - IDX_HEADS)))
    kw_scale = jnp.concatenate([jnp.ones((IDX_DIM,), F32), jnp.full((IDX_HEADS,), IDX_HEADS ** -0.5, F32),
                                jnp.zeros((LANES - IDX_DIM - IDX_HEADS,), F32)])[None, :]
    wqi = jnp.pad(wqi.reshape(-1, IDX_HEADS, IDX_DIM), ((0, 0), (0, 0), (0, LANES - IDX_DIM))).reshape(-1, IDX_HEADS * LANES)
    ws = [wu, wq * (DSA_LATENT ** -0.5 * LOG2E), wc, wqi, wkw, wmq * MEM_HEAD_DIM ** -0.5, wg]
    return [w.astype(BF16) for w in ws], kw_scale


def _pad_rows(a, n):
    return a if a.shape[0] == n else jnp.pad(a, ((0, n - a.shape[0]),) + ((0, 0),) * (a.ndim - 1))


def _pad_keys(a, s_pad):
    return jnp.pad(a, ((0, 0), (0, s_pad - a.shape[1]), (0, 0)))


def _token_mix(h, seq_shape, lw, mats, state, cache, mem_kv, pos0):
    b, l = seq_shape
    bf = lambda a: a.astype(BF16)
    seq = lambda a: a.reshape(b, l, -1)
    u, q, c, qi, kw, mq, g = _inproj(h, lw["g_norm1"], lw["g_kv"], lw["kw_scale"], lw["w_in"])
    ki, wi = kw[:, :IDX_DIM], kw[:, IDX_DIM:IDX_DIM + IDX_HEADS]

    ys, s_re, s_im = _ssm_branch(seq(u), mats, state[0], state[1])

    c_all, k_all = bf(seq(c)), bf(seq(ki))
    if cache is not None:
        c_all = jnp.concatenate([bf(cache[0]), c_all], axis=1)
        k_all = jnp.concatenate([bf(cache[1]), k_all], axis=1)
    s_all = c_all.shape[1]
    s_pad = -(-s_all // 1024) * 1024 if cache is not None else s_all
    yd = _dsa(seq(q), seq(qi), seq(wi), _pad_keys(k_all, s_pad), _pad_keys(c_all, s_pad), lw["w_uv"], pos0, s_all,
              min(DSA_TOPK, s_all // 4), _pick_tile(l, _DSA_TQ))

    ym = _memattn(seq(mq), mem_kv[0], mem_kv[1])

    flat = lambda a: a.reshape(b * l, -1)
    h2 = _merge(h, flat(ys), flat(yd), flat(ym), g, lw["w_glu"], lw["b_glu"], lw["w_br_ssm"], lw["w_br_dsa"],
                lw["w_br_mem"], lw["w_out"])
    return h2, seq(c), seq(ki), s_re, s_im


def _channel_mix(h2, lw, g_final, final_norm):
    n = h2.shape[0]
    h2 = _pad_rows(h2, -(-n // LANES) * LANES)
    xn, i1, i2, gate = _peer_route(h2, lw["g_norm2"], lw["w_peer_q"], lw["peer_sub_k1"], lw["peer_sub_k2"])
    a = _peer_gates(i1, i2, gate)
    return _peer_apply(xn, a, lw["peer_u"], lw["peer_v"], h2, g_final, final_norm)[:n]


def kernel(x_prompt, x_sample, mem_prompt, cache_dsa_latent, cache_dsa_idx_k, state_ssm_re, state_ssm_im, cache_mem_k, cache_mem_v, g_norm1, w_in, g_kv, w_uv, ssm_lam_re, ssm_lam_im, ssm_log_dt, ssm_b_re, ssm_b_im, ssm_c_re, ssm_c_im, ssm_d, w_glu, b_glu, g_mem, w_mem_kv, w_br_ssm, w_br_dsa, w_br_mem, w_out, g_norm2, w_peer_q, peer_sub_k1, peer_sub_k2, peer_u, peer_v, g_final):
    depth = w_in.shape[0]
    bp, lp, d = x_prompt.shape
    bs, ls, _ = x_sample.shape
    past = cache_dsa_latent.shape[2]
    mem_w = MEM_HEADS * MEM_HEAD_DIM
    n_mem = mem_prompt.shape[1]
    heads = (n_mem, MEM_HEADS, MEM_HEAD_DIM)
    bf = lambda a: a.astype(BF16)
    row = lambda a: a[None, :]

    hp, hs = x_prompt.reshape(bp * lp, d), x_sample.reshape(bs * ls, d)
    outs = [[] for _ in range(10)]
    for l in range(depth):
        ws, kw_scale = _split_w_in(w_in[l], d)
        lw = dict(g_norm1=row(g_norm1[l]), g_kv=row(g_kv[l]), kw_scale=kw_scale, w_in=ws, w_uv=bf(w_uv[l]),
                  w_glu=bf(w_glu[l]), b_glu=row(b_glu[l]), w_br_ssm=bf(w_br_ssm[l]), w_br_dsa=bf(w_br_dsa[l]),
                  w_br_mem=bf(w_br_mem[l]), w_out=bf(w_out[l]), g_norm2=row(g_norm2[l]), w_peer_q=bf(w_peer_q[l]),
                  peer_sub_k1=bf(peer_sub_k1[l]), peer_sub_k2=bf(peer_sub_k2[l]), peer_u=bf(peer_u[l]),
                  peer_v=bf(peer_v[l]))
        mats = _ssm_matrices(ssm_lam_re[l], ssm_lam_im[l], ssm_log_dt[l], ssm_b_re[l], ssm_b_im[l],
                             ssm_c_re[l], ssm_c_im[l], ssm_d[l], SSM_T)
        kv = _memkv(mem_prompt.reshape(bp * n_mem, d), row(g_mem[l]), bf(w_mem_kv[l]))
        mk_p, mv_p = kv[:, :mem_w].reshape(bp, n_mem, mem_w), kv[:, mem_w:].reshape(bp, n_mem, mem_w)
        zeros = jnp.zeros((bp,) + state_ssm_re.shape[2:], F32)

        h2p, c_p, ki_p, sre_p, sim_p = _token_mix(hp, (bp, lp), lw, mats, (zeros, zeros), None, (bf(mk_p), bf(mv_p)), 0)
        mem_s = (bf(cache_mem_k[l].reshape(bs, n_mem, mem_w)), bf(cache_mem_v[l].reshape(bs, n_mem, mem_w)))
        h2s, c_s, ki_s, sre_s, sim_s = _token_mix(hs, (bs, ls), lw, mats, (state_ssm_re[l], state_ssm_im[l]),
                                                  (cache_dsa_latent[l], cache_dsa_idx_k[l]), mem_s, past)
        final = l == depth - 1
        hp = _channel_mix(h2p, lw, row(g_final), final)
        hs = _channel_mix(h2s, lw, row(g_final), final)
        for lst, val in zip(outs, (c_p, ki_p, sre_p, sim_p, mk_p.reshape((bp,) + heads), mv_p.reshape((bp,) + heads),
                                   c_s, ki_s, sre_s, sim_s)):
            lst.append(val)
    return (hp.reshape(bp, lp, d), hs.reshape(bs, ls, d)) + tuple(jnp.stack(o) for o in outs)
```

```python
import functools
import math

import jax
import jax.numpy as jnp
import numpy as np
from jax import lax
from jax.experimental import pallas as pl
from jax.experimental.pallas import tpu as pltpu

F32 = jnp.float32
BF16 = jnp.bfloat16
I32 = jnp.int32

EPS = 1e-6
CHUNK = 64
SSM_GROUP = 16
SSM_STATE = 64
SSM_T = 32
DSA_HEADS = 8
DSA_LATENT = 256
DSA_HEAD_DIM = 128
IDX_HEADS = 8
IDX_DIM = 64
DSA_TOPK = 256
MEM_HEADS = 4
MEM_HEAD_DIM = 256
PEER_HEADS = 8
PEER_NKEYS = 128
PEER_HALF = 128
PEER_TOPK = 16
N_BRANCH = 3

LANES = 128
VMEM_LIMIT = 56 * 1024 * 1024
INT_MIN = -2 ** 31
INT_MAX = 2 ** 31 - 1
NEG_BIG = -1e30
LOG2E = 1.4426950408889634


def _pick_tile(n, target):
    if n <= target:
        return n
    for t in range(target, 7, -1):
        if n % t == 0 and t % 8 == 0:
            return t
    return n


def _params(sem):
    return pltpu.CompilerParams(dimension_semantics=sem, vmem_limit_bytes=VMEM_LIMIT)


def _const_spec(shape):
    nd = len(shape)
    return pl.BlockSpec(shape, lambda *_: (0,) * nd, pipeline_mode=pl.Buffered(1))


def _rms(x, g):
    return x * lax.rsqrt(jnp.mean(x * x, axis=-1, keepdims=True) + EPS) * g


def _gelu(x):
    return 0.5 * x * (1.0 + jnp.tanh(0.7978845608028654 * (x + 0.044715 * x * x * x)))


def _sigmoid(x):
    return 1.0 / (1.0 + jnp.exp(-x))


def _dot_nt(a, b):
    return lax.dot_general(a, b, (((1,), (1,)), ((), ())), preferred_element_type=F32)


def _dot(a, b):
    return jnp.dot(a, b, preferred_element_type=F32)


def _inproj_kernel(x_ref, g1_ref, gkv_ref, kws_ref, wu_ref, wq_ref, wc_ref, wqi_ref, wkw_ref, wmq_ref, wg_ref,
                   u_ref, q_ref, c_ref, qi_ref, kw_ref, mq_ref, g_ref):
    xn = _rms(x_ref[...], g1_ref[...]).astype(BF16)
    u_ref[...] = _dot(xn, wu_ref[...]).astype(BF16)
    q_ref[...] = _dot(xn, wq_ref[...]).astype(BF16)
    c_ref[...] = _rms(_dot(xn, wc_ref[...]), gkv_ref[...])
    qi_ref[...] = _dot(xn, wqi_ref[...]).astype(BF16)
    kw_ref[...] = _dot(xn, wkw_ref[...]) * kws_ref[...]
    mq_ref[...] = _dot(xn, wmq_ref[...]).astype(BF16)
    g_ref[...] = _sigmoid(_dot(xn, wg_ref[...])).astype(BF16)


def _inproj(h, g1, gkv, kw_scale, ws):
    n, d = h.shape
    tm = _pick_tile(n, 256)
    widths = [w.shape[1] for w in ws]
    dts = [BF16, BF16, F32, BF16, F32, BF16, BF16]
    row = lambda w: pl.BlockSpec((tm, w), lambda i: (i, 0))
    return pl.pallas_call(
        _inproj_kernel,
        grid=(n // tm,),
        in_specs=[row(d), _const_spec(g1.shape), _const_spec(gkv.shape), _const_spec(kw_scale.shape)]
        + [_const_spec(w.shape) for w in ws],
        out_specs=[row(w) for w in widths],
        out_shape=[jax.ShapeDtypeStruct((n, w), dt) for w, dt in zip(widths, dts)],
        compiler_params=_params(("parallel",)),
        name="inproj",
    )(h, g1, gkv, kw_scale, *ws)


def _memkv_kernel(x_ref, g_ref, w_ref, o_ref):
    xn = _rms(x_ref[...], g_ref[...]).astype(BF16)
    o_ref[...] = _dot(xn, w_ref[...])


def _memkv(mem, g, w):
    n, d = mem.shape
    tm = _pick_tile(n, 256)
    return pl.pallas_call(
        _memkv_kernel,
        grid=(n // tm,),
        in_specs=[pl.BlockSpec((tm, d), lambda i: (i, 0)), _const_spec(g.shape), _const_spec(w.shape)],
        out_specs=pl.BlockSpec((tm, w.shape[1]), lambda i: (i, 0)),
        out_shape=jax.ShapeDtypeStruct((n, w.shape[1]), F32),
        compiler_params=_params(("parallel",)),
        name="memkv",
    )(mem, g, w)


def _memattn_kernel(q_ref, k_ref, v_ref, o_ref):
    for hd in range(MEM_HEADS):
        sl = slice(hd * MEM_HEAD_DIM, (hd + 1) * MEM_HEAD_DIM)
        logits = _dot_nt(q_ref[0, :, sl], k_ref[0, :, sl])
        m = jnp.max(logits, axis=-1, keepdims=True)
        p = jnp.exp(logits - m)
        l = jnp.sum(p, axis=-1, keepdims=True)
        o = _dot(p.astype(BF16), v_ref[0, :, sl]) / l
        o_ref[0, :, sl] = o.astype(BF16)


def _memattn(q, k, v):
    b, l, w = q.shape
    tl = _pick_tile(l, 512)
    nm = k.shape[1]
    return pl.pallas_call(
        _memattn_kernel,
        grid=(b, l // tl),
        in_specs=[pl.BlockSpec((1, tl, w), lambda i, j: (i, j, 0)),
                  pl.BlockSpec((1, nm, w), lambda i, j: (i, 0, 0)),
                  pl.BlockSpec((1, nm, w), lambda i, j: (i, 0, 0))],
        out_specs=pl.BlockSpec((1, tl, w), lambda i, j: (i, j, 0)),
        out_shape=jax.ShapeDtypeStruct((b, l, w), BF16),
        compiler_params=_params(("parallel", "parallel")),
        name="memattn",
    )(q, k, v)


def _merge_kernel(h_ref, ys_ref, yd_ref, ym_ref, g_ref, wglu_ref, bglu_ref, wbs_ref, wbd_ref, wbm_ref, wout_ref,
                  o_ref):
    d = h_ref.shape[1]
    z = _gelu(ys_ref[...].astype(F32))
    gate = _sigmoid(_dot(z.astype(BF16), wglu_ref[...]) + bglu_ref[...])
    a = _dot((z * gate).astype(BF16), wbs_ref[...])
    b = _dot(yd_ref[...], wbd_ref[...])
    c = _dot(ym_ref[...], wbm_ref[...])
    g = g_ref[...].astype(F32)
    merged = g[:, 0:d] * a + g[:, d:2 * d] * b + g[:, 2 * d:3 * d] * c
    o_ref[...] = h_ref[...] + _dot(merged.astype(BF16), wout_ref[...])


def _merge(h, ys, yd, ym, g, wglu, bglu, wbs, wbd, wbm, wout):
    n, d = h.shape
    tm = _pick_tile(n, 512)
    row = lambda w: pl.BlockSpec((tm, w), lambda i: (i, 0))
    consts = [wglu, bglu, wbs, wbd, wbm, wout]
    return pl.pallas_call(
        _merge_kernel,
        grid=(n // tm,),
        in_specs=[row(d), row(d), row(d), row(d), row(3 * d)] + [_const_spec(c.shape) for c in consts],
        out_specs=row(d),
        out_shape=jax.ShapeDtypeStruct((n, d), F32),
        compiler_params=_params(("parallel",)),
        name="merge",
    )(h, ys, yd, ym, g, *consts)


def _ssm_matrices(lam_re, lam_im, log_dt, b_re, b_im, c_re, c_im, d, t_len):
    hi = lax.Precision.HIGHEST
    g_n, p_n = lam_re.shape
    dt = jnp.exp(log_dt)[:, None]
    mag = jnp.exp(lam_re * dt)
    ar, ai = mag * jnp.cos(lam_im * dt), mag * jnp.sin(lam_im * dt)
    den = lam_re * lam_re + lam_im * lam_im
    nr, ni = ar - 1.0, ai
    kr = ((nr * lam_re + ni * lam_im) / den)[..., None]
    ki = ((ni * lam_re - nr * lam_im) / den)[..., None]
    bbr, bbi = kr * b_re - ki * b_im, kr * b_im + ki * b_re
    j = jnp.arange(t_len + 1, dtype=F32)[:, None, None]
    pmag = jnp.exp(j * (lam_re * dt))
    pr, pi = pmag * jnp.cos(j * (lam_im * dt)), pmag * jnp.sin(j * (lam_im * dt))
    mr = pr[:t_len, ..., None] * bbr - pi[:t_len, ..., None] * bbi
    mi = pr[:t_len, ..., None] * bbi + pi[:t_len, ..., None] * bbr
    kern = (jnp.einsum('gdp,jgpc->jgdc', c_re, mr, precision=hi)
            - jnp.einsum('gdp,jgpc->jgdc', c_im, mi, precision=hi))
    s_i = jnp.arange(t_len)[:, None]
    t_i = jnp.arange(t_len)[None, :]
    lag = t_i - s_i
    kg = jnp.where((lag >= 0)[:, :, None, None, None], kern[jnp.clip(lag, 0)], 0.0)
    tz = kg.transpose(2, 0, 4, 1, 3).reshape(g_n, t_len * SSM_GROUP, t_len * SSM_GROUP)
    pad = ((0, 0), (0, 0), (0, LANES - p_n))
    vr = jnp.pad(mr[::-1].transpose(1, 0, 3, 2).reshape(g_n, t_len * SSM_GROUP, p_n), pad)
    vi = jnp.pad(mi[::-1].transpose(1, 0, 3, 2).reshape(g_n, t_len * SSM_GROUP, p_n), pad)
    tzv = jnp.concatenate([tz, vr, vi], axis=-1).astype(BF16)
    pr1, pi1 = pr[1:].transpose(1, 2, 0), pi[1:].transpose(1, 2, 0)
    crt, cit = c_re.transpose(0, 2, 1), c_im.transpose(0, 2, 1)
    wre = crt[:, :, None, :] * pr1[..., None] - cit[:, :, None, :] * pi1[..., None]
    wim = -(crt[:, :, None, :] * pi1[..., None] + cit[:, :, None, :] * pr1[..., None])
    rpad = ((0, 0), (0, LANES - p_n), (0, 0))
    wre = jnp.pad(wre.reshape(g_n, p_n, -1), rpad).astype(BF16)
    wim = jnp.pad(wim.reshape(g_n, p_n, -1), rpad).astype(BF16)
    atr = jnp.pad(pr[t_len], ((0, 0), (0, LANES - p_n)))[:, None, :]
    ati = jnp.pad(pi[t_len], ((0, 0), (0, LANES - p_n)))[:, None, :]
    drow = jnp.tile(d.reshape(g_n, 1, SSM_GROUP), (1, t_len, 1)).reshape(g_n, 1, t_len * SSM_GROUP)
    return tzv, wre, wim, atr, ati, drow


def _ssm_kernel(nk, nb, u_ref, tzv_ref, wre_ref, wim_ref, atr_ref, ati_ref, d_ref, ire_ref, iim_ref,
                y_ref, fre_ref, fim_ref, yi_ref, sr_ref, si_ref, xr_ref, xi_ref):
    tc = u_ref.shape[2]
    u = u_ref[0]
    full = _dot(u, tzv_ref[0])
    yi_ref[...] = full[:, :tc]
    sr_ref[...] = full[:, tc:tc + LANES]
    si_ref[...] = full[:, tc + LANES:]
    atr, ati = atr_ref[0], ati_ref[0]

    def step(k, carry):
        xr, xi = carry
        rows = pl.ds(pl.multiple_of(k * nb, nb), nb)
        xr_ref[rows, :] = xr
        xi_ref[rows, :] = xi
        return (atr * xr - ati * xi + sr_ref[rows, :], atr * xi + ati * xr + si_ref[rows, :])

    xr, xi = lax.fori_loop(0, nk, step, (ire_ref[0], iim_ref[0]))
    fre_ref[0] = xr
    fim_ref[0] = xi
    y = (yi_ref[...] + _dot(xr_ref[...].astype(BF16), wre_ref[0]) + _dot(xi_ref[...].astype(BF16), wim_ref[0])
         + d_ref[0] * u.astype(F32))
    y_ref[0] = y.astype(BF16)


def _ssm(u, mats, init_re, init_im, nk, nb):
    tzv, wre, wim, atr, ati, drow = mats
    g_n, r, tc = u.shape
    per_g = lambda a: pl.BlockSpec((1,) + a.shape[1:], lambda g: (g, 0, 0))
    ins = [u, tzv, wre, wim, atr, ati, drow, init_re, init_im]
    st = jax.ShapeDtypeStruct((g_n, nb, LANES), F32)
    return pl.pallas_call(
        functools.partial(_ssm_kernel, nk, nb),
        grid=(g_n,),
        in_specs=[per_g(a) for a in ins],
        out_specs=[per_g(u), per_g(init_re), per_g(init_re)],
        out_shape=[jax.ShapeDtypeStruct(u.shape, BF16), st, st],
        scratch_shapes=[pltpu.VMEM((r, tc), F32)] + [pltpu.VMEM((r, LANES), F32)] * 4,
        compiler_params=_params(("parallel",)),
        name="ssm",
    )(*ins)


def _ssm_branch(zu, mats, st_re, st_im):
    b, l, w = zu.shape
    g_n, p_n = st_re.shape[1], st_re.shape[2]
    nk = l // SSM_T
    ug = zu.reshape(b, nk, SSM_T, g_n, SSM_GROUP).transpose(3, 1, 0, 2, 4).reshape(g_n, nk * b, SSM_T * SSM_GROUP)
    pad = ((0, 0), (0, 0), (0, LANES - p_n))
    ire = jnp.pad(st_re.transpose(1, 0, 2), pad)
    iim = jnp.pad(st_im.transpose(1, 0, 2), pad)
    y, fre, fim = _ssm(ug, mats, ire, iim, nk, b)
    y = y.reshape(g_n, nk, b, SSM_T, SSM_GROUP).transpose(2, 1, 3, 0, 4).reshape(b, l, w)
    return y, fre[:, :, :p_n].transpose(1, 0, 2), fim[:, :, :p_n].transpose(1, 0, 2)


_DSA_TQ = 256
_DSA_VALUE_STEPS = 24


def _dsa_kernel(tq, sk, pos0, s_valid, n_sel, qlat_ref, qidx_ref, w_ref, ka_ref, c_ref, wuv_ref,
                o_ref, key_ref, bias_ref, j_ref):
    s_pad = key_ref.shape[1]
    q0 = pos0 + pl.program_id(1) * tq
    rowpos = q0 + lax.broadcasted_iota(I32, (tq, 1), 0)
    vis = jnp.minimum((rowpos // CHUNK + 1) * CHUNK, s_valid)
    vis_max = jnp.minimum(((q0 + tq - 1) // CHUNK + 1) * CHUNK, s_valid)
    nch = (vis_max + sk - 1) // sk
    lane = lax.broadcasted_iota(I32, (tq, sk), 1)
    kslice = lambda j: pl.ds(pl.multiple_of(j * sk, sk), sk)

    qs = jnp.concatenate([qidx_ref[0, :, hd * LANES:(hd + 1) * LANES] for hd in range(IDX_HEADS)], axis=0)

    nblk = sk // LANES

    def score_chunk(j, carry):
        kmin, kmax = carry
        rel = jnp.maximum(_dot_nt(qs, ka_ref[0, kslice(j), :]), 0.0)
        acc = jnp.zeros((tq, sk), F32)
        for hd in range(IDX_HEADS):
            acc = acc + w_ref[0, :, hd:hd + 1] * rel[hd * tq:(hd + 1) * tq]
        bits = pltpu.bitcast(acc, I32)
        key = bits ^ ((bits >> 31) & 0x7FFFFFFF)
        visible = j * sk + lane < vis
        key_ref[:, kslice(j)] = jnp.where(visible, key, INT_MIN)
        lo_key, hi_key = jnp.where(visible, key, INT_MAX), jnp.where(visible, key, INT_MIN)
        for i in range(nblk):
            kmin = jnp.minimum(kmin, lo_key[:, i * LANES:(i + 1) * LANES])
            kmax = jnp.maximum(kmax, hi_key[:, i * LANES:(i + 1) * LANES])
        return kmin, kmax

    kmin, kmax = lax.fori_loop(0, nch, score_chunk,
                               (jnp.full((tq, LANES), INT_MAX, I32), jnp.full((tq, LANES), INT_MIN, I32)))
    kmin = jnp.min(kmin, axis=1, keepdims=True)
    kmax = jnp.max(kmax, axis=1, keepdims=True)

    def count(pred):
        def body(j, c):
            hit = jnp.where(pred(key_ref[:, kslice(j)], j * sk + lane), 1.0, 0.0)
            for i in range(nblk):
                c = c + hit[:, i * LANES:(i + 1) * LANES]
            return c
        return jnp.sum(lax.fori_loop(0, nch, body, jnp.zeros((tq, LANES), F32)), axis=1, keepdims=True)

    key_of = lambda v: (lambda b: b ^ ((b >> 31) & 0x7FFFFFFF))(pltpu.bitcast(v, I32))
    val_of = lambda k: pltpu.bitcast(k ^ ((k >> 31) & 0x7FFFFFFF), F32)

    def settled(lo, hi, cnt):
        return (cnt <= n_sel) | (hi - 1 <= lo)

    def halve(state):
        it, lo, hi, cnt, _ = state
        mid_v = key_of(0.5 * val_of(lo) + 0.5 * val_of(hi))
        mid_k = (lo >> 1) + (hi >> 1) + (lo & hi & 1)
        mid = jnp.where((mid_v > lo) & (mid_v < hi) & (it < _DSA_VALUE_STEPS), mid_v, mid_k)
        c = count(lambda k, col: k >= mid)
        live = jnp.logical_not(settled(lo, hi, cnt))
        up = live & (c >= n_sel)
        lo, cnt, hi = jnp.where(up, mid, lo), jnp.where(up, c, cnt), jnp.where(live & (c < n_sel), mid, hi)
        return it + 1, lo, hi, cnt, jnp.max(jnp.where(settled(lo, hi, cnt), 0, 1))

    hi0 = jnp.where(kmax == INT_MAX, INT_MAX, kmax + 1)
    cnt0 = vis.astype(F32)
    state = (jnp.int32(0), kmin, hi0, cnt0, jnp.max(jnp.where(settled(kmin, hi0, cnt0), 0, 1)))
    _, thr, _, cnt, _ = lax.while_loop(lambda st: (st[0] < _DSA_VALUE_STEPS + 34) & (st[4] > 0), halve, state)
    thr = jnp.maximum(thr, INT_MIN + 1)
    excess = cnt > n_sel

    j_ref[...] = jnp.full(j_ref.shape, s_pad, I32)

    @pl.when(jnp.max(jnp.where(excess, 1.0, 0.0)) > 0.0)
    def _():
        need = n_sel - count(lambda k, col: k > thr)

        def idx_step(_, lohi):
            lo, hi = lohi
            mid = (lo + hi) >> 1
            ok = count(lambda k, col: (k == thr) & (col < mid)) >= need
            return jnp.where(ok, lo, mid), jnp.where(ok, mid, hi)

        steps = int(math.ceil(math.log2(s_pad))) + 1
        _, hi = lax.fori_loop(0, steps, idx_step, (jnp.zeros((tq, 1), I32), jnp.full((tq, 1), s_pad, I32)))
        j_ref[...] = jnp.broadcast_to(jnp.where(excess, hi, s_pad), j_ref.shape)

    jlim = j_ref[:, 0:1]

    rows = DSA_HEADS * tq
    q = jnp.concatenate([qlat_ref[0, :, hd * DSA_LATENT:(hd + 1) * DSA_LATENT] for hd in range(DSA_HEADS)], axis=0)

    wb = min(sk, 2 * LANES)

    lane_wb = lax.broadcasted_iota(I32, (tq, wb), 1)

    def logits(j, i, first):
        blk = pl.ds(pl.multiple_of(j * sk + i * wb, wb), wb)
        if first:
            k = key_ref[:, blk]
            sel = (k > thr) | ((k == thr) & (j * sk + i * wb + lane_wb < jlim))
            bias = jnp.where(sel, 0.0, NEG_BIG)
            bias_ref[:, blk] = bias
        else:
            bias = bias_ref[:, blk]
        s = _dot_nt(q, c_ref[0, blk, :])
        return (s.reshape(DSA_HEADS, tq, wb) + bias[None]).reshape(rows, wb)

    def max_chunk(j, mx):
        for i in range(sk // wb):
            s = logits(j, i, True)
            for t in range(wb // LANES):
                mx = jnp.maximum(mx, s[:, t * LANES:(t + 1) * LANES])
        return mx

    mx = lax.fori_loop(0, nch, max_chunk, jnp.full((rows, LANES), NEG_BIG, F32))
    m = jnp.broadcast_to(jnp.max(mx, axis=1, keepdims=True), (rows, LANES))

    def acc_chunk(j, carry):
        l, acc = carry
        ps = []
        for i in range(sk // wb):
            s = logits(j, i, False)
            for t in range(wb // LANES):
                p = jnp.exp2(s[:, t * LANES:(t + 1) * LANES] - m)
                l = l + p
                ps.append(p.astype(BF16))
        return l, acc + _dot(jnp.concatenate(ps, axis=1), c_ref[0, kslice(j), :])

    l, acc = lax.fori_loop(0, nch, acc_chunk, (jnp.zeros((rows, LANES), F32), jnp.zeros((rows, DSA_LATENT), F32)))
    o = (acc / jnp.sum(l, axis=1, keepdims=True)).astype(BF16)
    for hd in range(DSA_HEADS):
        o_ref[0, :, hd * DSA_HEAD_DIM:(hd + 1) * DSA_HEAD_DIM] = _dot(o[hd * tq:(hd + 1) * tq], wuv_ref[hd]).astype(BF16)


def _dsa(qlat, qidx, w, kidx, c, wuv, pos0, s_valid, n_sel, tq):
    b, l, _ = qlat.shape
    s_pad = c.shape[1]
    sk = min(1024, s_pad)
    ka = jnp.pad(kidx, ((0, 0), (0, 0), (0, LANES - IDX_DIM)))
    qspec = lambda a: pl.BlockSpec((1, tq, a.shape[2]), lambda i, j: (i, j, 0))
    kspec = lambda a: pl.BlockSpec((1,) + a.shape[1:], lambda i, j: (i, 0, 0))
    dh = wuv.shape[0] * wuv.shape[2]
    return pl.pallas_call(
        functools.partial(_dsa_kernel, tq, sk, pos0, s_valid, n_sel),
        grid=(b, l // tq),
        in_specs=[qspec(qlat), qspec(qidx), qspec(w), kspec(ka), kspec(c), _const_spec(wuv.shape)],
        out_specs=pl.BlockSpec((1, tq, dh), lambda i, j: (i, j, 0)),
        out_shape=jax.ShapeDtypeStruct((b, l, dh), BF16),
        scratch_shapes=[pltpu.VMEM((tq, s_pad), I32), pltpu.VMEM((tq, s_pad), F32), pltpu.VMEM((tq, LANES), I32)],
        compiler_params=_params(("parallel", "arbitrary")),
        name="dsa",
    )(qlat, qidx, w, ka, c, wuv)


_PEER_PAIRS = [(r, c) for r in range(PEER_TOPK) for c in range(PEER_TOPK) if (r + 1) * (c + 1) <= PEER_TOPK]


def _top_rows(s, k):
    row = lax.broadcasted_iota(I32, s.shape, 0)
    vals, idxs = [], []
    for _ in range(k):
        m = jnp.max(s, axis=0, keepdims=True)
        first = jnp.min(jnp.where(s == m, row, s.shape[0]), axis=0, keepdims=True)
        s = jnp.where(row == first, -jnp.inf, s)
        vals.append(m)
        idxs.append(first)
    return jnp.concatenate(vals, axis=0), jnp.concatenate(idxs, axis=0)


def _peer_route_kernel(h_ref, g_ref, wq_ref, k1_ref, k2_ref, xn_ref, i1_ref, i2_ref, gate_ref):
    tn = h_ref.shape[0]
    xn = _rms(h_ref[...], g_ref[...]).astype(BF16)
    xn_ref[...] = xn
    q = _dot(xn, wq_ref[...]).astype(BF16)
    out1, out2, outg = [], [], []
    for hd in range(PEER_HEADS):
        qa = q[:, (2 * hd) * PEER_HALF:(2 * hd + 1) * PEER_HALF]
        qb = q[:, (2 * hd + 1) * PEER_HALF:(2 * hd + 2) * PEER_HALF]
        v1, i1 = _top_rows(_dot_nt(k1_ref[hd], qa), PEER_TOPK)
        v2, i2 = _top_rows(_dot_nt(k2_ref[hd], qb), PEER_TOPK)
        cand = jnp.concatenate([v1[r:r + 1] + v2[c:c + 1] for r, c in _PEER_PAIRS], axis=0)
        c1 = jnp.concatenate([i1[r:r + 1] for r, _ in _PEER_PAIRS], axis=0)
        c2 = jnp.concatenate([i2[c:c + 1] for _, c in _PEER_PAIRS], axis=0)
        row = lax.broadcasted_iota(I32, cand.shape, 0)
        tops, e1, e2 = [], [], []
        for _ in range(PEER_TOPK):
            m = jnp.max(cand, axis=0, keepdims=True)
            first = jnp.min(jnp.where(cand == m, row, cand.shape[0]), axis=0, keepdims=True)
            hit = row == first
            e1.append(jnp.sum(jnp.where(hit, c1, 0), axis=0, keepdims=True))
            e2.append(jnp.sum(jnp.where(hit, c2, 0), axis=0, keepdims=True))
            cand = jnp.where(hit, -jnp.inf, cand)
            tops.append(m)
        top = jnp.concatenate(tops, axis=0)
        ex = jnp.exp(top - top[0:1])
        outg.append(ex / jnp.sum(ex, axis=0, keepdims=True))
        out1.append(jnp.concatenate(e1, axis=0))
        out2.append(jnp.concatenate(e2, axis=0))
    as_f32 = lambda parts: pltpu.bitcast(jnp.concatenate(parts, axis=0), F32)
    i1_ref[...] = pltpu.bitcast(as_f32(out1).T, I32)
    i2_ref[...] = pltpu.bitcast(as_f32(out2).T, I32)
    gate_ref[...] = jnp.concatenate(outg, axis=0).T


def _peer_route(h, g, wq, k1, k2):
    n, d = h.shape
    tn = LANES
    slots = PEER_HEADS * PEER_TOPK
    row = lambda w: pl.BlockSpec((tn, w), lambda i: (i, 0))
    return pl.pallas_call(
        _peer_route_kernel,
        grid=(n // tn,),
        in_specs=[row(d), _const_spec(g.shape), _const_spec(wq.shape), _const_spec(k1.shape), _const_spec(k2.shape)],
        out_specs=[row(d), row(slots), row(slots), row(slots)],
        out_shape=[jax.ShapeDtypeStruct((n, d), BF16), jax.ShapeDtypeStruct((n, slots), I32),
                   jax.ShapeDtypeStruct((n, slots), I32), jax.ShapeDtypeStruct((n, slots), F32)],
        compiler_params=_params(("parallel",)),
        name="peer_route",
    )(h, g, wq, k1, k2)


_PLANE_PAD = 4


def _peer_gates_kernel(i1_ref, i2_ref, gate_ref, a_ref, planes_ref):
    tn, slots = i1_ref.shape
    plane = tn + _PLANE_PAD
    sub = lax.broadcasted_iota(I32, (PEER_NKEYS, slots), 0)

    def token(n, carry):
        r = pl.ds(n, 1)
        pt = jnp.where(sub == i1_ref[r, :], gate_ref[r, :], 0.0).astype(BF16)
        qt = jnp.where(sub == i2_ref[r, :], 1.0, 0.0).astype(BF16)
        planes_ref[pl.ds(n, PEER_NKEYS, stride=plane), :] = _dot_nt(pt, qt)
        return carry

    lax.fori_loop(0, tn, token, 0, unroll=32)
    for k in range(PEER_NKEYS):
        a_ref[:, k * PEER_NKEYS:(k + 1) * PEER_NKEYS] = planes_ref[pl.ds(k * plane, tn), :].astype(BF16)


def _peer_gates(i1, i2, gate):
    n, slots = i1.shape
    tn = _pick_tile(n, 128)
    ne = PEER_NKEYS * PEER_NKEYS
    row = lambda w: pl.BlockSpec((tn, w), lambda i: (i, 0))
    return pl.pallas_call(
        _peer_gates_kernel,
        grid=(n // tn,),
        in_specs=[row(slots)] * 3,
        out_specs=row(ne),
        out_shape=jax.ShapeDtypeStruct((n, ne), BF16),
        scratch_shapes=[pltpu.VMEM((PEER_NKEYS * (tn + _PLANE_PAD), PEER_NKEYS), F32)],
        compiler_params=_params(("parallel",)),
        name="peer_gates",
    )(i1, i2, gate)


_PEER_SUB = 256
def _peer_apply_kernel(final_norm, xn_ref, a_ref, u_ref, v_ref, h_ref, g_ref, o_ref, acc_ref):
    j = pl.program_id(1)

    @pl.when(j == 0)
    def _():
        acc_ref[...] = jnp.zeros_like(acc_ref)

    xn = xn_ref[...]
    ws = []
    for s in range(u_ref.shape[0] // _PEER_SUB):
        sub = slice(s * _PEER_SUB, (s + 1) * _PEER_SUB)
        act = _gelu(_dot_nt(xn, u_ref[sub, :]))
        ws.append((act * a_ref[:, sub].astype(F32)).astype(BF16))
    acc_ref[...] += _dot(jnp.concatenate(ws, axis=1), v_ref[...])

    @pl.when(j == pl.num_programs(1) - 1)
    def _():
        y = h_ref[...] + acc_ref[...]
        o_ref[...] = _rms(y, g_ref[...]) if final_norm else y


def _peer_apply(xn, a, u, v, h, g, final_norm):
    n, d = xn.shape
    ne = u.shape[0]
    tn = _pick_tile(n, 512)
    te = 8 * _PEER_SUB
    return pl.pallas_call(
        functools.partial(_peer_apply_kernel, final_norm),
        grid=(n // tn, ne // te),
        in_specs=[pl.BlockSpec((tn, d), lambda i, j: (i, 0)), pl.BlockSpec((tn, te), lambda i, j: (i, j)),
                  pl.BlockSpec((te, d), lambda i, j: (j, 0)), pl.BlockSpec((te, d), lambda i, j: (j, 0)),
                  pl.BlockSpec((tn, d), lambda i, j: (i, 0)), _const_spec(g.shape)],
        out_specs=pl.BlockSpec((tn, d), lambda i, j: (i, 0)),
        out_shape=jax.ShapeDtypeStruct((n, d), F32),
        scratch_shapes=[pltpu.VMEM((tn, d), F32)],
        compiler_params=_params(("parallel", "arbitrary")),
        name="peer_apply",
    )(xn, a, u, v, h, g)


def _split_w_in(w_in, d):
    widths = (d, DSA_HEADS * DSA_LATENT, DSA_LATENT, IDX_HEADS * IDX_DIM, IDX_DIM, IDX_HEADS,
              MEM_HEADS * MEM_HEAD_DIM, N_BRANCH * d)
    offs = np.cumsum((0,) + widths)
    wu, wq, wc, wqi, wki, wwi, wmq, wg = [w_in[:, offs[i]:offs[i + 1]] for i in range(8)]
    wkw = jnp.pad(jnp.concatenate([wki, wwi], axis=1), ((0, 0), (0, LANES - IDX_DIM - IDX_HEADS)))
    kw_scale = jnp.concatenate([jnp.ones((IDX_DIM,), F32), jnp.full((IDX_HEADS,), IDX_HEADS ** -0.5, F32),
                                jnp.zeros((LANES - IDX_DIM - IDX_HEADS,), F32)])[None, :]
    wqi = jnp.pad(wqi.reshape(-1, IDX_HEADS, IDX_DIM), ((0, 0), (0, 0), (0, LANES - IDX_DIM))).reshape(-1, IDX_HEADS * LANES)
    ws = [wu, wq * (DSA_LATENT ** -0.5 * LOG2E), wc, wqi, wkw, wmq * MEM_HEAD_DIM ** -0.5, wg]
    return [w.astype(BF16) for w in ws], kw_scale


def _pad_rows(a, n):
    return a if a.shape[0] == n else jnp.pad(a, ((0, n - a.shape[0]),) + ((0, 0),) * (a.ndim - 1))


def _pad_keys(a, s_pad):
    return jnp.pad(a, ((0, 0), (0, s_pad - a.shape[1]), (0, 0)))


def _token_mix(h, seq_shape, lw, mats, state, cache, mem_kv, pos0):
    b, l = seq_shape
    bf = lambda a: a.astype(BF16)
    seq = lambda a: a.reshape(b, l, -1)
    u, q, c, qi, kw, mq, g = _inproj(h, lw["g_norm1"], lw["g_kv"], lw["kw_scale"], lw["w_in"])
    ki, wi = kw[:, :IDX_DIM], kw[:, IDX_DIM:IDX_DIM + IDX_HEADS]

    ys, s_re, s_im = _ssm_branch(seq(u), mats, state[0], state[1])

    c_all, k_all = bf(seq(c)), bf(seq(ki))
    if cache is not None:
        c_all = jnp.concatenate([bf(cache[0]), c_all], axis=1)
        k_all = jnp.concatenate([bf(cache[1]), k_all], axis=1)
    s_all = c_all.shape[1]
    s_pad = -(-s_all // 1024) * 1024 if cache is not None else s_all
    yd = _dsa(seq(q), seq(qi), seq(wi), _pad_keys(k_all, s_pad), _pad_keys(c_all, s_pad), lw["w_uv"], pos0, s_all,
              min(DSA_TOPK, s_all // 4), _pick_tile(l, _DSA_TQ))

    ym = _memattn(seq(mq), mem_kv[0], mem_kv[1])

    flat = lambda a: a.reshape(b * l, -1)
    h2 = _merge(h, flat(ys), flat(yd), flat(ym), g, lw["w_glu"], lw["b_glu"], lw["w_br_ssm"], lw["w_br_dsa"],
                lw["w_br_mem"], lw["w_out"])
    return h2, seq(c), seq(ki), s_re, s_im


def _channel_mix(h2, lw, g_final, final_norm):
    n = h2.shape[0]
    h2 = _pad_rows(h2, -(-n // LANES) * LANES)
    xn, i1, i2, gate = _peer_route(h2, lw["g_norm2"], lw["w_peer_q"], lw["peer_sub_k1"], lw["peer_sub_k2"])
    a = _peer_gates(i1, i2, gate)
    return _peer_apply(xn, a, lw["peer_u"], lw["peer_v"], h2, g_final, final_norm)[:n]


def kernel(x_prompt, x_sample, mem_prompt, cache_dsa_latent, cache_dsa_idx_k, state_ssm_re, state_ssm_im, cache_mem_k, cache_mem_v, g_norm1, w_in, g_kv, w_uv, ssm_lam_re, ssm_lam_im, ssm_log_dt, ssm_b_re, ssm_b_im, ssm_c_re, ssm_c_im, ssm_d, w_glu, b_glu, g_mem, w_mem_kv, w_br_ssm, w_br_dsa, w_br_mem, w_out, g_norm2, w_peer_q, peer_sub_k1, peer_sub_k2, peer_u, peer_v, g_final):
    depth = w_in.shape[0]
    bp, lp, d = x_prompt.shape
    bs, ls, _ = x_sample.shape
    past = cache_dsa_latent.shape[2]
    mem_w = MEM_HEADS * MEM_HEAD_DIM
    n_mem = mem_prompt.shape[1]
    heads = (n_mem, MEM_HEADS, MEM_HEAD_DIM)
    bf = lambda a: a.astype(BF16)
    row = lambda a: a[None, :]

    hp, hs = x_prompt.reshape(bp * lp, d), x_sample.reshape(bs * ls, d)
    outs = [[] for _ in range(10)]
    for l in range(depth):
        ws, kw_scale = _split_w_in(w_in[l], d)
        lw = dict(g_norm1=row(g_norm1[l]), g_kv=row(g_kv[l]), kw_scale=kw_scale, w_in=ws, w_uv=bf(w_uv[l]),
                  w_glu=bf(w_glu[l]), b_glu=row(b_glu[l]), w_br_ssm=bf(w_br_ssm[l]), w_br_dsa=bf(w_br_dsa[l]),
                  w_br_mem=bf(w_br_mem[l]), w_out=bf(w_out[l]), g_norm2=row(g_norm2[l]), w_peer_q=bf(w_peer_q[l]),
                  peer_sub_k1=bf(peer_sub_k1[l]), peer_sub_k2=bf(peer_sub_k2[l]), peer_u=bf(peer_u[l]),
                  peer_v=bf(peer_v[l]))
        mats = _ssm_matrices(ssm_lam_re[l], ssm_lam_im[l], ssm_log_dt[l], ssm_b_re[l], ssm_b_im[l],
                             ssm_c_re[l], ssm_c_im[l], ssm_d[l], SSM_T)
        kv = _memkv(mem_prompt.reshape(bp * n_mem, d), row(g_mem[l]), bf(w_mem_kv[l]))
        mk_p, mv_p = kv[:, :mem_w].reshape(bp, n_mem, mem_w), kv[:, mem_w:].reshape(bp, n_mem, mem_w)
        zeros = jnp.zeros((bp,) + state_ssm_re.shape[2:], F32)

        h2p, c_p, ki_p, sre_p, sim_p = _token_mix(hp, (bp, lp), lw, mats, (zeros, zeros), None, (bf(mk_p), bf(mv_p)), 0)
        mem_s = (bf(cache_mem_k[l].reshape(bs, n_mem, mem_w)), bf(cache_mem_v[l].reshape(bs, n_mem, mem_w)))
        h2s, c_s, ki_s, sre_s, sim_s = _token_mix(hs, (bs, ls), lw, mats, (state_ssm_re[l], state_ssm_im[l]),
                                                  (cache_dsa_latent[l], cache_dsa_idx_k[l]), mem_s, past)
        final = l == depth - 1
        hp = _channel_mix(h2p, lw, row(g_final), final)
        hs = _channel_mix(h2s, lw, row(g_final), final)
        for lst, val in zip(outs, (c_p, ki_p, sre_p, sim_p, mk_p.reshape((bp,) + heads), mv_p.reshape((bp,) + heads),
                                   c_s, ki_s, sre_s, sim_s)):
            lst.append(val)
    return (hp.reshape(bp, lp, d), hs.reshape(bs, ls, d)) + tuple(jnp.stack(o) for o in outs)
```

```python
import functools
import math

import jax
import jax.numpy as jnp
import numpy as np
from jax import lax
from jax.experimental import pallas as pl
from jax.experimental.pallas import tpu as pltpu

F32 = jnp.float32
BF16 = jnp.bfloat16
I32 = jnp.int32

EPS = 1e-6
CHUNK = 64
SSM_GROUP = 16
SSM_STATE = 64
SSM_T = 32
DSA_HEADS = 8
DSA_LATENT = 256
DSA_HEAD_DIM = 128
IDX_HEADS = 8
IDX_DIM = 64
DSA_TOPK = 256
MEM_HEADS = 4
MEM_HEAD_DIM = 256
PEER_HEADS = 8
PEER_NKEYS = 128
PEER_HALF = 128
PEER_TOPK = 16
N_BRANCH = 3

LANES = 128
VMEM_LIMIT = 56 * 1024 * 1024
INT_MIN = -2 ** 31
INT_MAX = 2 ** 31 - 1
NEG_BIG = -1e30
LOG2E = 1.4426950408889634


def _pick_tile(n, target):
    if n <= target:
        return n
    for t in range(target, 7, -1):
        if n % t == 0 and t % 8 == 0:
            return t
    return n


def _params(sem):
    return pltpu.CompilerParams(dimension_semantics=sem, vmem_limit_bytes=VMEM_LIMIT)


def _const_spec(shape):
    nd = len(shape)
    return pl.BlockSpec(shape, lambda *_: (0,) * nd, pipeline_mode=pl.Buffered(1))


def _rms(x, g):
    return x * lax.rsqrt(jnp.mean(x * x, axis=-1, keepdims=True) + EPS) * g


def _gelu(x):
    return 0.5 * x * (1.0 + jnp.tanh(0.7978845608028654 * (x + 0.044715 * x * x * x)))


def _sigmoid(x):
    return 1.0 / (1.0 + jnp.exp(-x))


def _dot_nt(a, b):
    return lax.dot_general(a, b, (((1,), (1,)), ((), ())), preferred_element_type=F32)


def _dot(a, b):
    return jnp.dot(a, b, preferred_element_type=F32)


def _inproj_kernel(x_ref, g1_ref, gkv_ref, kws_ref, wu_ref, wq_ref, wc_ref, wqi_ref, wkw_ref, wmq_ref, wg_ref,
                   u_ref, q_ref, c_ref, qi_ref, kw_ref, mq_ref, g_ref):
    xn = _rms(x_ref[...], g1_ref[...]).astype(BF16)
    u_ref[...] = _dot(xn, wu_ref[...]).astype(BF16)
    q_ref[...] = _dot(xn, wq_ref[...]).astype(BF16)
    c_ref[...] = _rms(_dot(xn, wc_ref[...]), gkv_ref[...])
    qi_ref[...] = _dot(xn, wqi_ref[...]).astype(BF16)
    kw_ref[...] = _dot(xn, wkw_ref[...]) * kws_ref[...]
    mq_ref[...] = _dot(xn, wmq_ref[...]).astype(BF16)
    g_ref[...] = _sigmoid(_dot(xn, wg_ref[...])).astype(BF16)


def _inproj(h, g1, gkv, kw_scale, ws):
    n, d = h.shape
    tm = _pick_tile(n, 256)
    widths = [w.shape[1] for w in ws]
    dts = [BF16, BF16, F32, BF16, F32, BF16, BF16]
    row = lambda w: pl.BlockSpec((tm, w), lambda i: (i, 0))
    return pl.pallas_call(
        _inproj_kernel,
        grid=(n // tm,),
        in_specs=[row(d), _const_spec(g1.shape), _const_spec(gkv.shape), _const_spec(kw_scale.shape)]
        + [_const_spec(w.shape) for w in ws],
        out_specs=[row(w) for w in widths],
        out_shape=[jax.ShapeDtypeStruct((n, w), dt) for w, dt in zip(widths, dts)],
        compiler_params=_params(("parallel",)),
        name="inproj",
    )(h, g1, gkv, kw_scale, *ws)


def _memkv_kernel(x_ref, g_ref, w_ref, o_ref):
    xn = _rms(x_ref[...], g_ref[...]).astype(BF16)
    o_ref[...] = _dot(xn, w_ref[...])


def _memkv(mem, g, w):
    n, d = mem.shape
    tm = _pick_tile(n, 256)
    return pl.pallas_call(
        _memkv_kernel,
        grid=(n // tm,),
        in_specs=[pl.BlockSpec((tm, d), lambda i: (i, 0)), _const_spec(g.shape), _const_spec(w.shape)],
        out_specs=pl.BlockSpec((tm, w.shape[1]), lambda i: (i, 0)),
        out_shape=jax.ShapeDtypeStruct((n, w.shape[1]), F32),
        compiler_params=_params(("parallel",)),
        name="memkv",
    )(mem, g, w)


def _memattn_kernel(q_ref, k_ref, v_ref, o_ref):
    for hd in range(MEM_HEADS):
        sl = slice(hd * MEM_HEAD_DIM, (hd + 1) * MEM_HEAD_DIM)
        logits = _dot_nt(q_ref[0, :, sl], k_ref[0, :, sl])
        m = jnp.max(logits, axis=-1, keepdims=True)
        p = jnp.exp(logits - m)
        l = jnp.sum(p, axis=-1, keepdims=True)
        o = _dot(p.astype(BF16), v_ref[0, :, sl]) / l
        o_ref[0, :, sl] = o.astype(BF16)


def _memattn(q, k, v):
    b, l, w = q.shape
    tl = _pick_tile(l, 512)
    nm = k.shape[1]
    return pl.pallas_call(
        _memattn_kernel,
        grid=(b, l // tl),
        in_specs=[pl.BlockSpec((1, tl, w), lambda i, j: (i, j, 0)),
                  pl.BlockSpec((1, nm, w), lambda i, j: (i, 0, 0)),
                  pl.BlockSpec((1, nm, w), lambda i, j: (i, 0, 0))],
        out_specs=pl.BlockSpec((1, tl, w), lambda i, j: (i, j, 0)),
        out_shape=jax.ShapeDtypeStruct((b, l, w), BF16),
        compiler_params=_params(("parallel", "parallel")),
        name="memattn",
    )(q, k, v)


def _merge_kernel(h_ref, ys_ref, yd_ref, ym_ref, g_ref, wglu_ref, bglu_ref, wbs_ref, wbd_ref, wbm_ref, wout_ref,
                  o_ref):
    d = h_ref.shape[1]
    z = _gelu(ys_ref[...].astype(F32))
    gate = _sigmoid(_dot(z.astype(BF16), wglu_ref[...]) + bglu_ref[...])
    a = _dot((z * gate).astype(BF16), wbs_ref[...])
    b = _dot(yd_ref[...], wbd_ref[...])
    c = _dot(ym_ref[...], wbm_ref[...])
    g = g_ref[...].astype(F32)
    merged = g[:, 0:d] * a + g[:, d:2 * d] * b + g[:, 2 * d:3 * d] * c
    o_ref[...] = h_ref[...] + _dot(merged.astype(BF16), wout_ref[...])


def _merge(h, ys, yd, ym, g, wglu, bglu, wbs, wbd, wbm, wout):
    n, d = h.shape
    tm = _pick_tile(n, 512)
    row = lambda w: pl.BlockSpec((tm, w), lambda i: (i, 0))
    consts = [wglu, bglu, wbs, wbd, wbm, wout]
    return pl.pallas_call(
        _merge_kernel,
        grid=(n // tm,),
        in_specs=[row(d), row(d), row(d), row(d), row(3 * d)] + [_const_spec(c.shape) for c in consts],
        out_specs=row(d),
        out_shape=jax.ShapeDtypeStruct((n, d), F32),
        compiler_params=_params(("parallel",)),
        name="merge",
    )(h, ys, yd, ym, g, *consts)


def _ssm_matrices(lam_re, lam_im, log_dt, b_re, b_im, c_re, c_im, d, t_len):
    hi = lax.Precision.HIGHEST
    g_n, p_n = lam_re.shape
    dt = jnp.exp(log_dt)[:, None]
    mag = jnp.exp(lam_re * dt)
    ar, ai = mag * jnp.cos(lam_im * dt), mag * jnp.sin(lam_im * dt)
    den = lam_re * lam_re + lam_im * lam_im
    nr, ni = ar - 1.0, ai
    kr = ((nr * lam_re + ni * lam_im) / den)[..., None]
    ki = ((ni * lam_re - nr * lam_im) / den)[..., None]
    bbr, bbi = kr * b_re - ki * b_im, kr * b_im + ki * b_re
    j = jnp.arange(t_len + 1, dtype=F32)[:, None, None]
    pmag = jnp.exp(j * (lam_re * dt))
    pr, pi = pmag * jnp.cos(j * (lam_im * dt)), pmag * jnp.sin(j * (lam_im * dt))
    mr = pr[:t_len, ..., None] * bbr - pi[:t_len, ..., None] * bbi
    mi = pr[:t_len, ..., None] * bbi + pi[:t_len, ..., None] * bbr
    kern = (jnp.einsum('gdp,jgpc->jgdc', c_re, mr, precision=hi)
            - jnp.einsum('gdp,jgpc->jgdc', c_im, mi, precision=hi))
    s_i = jnp.arange(t_len)[:, None]
    t_i = jnp.arange(t_len)[None, :]
    lag = t_i - s_i
    kg = jnp.where((lag >= 0)[:, :, None, None, None], kern[jnp.clip(lag, 0)], 0.0)
    tz = kg.transpose(2, 0, 4, 1, 3).reshape(g_n, t_len * SSM_GROUP, t_len * SSM_GROUP)
    pad = ((0, 0), (0, 0), (0, LANES - p_n))
    vr = jnp.pad(mr[::-1].transpose(1, 0, 3, 2).reshape(g_n, t_len * SSM_GROUP, p_n), pad)
    vi = jnp.pad(mi[::-1].transpose(1, 0, 3, 2).reshape(g_n, t_len * SSM_GROUP, p_n), pad)
    tzv = jnp.concatenate([tz, vr, vi], axis=-1).astype(BF16)
    pr1, pi1 = pr[1:].transpose(1, 2, 0), pi[1:].transpose(1, 2, 0)
    crt, cit = c_re.transpose(0, 2, 1), c_im.transpose(0, 2, 1)
    wre = crt[:, :, None, :] * pr1[..., None] - cit[:, :, None, :] * pi1[..., None]
    wim = -(crt[:, :, None, :] * pi1[..., None] + cit[:, :, None, :] * pr1[..., None])
    rpad = ((0, 0), (0, LANES - p_n), (0, 0))
    wre = jnp.pad(wre.reshape(g_n, p_n, -1), rpad).astype(BF16)
    wim = jnp.pad(wim.reshape(g_n, p_n, -1), rpad).astype(BF16)
    atr = jnp.pad(pr[t_len], ((0, 0), (0, LANES - p_n)))[:, None, :]
    ati = jnp.pad(pi[t_len], ((0, 0), (0, LANES - p_n)))[:, None, :]
    drow = jnp.tile(d.reshape(g_n, 1, SSM_GROUP), (1, t_len, 1)).reshape(g_n, 1, t_len * SSM_GROUP)
    return tzv, wre, wim, atr, ati, drow


def _ssm_kernel(nk, nb, u_ref, tzv_ref, wre_ref, wim_ref, atr_ref, ati_ref, d_ref, ire_ref, iim_ref,
                y_ref, fre_ref, fim_ref, yi_ref, sr_ref, si_ref, xr_ref, xi_ref):
    tc = u_ref.shape[2]
    u = u_ref[0]
    full = _dot(u, tzv_ref[0])
    yi_ref[...] = full[:, :tc]
    sr_ref[...] = full[:, tc:tc + LANES]
    si_ref[...] = full[:, tc + LANES:]
    atr, ati = atr_ref[0], ati_ref[0]

    def step(k, carry):
        xr, xi = carry
        rows = pl.ds(pl.multiple_of(k * nb, nb), nb)
        xr_ref[rows, :] = xr
        xi_ref[rows, :] = xi
        return (atr * xr - ati * xi + sr_ref[rows, :], atr * xi + ati * xr + si_ref[rows, :])

    xr, xi = lax.fori_loop(0, nk, step, (ire_ref[0], iim_ref[0]))
    fre_ref[0] = xr
    fim_ref[0] = xi
    y = (yi_ref[...] + _dot(xr_ref[...].astype(BF16), wre_ref[0]) + _dot(xi_ref[...].astype(BF16), wim_ref[0])
         + d_ref[0] * u.astype(F32))
    y_ref[0] = y.astype(BF16)


def _ssm(u, mats, init_re, init_im, nk, nb):
    tzv, wre, wim, atr, ati, drow = mats
    g_n, r, tc = u.shape
    per_g = lambda a: pl.BlockSpec((1,) + a.shape[1:], lambda g: (g, 0, 0))
    ins = [u, tzv, wre, wim, atr, ati, drow, init_re, init_im]
    st = jax.ShapeDtypeStruct((g_n, nb, LANES), F32)
    return pl.pallas_call(
        functools.partial(_ssm_kernel, nk, nb),
        grid=(g_n,),
        in_specs=[per_g(a) for a in ins],
        out_specs=[per_g(u), per_g(init_re), per_g(init_re)],
        out_shape=[jax.ShapeDtypeStruct(u.shape, BF16), st, st],
        scratch_shapes=[pltpu.VMEM((r, tc), F32)] + [pltpu.VMEM((r, LANES), F32)] * 4,
        compiler_params=_params(("parallel",)),
        name="ssm",
    )(*ins)


def _ssm_branch(zu, mats, st_re, st_im):
    b, l, w = zu.shape
    g_n, p_n = st_re.shape[1], st_re.shape[2]
    nk = l // SSM_T
    ug = zu.reshape(b, nk, SSM_T, g_n, SSM_GROUP).transpose(3, 1, 0, 2, 4).reshape(g_n, nk * b, SSM_T * SSM_GROUP)
    pad = ((0, 0), (0, 0), (0, LANES - p_n))
    ire = jnp.pad(st_re.transpose(1, 0, 2), pad)
    iim = jnp.pad(st_im.transpose(1, 0, 2), pad)
    y, fre, fim = _ssm(ug, mats, ire, iim, nk, b)
    y = y.reshape(g_n, nk, b, SSM_T, SSM_GROUP).transpose(2, 1, 3, 0, 4).reshape(b, l, w)
    return y, fre[:, :, :p_n].transpose(1, 0, 2), fim[:, :, :p_n].transpose(1, 0, 2)


_DSA_TQ = 256
_DSA_VALUE_STEPS = 24
_DSA_SK = 1024
_DSA_MIN_SUM = 2.0 ** -80


def _dsa_kernel(tq, sk, pos0, s_valid, n_sel, qlat_ref, qidx_ref, w_ref, ka_ref, c_ref, cmax_ref, wuv_ref,
                o_ref, key_ref, bias_ref, j_ref):
    s_pad = key_ref.shape[1]
    q0 = pos0 + pl.program_id(1) * tq
    rowpos = q0 + lax.broadcasted_iota(I32, (tq, 1), 0)
    vis = jnp.minimum((rowpos // CHUNK + 1) * CHUNK, s_valid)
    vis_max = jnp.minimum(((q0 + tq - 1) // CHUNK + 1) * CHUNK, s_valid)
    nch = (vis_max + sk - 1) // sk
    lane = lax.broadcasted_iota(I32, (tq, sk), 1)
    kslice = lambda j: pl.ds(pl.multiple_of(j * sk, sk), sk)

    qs = jnp.concatenate([qidx_ref[0, :, hd * LANES:(hd + 1) * LANES] for hd in range(IDX_HEADS)], axis=0)

    nblk = sk // LANES

    def score_chunk(j, carry):
        kmin, kmax = carry
        rel = jnp.maximum(_dot_nt(qs, ka_ref[0, kslice(j), :]), 0.0)
        acc = jnp.zeros((tq, sk), F32)
        for hd in range(IDX_HEADS):
            acc = acc + w_ref[0, :, hd:hd + 1] * rel[hd * tq:(hd + 1) * tq]
        bits = pltpu.bitcast(acc, I32)
        key = bits ^ ((bits >> 31) & 0x7FFFFFFF)
        visible = j * sk + lane < vis
        key_ref[:, kslice(j)] = jnp.where(visible, key, INT_MIN)
        lo_key, hi_key = jnp.where(visible, key, INT_MAX), jnp.where(visible, key, INT_MIN)
        for i in range(nblk):
            kmin = jnp.minimum(kmin, lo_key[:, i * LANES:(i + 1) * LANES])
            kmax = jnp.maximum(kmax, hi_key[:, i * LANES:(i + 1) * LANES])
        return kmin, kmax

    kmin, kmax = lax.fori_loop(0, nch, score_chunk,
                               (jnp.full((tq, LANES), INT_MAX, I32), jnp.full((tq, LANES), INT_MIN, I32)))
    kmin = jnp.min(kmin, axis=1, keepdims=True)
    kmax = jnp.max(kmax, axis=1, keepdims=True)

    def count(pred):
        def body(j, c):
            hit = jnp.where(pred(key_ref[:, kslice(j)], j * sk + lane), 1.0, 0.0)
            for i in range(nblk):
                c = c + hit[:, i * LANES:(i + 1) * LANES]
            return c
        return jnp.sum(lax.fori_loop(0, nch, body, jnp.zeros((tq, LANES), F32)), axis=1, keepdims=True)

    key_of = lambda v: (lambda b: b ^ ((b >> 31) & 0x7FFFFFFF))(pltpu.bitcast(v, I32))
    val_of = lambda k: pltpu.bitcast(k ^ ((k >> 31) & 0x7FFFFFFF), F32)

    def settled(lo, hi, cnt):
        return (cnt <= n_sel) | (hi - 1 <= lo)

    def halve(state):
        it, lo, hi, cnt, _ = state
        mid_v = key_of(0.5 * val_of(lo) + 0.5 * val_of(hi))
        mid_k = (lo >> 1) + (hi >> 1) + (lo & hi & 1)
        mid = jnp.where((mid_v > lo) & (mid_v < hi) & (it < _DSA_VALUE_STEPS), mid_v, mid_k)
        c = count(lambda k, col: k >= mid)
        live = jnp.logical_not(settled(lo, hi, cnt))
        up = live & (c >= n_sel)
        lo, cnt, hi = jnp.where(up, mid, lo), jnp.where(up, c, cnt), jnp.where(live & (c < n_sel), mid, hi)
        return it + 1, lo, hi, cnt, jnp.max(jnp.where(settled(lo, hi, cnt), 0, 1))

    c_pos, c_nn = count(lambda k, col: k >= 1), count(lambda k, col: k >= 0)
    pos, zero = c_pos >= n_sel, c_nn >= n_sel
    lo0 = jnp.where(pos, 1, jnp.where(zero, 0, kmin))
    hi0 = jnp.where(pos, jnp.where(kmax == INT_MAX, INT_MAX, kmax + 1), jnp.where(zero, 1, 0))
    cnt0 = jnp.where(pos, c_pos, jnp.where(zero, c_nn, vis.astype(F32)))
    state = (jnp.int32(0), lo0, hi0, cnt0, jnp.max(jnp.where(settled(lo0, hi0, cnt0), 0, 1)))
    _, thr, _, cnt, _ = lax.while_loop(lambda st: (st[0] < _DSA_VALUE_STEPS + 34) & (st[4] > 0), halve, state)
    thr = jnp.maximum(thr, INT_MIN + 1)
    excess = cnt > n_sel

    j_ref[...] = jnp.full(j_ref.shape, s_pad, I32)

    @pl.when(jnp.max(jnp.where(excess, 1.0, 0.0)) > 0.0)
    def _():
        need = n_sel - count(lambda k, col: k > thr)

        def idx_step(_, lohi):
            lo, hi = lohi
            mid = (lo + hi) >> 1
            ok = count(lambda k, col: (k == thr) & (col < mid)) >= need
            return jnp.where(ok, lo, mid), jnp.where(ok, mid, hi)

        steps = int(math.ceil(math.log2(s_pad))) + 1
        _, hi = lax.fori_loop(0, steps, idx_step, (jnp.zeros((tq, 1), I32), jnp.full((tq, 1), s_pad, I32)))
        j_ref[...] = jnp.broadcast_to(jnp.where(excess, hi, s_pad), j_ref.shape)

    jlim = j_ref[:, 0:1]

    rows = DSA_HEADS * tq
    q = jnp.concatenate([qlat_ref[0, :, hd * DSA_LATENT:(hd + 1) * DSA_LATENT] for hd in range(DSA_HEADS)], axis=0)

    wb = min(sk, 2 * LANES)

    lane_wb = lax.broadcasted_iota(I32, (tq, wb), 1)

    def logits(j, i, first):
        blk = pl.ds(pl.multiple_of(j * sk + i * wb, wb), wb)
        if first:
            k = key_ref[:, blk]
            sel = (k > thr) | ((k == thr) & (j * sk + i * wb + lane_wb < jlim))
            bias = jnp.where(sel, 0.0, NEG_BIG)
            bias_ref[:, blk] = bias
        else:
            bias = bias_ref[:, blk]
        s = _dot_nt(q, c_ref[0, blk, :])
        return (s.reshape(DSA_HEADS, tq, wb) + bias[None]).reshape(rows, wb)

    def attend(m, first):
        def acc_chunk(j, carry):
            l, acc = carry
            ps = []
            for i in range(sk // wb):
                s = logits(j, i, first)
                for t in range(wb // LANES):
                    p = jnp.exp2(s[:, t * LANES:(t + 1) * LANES] - m)
                    l = l + p
                    ps.append(p.astype(BF16))
            return l, acc + _dot(jnp.concatenate(ps, axis=1), c_ref[0, kslice(j), :])

        l, acc = lax.fori_loop(0, nch, acc_chunk, (jnp.zeros((rows, LANES), F32), jnp.zeros((rows, DSA_LATENT), F32)))
        return jnp.sum(l, axis=1, keepdims=True), acc

    def emit(l, acc):
        o = (acc / l).astype(BF16)
        for hd in range(DSA_HEADS):
            o_ref[0, :, hd * DSA_HEAD_DIM:(hd + 1) * DSA_HEAD_DIM] = _dot(o[hd * tq:(hd + 1) * tq], wuv_ref[hd]).astype(BF16)

    qf = q.astype(F32)
    bound = jnp.sqrt(jnp.sum(qf * qf, axis=1, keepdims=True)) * cmax_ref[0, 0:1, 0:1]
    l, acc = attend(jnp.broadcast_to(bound, (rows, LANES)), True)
    healthy = jnp.min(l) > _DSA_MIN_SUM

    @pl.when(healthy)
    def _():
        emit(l, acc)

    @pl.when(jnp.logical_not(healthy))
    def _():
        def max_chunk(j, mx):
            for i in range(sk // wb):
                s = logits(j, i, False)
                for t in range(wb // LANES):
                    mx = jnp.maximum(mx, s[:, t * LANES:(t + 1) * LANES])
            return mx

        mx = lax.fori_loop(0, nch, max_chunk, jnp.full((rows, LANES), NEG_BIG, F32))
        emit(*attend(jnp.broadcast_to(jnp.max(mx, axis=1, keepdims=True), (rows, LANES)), False))


def _dsa(qlat, qidx, w, kidx, c, wuv, pos0, s_valid, n_sel, tq):
    b, l, _ = qlat.shape
    s_pad = c.shape[1]
    sk = min(_DSA_SK, s_pad)
    ka = jnp.pad(kidx, ((0, 0), (0, 0), (0, LANES - IDX_DIM)))
    cf = c.astype(F32)
    cmax = jnp.broadcast_to(jnp.sqrt(jnp.max(jnp.sum(cf * cf, axis=2), axis=1))[:, None, None], (b, 1, LANES))
    qspec = lambda a: pl.BlockSpec((1, tq, a.shape[2]), lambda i, j: (i, j, 0))
    kspec = lambda a: pl.BlockSpec((1,) + a.shape[1:], lambda i, j: (i, 0, 0))
    dh = wuv.shape[0] * wuv.shape[2]
    return pl.pallas_call(
        functools.partial(_dsa_kernel, tq, sk, pos0, s_valid, n_sel),
        grid=(b, l // tq),
        in_specs=[qspec(qlat), qspec(qidx), qspec(w), kspec(ka), kspec(c), kspec(cmax), _const_spec(wuv.shape)],
        out_specs=pl.BlockSpec((1, tq, dh), lambda i, j: (i, j, 0)),
        out_shape=jax.ShapeDtypeStruct((b, l, dh), BF16),
        scratch_shapes=[pltpu.VMEM((tq, s_pad), I32), pltpu.VMEM((tq, s_pad), F32), pltpu.VMEM((tq, LANES), I32)],
        compiler_params=_params(("parallel", "arbitrary")),
        name="dsa",
    )(qlat, qidx, w, ka, c, cmax, wuv)


_PEER_PAIRS = [(r, c) for r in range(PEER_TOPK) for c in range(PEER_TOPK) if (r + 1) * (c + 1) <= PEER_TOPK]


def _top_rows(s, k):
    row = lax.broadcasted_iota(I32, s.shape, 0)
    vals, idxs = [], []
    for _ in range(k):
        m = jnp.max(s, axis=0, keepdims=True)
        first = jnp.min(jnp.where(s == m, row, s.shape[0]), axis=0, keepdims=True)
        s = jnp.where(row == first, -jnp.inf, s)
        vals.append(m)
        idxs.append(first)
    return jnp.concatenate(vals, axis=0), jnp.concatenate(idxs, axis=0)


def _peer_route_kernel(h_ref, g_ref, wq_ref, k1_ref, k2_ref, xn_ref, i1_ref, i2_ref, gate_ref):
    tn = h_ref.shape[0]
    xn = _rms(h_ref[...], g_ref[...]).astype(BF16)
    xn_ref[...] = xn
    q = _dot(xn, wq_ref[...]).astype(BF16)
    out1, out2, outg = [], [], []
    for hd in range(PEER_HEADS):
        qa = q[:, (2 * hd) * PEER_HALF:(2 * hd + 1) * PEER_HALF]
        qb = q[:, (2 * hd + 1) * PEER_HALF:(2 * hd + 2) * PEER_HALF]
        v1, i1 = _top_rows(_dot_nt(k1_ref[hd], qa), PEER_TOPK)
        v2, i2 = _top_rows(_dot_nt(k2_ref[hd], qb), PEER_TOPK)
        cand = jnp.concatenate([v1[r:r + 1] + v2[c:c + 1] for r, c in _PEER_PAIRS], axis=0)
        c1 = jnp.concatenate([i1[r:r + 1] for r, _ in _PEER_PAIRS], axis=0)
        c2 = jnp.concatenate([i2[c:c + 1] for _, c in _PEER_PAIRS], axis=0)
        row = lax.broadcasted_iota(I32, cand.shape, 0)
        tops, e1, e2 = [], [], []
        for _ in range(PEER_TOPK):
            m = jnp.max(cand, axis=0, keepdims=True)
            first = jnp.min(jnp.where(cand == m, row, cand.shape[0]), axis=0, keepdims=True)
            hit = row == first
            e1.append(jnp.sum(jnp.where(hit, c1, 0), axis=0, keepdims=True))
            e2.append(jnp.sum(jnp.where(hit, c2, 0), axis=0, keepdims=True))
            cand = jnp.where(hit, -jnp.inf, cand)
            tops.append(m)
        top = jnp.concatenate(tops, axis=0)
        ex = jnp.exp(top - top[0:1])
        outg.append(ex / jnp.sum(ex, axis=0, keepdims=True))
        out1.append(jnp.concatenate(e1, axis=0))
        out2.append(jnp.concatenate(e2, axis=0))
    as_f32 = lambda parts: pltpu.bitcast(jnp.concatenate(parts, axis=0), F32)
    i1_ref[...] = pltpu.bitcast(as_f32(out1).T, I32)
    i2_ref[...] = pltpu.bitcast(as_f32(out2).T, I32)
    gate_ref[...] = jnp.concatenate(outg, axis=0).T


def _peer_route(h, g, wq, k1, k2):
    n, d = h.shape
    tn = LANES
    slots = PEER_HEADS * PEER_TOPK
    row = lambda w: pl.BlockSpec((tn, w), lambda i: (i, 0))
    return pl.pallas_call(
        _peer_route_kernel,
        grid=(n // tn,),
        in_specs=[row(d), _const_spec(g.shape), _const_spec(wq.shape), _const_spec(k1.shape), _const_spec(k2.shape)],
        out_specs=[row(d), row(slots), row(slots), row(slots)],
        out_shape=[jax.ShapeDtypeStruct((n, d), BF16), jax.ShapeDtypeStruct((n, slots), I32),
                   jax.ShapeDtypeStruct((n, slots), I32), jax.ShapeDtypeStruct((n, slots), F32)],
        compiler_params=_params(("parallel",)),
        name="peer_route",
    )(h, g, wq, k1, k2)


_PLANE_PAD = 4


def _peer_gates_kernel(i1_ref, i2_ref, gate_ref, a_ref, planes_ref):
    tn, slots = i1_ref.shape
    plane = tn + _PLANE_PAD
    sub = lax.broadcasted_iota(I32, (PEER_NKEYS, slots), 0)

    def token(n, carry):
        r = pl.ds(n, 1)
        pt = jnp.where(sub == i1_ref[r, :], gate_ref[r, :], 0.0).astype(BF16)
        qt = jnp.where(sub == i2_ref[r, :], 1.0, 0.0).astype(BF16)
        planes_ref[pl.ds(n, PEER_NKEYS, stride=plane), :] = _dot_nt(pt, qt)
        return carry

    lax.fori_loop(0, tn, token, 0, unroll=32)
    for k in range(PEER_NKEYS):
        a_ref[:, k * PEER_NKEYS:(k + 1) * PEER_NKEYS] = planes_ref[pl.ds(k * plane, tn), :].astype(BF16)


def _peer_gates(i1, i2, gate):
    n, slots = i1.shape
    tn = _pick_tile(n, 128)
    ne = PEER_NKEYS * PEER_NKEYS
    row = lambda w: pl.BlockSpec((tn, w), lambda i: (i, 0))
    return pl.pallas_call(
        _peer_gates_kernel,
        grid=(n // tn,),
        in_specs=[row(slots)] * 3,
        out_specs=row(ne),
        out_shape=jax.ShapeDtypeStruct((n, ne), BF16),
        scratch_shapes=[pltpu.VMEM((PEER_NKEYS * (tn + _PLANE_PAD), PEER_NKEYS), F32)],
        compiler_params=_params(("parallel",)),
        name="peer_gates",
    )(i1, i2, gate)


_PEER_SUB = 256
def _peer_apply_kernel(final_norm, xn_ref, a_ref, u_ref, v_ref, h_ref, g_ref, o_ref, acc_ref):
    j = pl.program_id(1)

    @pl.when(j == 0)
    def _():
        acc_ref[...] = jnp.zeros_like(acc_ref)

    xn = xn_ref[...]
    ws = []
    for s in range(u_ref.shape[0] // _PEER_SUB):
        sub = slice(s * _PEER_SUB, (s + 1) * _PEER_SUB)
        act = _gelu(_dot_nt(xn, u_ref[sub, :]))
        ws.append((act * a_ref[:, sub].astype(F32)).astype(BF16))
    acc_ref[...] += _dot(jnp.concatenate(ws, axis=1), v_ref[...])

    @pl.when(j == pl.num_programs(1) - 1)
    def _():
        y = h_ref[...] + acc_ref[...]
        o_ref[...] = _rms(y, g_ref[...]) if final_norm else y


def _peer_apply(xn, a, u, v, h, g, final_norm):
    n, d = xn.shape
    ne = u.shape[0]
    tn = _pick_tile(n, 512)
    te = 8 * _PEER_SUB
    return pl.pallas_call(
        functools.partial(_peer_apply_kernel, final_norm),
        grid=(n // tn, ne // te),
        in_specs=[pl.BlockSpec((tn, d), lambda i, j: (i, 0)), pl.BlockSpec((tn, te), lambda i, j: (i, j)),
                  pl.BlockSpec((te, d), lambda i, j: (j, 0)), pl.BlockSpec((te, d), lambda i, j: (j, 0)),
                  pl.BlockSpec((tn, d), lambda i, j: (i, 0)), _const_spec(g.shape)],
        out_specs=pl.BlockSpec((tn, d), lambda i, j: (i, 0)),
        out_shape=jax.ShapeDtypeStruct((n, d), F32),
        scratch_shapes=[pltpu.VMEM((tn, d), F32)],
        compiler_params=_params(("parallel", "arbitrary")),
        name="peer_apply",
    )(xn, a, u, v, h, g)


def _split_w_in(w_in, d):
    widths = (d, DSA_HEADS * DSA_LATENT, DSA_LATENT, IDX_HEADS * IDX_DIM, IDX_DIM, IDX_HEADS,
              MEM_HEADS * MEM_HEAD_DIM, N_BRANCH * d)
    offs = np.cumsum((0,) + widths)
    wu, wq, wc, wqi, wki, wwi, wmq, wg = [w_in[:, offs[i]:offs[i + 1]] for i in range(8)]
    wkw = jnp.pad(jnp.concatenate([wki, wwi], axis=1), ((0, 0), (0, LANES - IDX_DIM - IDX_HEADS)))
    kw_scale = jnp.concatenate([jnp.ones((IDX_DIM,), F32), jnp.full((IDX_HEADS,), IDX_HEADS ** -0.5, F32),
                                jnp.zeros((LANES - IDX_DIM - IDX_HEADS,), F32)])[None, :]
    wqi = jnp.pad(wqi.reshape(-1, IDX_HEADS, IDX_DIM), ((0, 0), (0, 0), (0, LANES - IDX_DIM))).reshape(-1, IDX_HEADS * LANES)
    ws = [wu, wq * (DSA_LATENT ** -0.5 * LOG2E), wc, wqi, wkw, wmq * MEM_HEAD_DIM ** -0.5, wg]
    return [w.astype(BF16) for w in ws], kw_scale


def _pad_rows(a, n):
    return a if a.shape[0] == n else jnp.pad(a, ((0, n - a.shape[0]),) + ((0, 0),) * (a.ndim - 1))


def _pad_keys(a, s_pad):
    return jnp.pad(a, ((0, 0), (0, s_pad - a.shape[1]), (0, 0)))


def _token_mix(h, seq_shape, lw, mats, state, cache, mem_kv, pos0):
    b, l = seq_shape
    bf = lambda a: a.astype(BF16)
    seq = lambda a: a.reshape(b, l, -1)
    u, q, c, qi, kw, mq, g = _inproj(h, lw["g_norm1"], lw["g_kv"], lw["kw_scale"], lw["w_in"])
    ki, wi = kw[:, :IDX_DIM], kw[:, IDX_DIM:IDX_DIM + IDX_HEADS]

    ys, s_re, s_im = _ssm_branch(seq(u), mats, state[0], state[1])

    c_all, k_all = bf(seq(c)), bf(seq(ki))
    if cache is not None:
        c_all = jnp.concatenate([bf(cache[0]), c_all], axis=1)
        k_all = jnp.concatenate([bf(cache[1]), k_all], axis=1)
    s_all = c_all.shape[1]
    s_pad = -(-s_all // 1024) * 1024 if cache is not None else s_all
    yd = _dsa(seq(q), seq(qi), seq(wi), _pad_keys(k_all, s_pad), _pad_keys(c_all, s_pad), lw["w_uv"], pos0, s_all,
              min(DSA_TOPK, s_all // 4), _pick_tile(l, _DSA_TQ))

    ym = _memattn(seq(mq), mem_kv[0], mem_kv[1])

    flat = lambda a: a.reshape(b * l, -1)
    h2 = _merge(h, flat(ys), flat(yd), flat(ym), g, lw["w_glu"], lw["b_glu"], lw["w_br_ssm"], lw["w_br_dsa"],
                lw["w_br_mem"], lw["w_out"])
    return h2, seq(c), seq(ki), s_re, s_im


def _channel_mix(h2, lw, g_final, final_norm):
    n = h2.shape[0]
    h2 = _pad_rows(h2, -(-n // LANES) * LANES)
    xn, i1, i2, gate = _peer_route(h2, lw["g_norm2"], lw["w_peer_q"], lw["peer_sub_k1"], lw["peer_sub_k2"])
    a = _peer_gates(i1, i2, gate)
    return _peer_apply(xn, a, lw["peer_u"], lw["peer_v"], h2, g_final, final_norm)[:n]


def kernel(x_prompt, x_sample, mem_prompt, cache_dsa_latent, cache_dsa_idx_k, state_ssm_re, state_ssm_im, cache_mem_k, cache_mem_v, g_norm1, w_in, g_kv, w_uv, ssm_lam_re, ssm_lam_im, ssm_log_dt, ssm_b_re, ssm_b_im, ssm_c_re, ssm_c_im, ssm_d, w_glu, b_glu, g_mem, w_mem_kv, w_br_ssm, w_br_dsa, w_br_mem, w_out, g_norm2, w_peer_q, peer_sub_k1, peer_sub_k2, peer_u, peer_v, g_final):
    depth = w_in.shape[0]
    bp, lp, d = x_prompt.shape
    bs, ls, _ = x_sample.shape
    past = cache_dsa_latent.shape[2]
    mem_w = MEM_HEADS * MEM_HEAD_DIM
    n_mem = mem_prompt.shape[1]
    heads = (n_mem, MEM_HEADS, MEM_HEAD_DIM)
    bf = lambda a: a.astype(BF16)
    row = lambda a: a[None, :]

    hp, hs = x_prompt.reshape(bp * lp, d), x_sample.reshape(bs * ls, d)
    outs = [[] for _ in range(10)]
    for l in range(depth):
        ws, kw_scale = _split_w_in(w_in[l], d)
        lw = dict(g_norm1=row(g_norm1[l]), g_kv=row(g_kv[l]), kw_scale=kw_scale, w_in=ws, w_uv=bf(w_uv[l]),
                  w_glu=bf(w_glu[l]), b_glu=row(b_glu[l]), w_br_ssm=bf(w_br_ssm[l]), w_br_dsa=bf(w_br_dsa[l]),
                  w_br_mem=bf(w_br_mem[l]), w_out=bf(w_out[l]), g_norm2=row(g_norm2[l]), w_peer_q=bf(w_peer_q[l]),
                  peer_sub_k1=bf(peer_sub_k1[l]), peer_sub_k2=bf(peer_sub_k2[l]), peer_u=bf(peer_u[l]),
                  peer_v=bf(peer_v[l]))
        mats = _ssm_matrices(ssm_lam_re[l], ssm_lam_im[l], ssm_log_dt[l], ssm_b_re[l], ssm_b_im[l],
                             ssm_c_re[l], ssm_c_im[l], ssm_d[l], SSM_T)
        kv = _memkv(mem_prompt.reshape(bp * n_mem, d), row(g_mem[l]), bf(w_mem_kv[l]))
        mk_p, mv_p = kv[:, :mem_w].reshape(bp, n_mem, mem_w), kv[:, mem_w:].reshape(bp, n_mem, mem_w)
        zeros = jnp.zeros((bp,) + state_ssm_re.shape[2:], F32)

        h2p, c_p, ki_p, sre_p, sim_p = _token_mix(hp, (bp, lp), lw, mats, (zeros, zeros), None, (bf(mk_p), bf(mv_p)), 0)
        mem_s = (bf(cache_mem_k[l].reshape(bs, n_mem, mem_w)), bf(cache_mem_v[l].reshape(bs, n_mem, mem_w)))
        h2s, c_s, ki_s, sre_s, sim_s = _token_mix(hs, (bs, ls), lw, mats, (state_ssm_re[l], state_ssm_im[l]),
                                                  (cache_dsa_latent[l], cache_dsa_idx_k[l]), mem_s, past)
        final = l == depth - 1
        hp = _channel_mix(h2p, lw, row(g_final), final)
        hs = _channel_mix(h2s, lw, row(g_final), final)
        for lst, val in zip(outs, (c_p, ki_p, sre_p, sim_p, mk_p.reshape((bp,) + heads), mv_p.reshape((bp,) + heads),
                                   c_s, ki_s, sre_s, sim_s)):
            lst.append(val)
    return (hp.reshape(bp, lp, d), hs.reshape(bs, ls, d)) + tuple(jnp.stack(o) for o in outs)
```

```python
import functools
import math

import jax
import jax.numpy as jnp
import numpy as np
from jax import lax
from jax.experimental import pallas as pl
from jax.experimental.pallas import tpu as pltpu

F32 = jnp.float32
BF16 = jnp.bfloat16
I32 = jnp.int32

EPS = 1e-6
CHUNK = 64
SSM_GROUP = 16
SSM_STATE = 64
SSM_T = 32
DSA_HEADS = 8
DSA_LATENT = 256
DSA_HEAD_DIM = 128
IDX_HEADS = 8
IDX_DIM = 64
DSA_TOPK = 256
MEM_HEADS = 4
MEM_HEAD_DIM = 256
PEER_HEADS = 8
PEER_NKEYS = 128
PEER_HALF = 128
PEER_TOPK = 16
N_BRANCH = 3

LANES = 128
VMEM_LIMIT = 56 * 1024 * 1024
INT_MIN = -2 ** 31
INT_MAX = 2 ** 31 - 1
NEG_BIG = -1e30
LOG2E = 1.4426950408889634


def _pick_tile(n, target):
    if n <= target:
        return n
    for t in range(target, 7, -1):
        if n % t == 0 and t % 8 == 0:
            return t
    return n


def _params(sem):
    return pltpu.CompilerParams(dimension_semantics=sem, vmem_limit_bytes=VMEM_LIMIT)


def _const_spec(shape):
    nd = len(shape)
    return pl.BlockSpec(shape, lambda *_: (0,) * nd, pipeline_mode=pl.Buffered(1))


def _rms(x, g):
    return x * lax.rsqrt(jnp.mean(x * x, axis=-1, keepdims=True) + EPS) * g


def _gelu(x):
    return 0.5 * x * (1.0 + jnp.tanh(0.7978845608028654 * (x + 0.044715 * x * x * x)))


def _sigmoid(x):
    return 1.0 / (1.0 + jnp.exp(-x))


def _dot_nt(a, b):
    return lax.dot_general(a, b, (((1,), (1,)), ((), ())), preferred_element_type=F32)


def _dot(a, b):
    return jnp.dot(a, b, preferred_element_type=F32)


def _inproj_kernel(x_ref, g1_ref, gkv_ref, kws_ref, wu_ref, wq_ref, wc_ref, wqi_ref, wkw_ref, wmq_ref, wg_ref,
                   u_ref, q_ref, c_ref, qi_ref, kw_ref, mq_ref, g_ref):
    xn = _rms(x_ref[...], g1_ref[...]).astype(BF16)
    u_ref[...] = _dot(xn, wu_ref[...]).astype(BF16)
    q_ref[...] = _dot(xn, wq_ref[...]).astype(BF16)
    c_ref[...] = _rms(_dot(xn, wc_ref[...]), gkv_ref[...])
    qi_ref[...] = _dot(xn, wqi_ref[...]).astype(BF16)
    kw_ref[...] = _dot(xn, wkw_ref[...]) * kws_ref[...]
    mq_ref[...] = _dot(xn, wmq_ref[...]).astype(BF16)
    g_ref[...] = _sigmoid(_dot(xn, wg_ref[...])).astype(BF16)


def _inproj(h, g1, gkv, kw_scale, ws):
    n, d = h.shape
    tm = _pick_tile(n, 256)
    widths = [w.shape[1] for w in ws]
    dts = [BF16, BF16, F32, BF16, F32, BF16, BF16]
    row = lambda w: pl.BlockSpec((tm, w), lambda i: (i, 0))
    return pl.pallas_call(
        _inproj_kernel,
        grid=(n // tm,),
        in_specs=[row(d), _const_spec(g1.shape), _const_spec(gkv.shape), _const_spec(kw_scale.shape)]
        + [_const_spec(w.shape) for w in ws],
        out_specs=[row(w) for w in widths],
        out_shape=[jax.ShapeDtypeStruct((n, w), dt) for w, dt in zip(widths, dts)],
        compiler_params=_params(("parallel",)),
        name="inproj",
    )(h, g1, gkv, kw_scale, *ws)


def _memkv_kernel(x_ref, g_ref, w_ref, o_ref):
    xn = _rms(x_ref[...], g_ref[...]).astype(BF16)
    o_ref[...] = _dot(xn, w_ref[...])


def _memkv(mem, g, w):
    n, d = mem.shape
    tm = _pick_tile(n, 256)
    return pl.pallas_call(
        _memkv_kernel,
        grid=(n // tm,),
        in_specs=[pl.BlockSpec((tm, d), lambda i: (i, 0)), _const_spec(g.shape), _const_spec(w.shape)],
        out_specs=pl.BlockSpec((tm, w.shape[1]), lambda i: (i, 0)),
        out_shape=jax.ShapeDtypeStruct((n, w.shape[1]), F32),
        compiler_params=_params(("parallel",)),
        name="memkv",
    )(mem, g, w)


def _memattn_kernel(q_ref, k_ref, v_ref, o_ref):
    for hd in range(MEM_HEADS):
        sl = slice(hd * MEM_HEAD_DIM, (hd + 1) * MEM_HEAD_DIM)
        logits = _dot_nt(q_ref[0, :, sl], k_ref[0, :, sl])
        m = jnp.max(logits, axis=-1, keepdims=True)
        p = jnp.exp(logits - m)
        l = jnp.sum(p, axis=-1, keepdims=True)
        o = _dot(p.astype(BF16), v_ref[0, :, sl]) / l
        o_ref[0, :, sl] = o.astype(BF16)


def _memattn(q, k, v):
    b, l, w = q.shape
    tl = _pick_tile(l, 512)
    nm = k.shape[1]
    return pl.pallas_call(
        _memattn_kernel,
        grid=(b, l // tl),
        in_specs=[pl.BlockSpec((1, tl, w), lambda i, j: (i, j, 0)),
                  pl.BlockSpec((1, nm, w), lambda i, j: (i, 0, 0)),
                  pl.BlockSpec((1, nm, w), lambda i, j: (i, 0, 0))],
        out_specs=pl.BlockSpec((1, tl, w), lambda i, j: (i, j, 0)),
        out_shape=jax.ShapeDtypeStruct((b, l, w), BF16),
        compiler_params=_params(("parallel", "parallel")),
        name="memattn",
    )(q, k, v)


def _merge_kernel(h_ref, ys_ref, yd_ref, ym_ref, g_ref, wglu_ref, bglu_ref, wbs_ref, wbd_ref, wbm_ref, wout_ref,
                  o_ref):
    d = h_ref.shape[1]
    z = _gelu(ys_ref[...].astype(F32))
    gate = _sigmoid(_dot(z.astype(BF16), wglu_ref[...]) + bglu_ref[...])
    a = _dot((z * gate).astype(BF16), wbs_ref[...])
    b = _dot(yd_ref[...], wbd_ref[...])
    c = _dot(ym_ref[...], wbm_ref[...])
    g = g_ref[...].astype(F32)
    merged = g[:, 0:d] * a + g[:, d:2 * d] * b + g[:, 2 * d:3 * d] * c
    o_ref[...] = h_ref[...] + _dot(merged.astype(BF16), wout_ref[...])


def _merge(h, ys, yd, ym, g, wglu, bglu, wbs, wbd, wbm, wout):
    n, d = h.shape
    tm = _pick_tile(n, 512)
    row = lambda w: pl.BlockSpec((tm, w), lambda i: (i, 0))
    consts = [wglu, bglu, wbs, wbd, wbm, wout]
    return pl.pallas_call(
        _merge_kernel,
        grid=(n // tm,),
        in_specs=[row(d), row(d), row(d), row(d), row(3 * d)] + [_const_spec(c.shape) for c in consts],
        out_specs=row(d),
        out_shape=jax.ShapeDtypeStruct((n, d), F32),
        compiler_params=_params(("parallel",)),
        name="merge",
    )(h, ys, yd, ym, g, *consts)


def _ssm_matrices(lam_re, lam_im, log_dt, b_re, b_im, c_re, c_im, d, t_len):
    hi = lax.Precision.HIGHEST
    g_n, p_n = lam_re.shape
    dt = jnp.exp(log_dt)[:, None]
    mag = jnp.exp(lam_re * dt)
    ar, ai = mag * jnp.cos(lam_im * dt), mag * jnp.sin(lam_im * dt)
    den = lam_re * lam_re + lam_im * lam_im
    nr, ni = ar - 1.0, ai
    kr = ((nr * lam_re + ni * lam_im) / den)[..., None]
    ki = ((ni * lam_re - nr * lam_im) / den)[..., None]
    bbr, bbi = kr * b_re - ki * b_im, kr * b_im + ki * b_re
    j = jnp.arange(t_len + 1, dtype=F32)[:, None, None]
    pmag = jnp.exp(j * (lam_re * dt))
    pr, pi = pmag * jnp.cos(j * (lam_im * dt)), pmag * jnp.sin(j * (lam_im * dt))
    mr = pr[:t_len, ..., None] * bbr - pi[:t_len, ..., None] * bbi
    mi = pr[:t_len, ..., None] * bbi + pi[:t_len, ..., None] * bbr
    kern = (jnp.einsum('gdp,jgpc->jgdc', c_re, mr, precision=hi)
            - jnp.einsum('gdp,jgpc->jgdc', c_im, mi, precision=hi))
    s_i = jnp.arange(t_len)[:, None]
    t_i = jnp.arange(t_len)[None, :]
    lag = t_i - s_i
    kg = jnp.where((lag >= 0)[:, :, None, None, None], kern[jnp.clip(lag, 0)], 0.0)
    tz = kg.transpose(2, 0, 4, 1, 3).reshape(g_n, t_len * SSM_GROUP, t_len * SSM_GROUP)
    pad = ((0, 0), (0, 0), (0, LANES - p_n))
    vr = jnp.pad(mr[::-1].transpose(1, 0, 3, 2).reshape(g_n, t_len * SSM_GROUP, p_n), pad)
    vi = jnp.pad(mi[::-1].transpose(1, 0, 3, 2).reshape(g_n, t_len * SSM_GROUP, p_n), pad)
    tzv = jnp.concatenate([tz, vr, vi], axis=-1).astype(BF16)
    pr1, pi1 = pr[1:].transpose(1, 2, 0), pi[1:].transpose(1, 2, 0)
    crt, cit = c_re.transpose(0, 2, 1), c_im.transpose(0, 2, 1)
    wre = crt[:, :, None, :] * pr1[..., None] - cit[:, :, None, :] * pi1[..., None]
    wim = -(crt[:, :, None, :] * pi1[..., None] + cit[:, :, None, :] * pr1[..., None])
    rpad = ((0, 0), (0, LANES - p_n), (0, 0))
    wre = jnp.pad(wre.reshape(g_n, p_n, -1), rpad).astype(BF16)
    wim = jnp.pad(wim.reshape(g_n, p_n, -1), rpad).astype(BF16)
    atr = jnp.pad(pr[t_len], ((0, 0), (0, LANES - p_n)))[:, None, :]
    ati = jnp.pad(pi[t_len], ((0, 0), (0, LANES - p_n)))[:, None, :]
    drow = jnp.tile(d.reshape(g_n, 1, SSM_GROUP), (1, t_len, 1)).reshape(g_n, 1, t_len * SSM_GROUP)
    return tzv, wre, wim, atr, ati, drow


def _ssm_kernel(nk, nb, u_ref, tzv_ref, wre_ref, wim_ref, atr_ref, ati_ref, d_ref, ire_ref, iim_ref,
                y_ref, fre_ref, fim_ref, yi_ref, sr_ref, si_ref, xr_ref, xi_ref):
    tc = u_ref.shape[2]
    u = u_ref[0]
    full = _dot(u, tzv_ref[0])
    yi_ref[...] = full[:, :tc]
    sr_ref[...] = full[:, tc:tc + LANES]
    si_ref[...] = full[:, tc + LANES:]
    atr, ati = atr_ref[0], ati_ref[0]

    def step(k, carry):
        xr, xi = carry
        rows = pl.ds(pl.multiple_of(k * nb, nb), nb)
        xr_ref[rows, :] = xr
        xi_ref[rows, :] = xi
        return (atr * xr - ati * xi + sr_ref[rows, :], atr * xi + ati * xr + si_ref[rows, :])

    xr, xi = lax.fori_loop(0, nk, step, (ire_ref[0], iim_ref[0]))
    fre_ref[0] = xr
    fim_ref[0] = xi
    y = (yi_ref[...] + _dot(xr_ref[...].astype(BF16), wre_ref[0]) + _dot(xi_ref[...].astype(BF16), wim_ref[0])
         + d_ref[0] * u.astype(F32))
    y_ref[0] = y.astype(BF16)


def _ssm(u, mats, init_re, init_im, nk, nb):
    tzv, wre, wim, atr, ati, drow = mats
    g_n, r, tc = u.shape
    per_g = lambda a: pl.BlockSpec((1,) + a.shape[1:], lambda g: (g, 0, 0))
    ins = [u, tzv, wre, wim, atr, ati, drow, init_re, init_im]
    st = jax.ShapeDtypeStruct((g_n, nb, LANES), F32)
    return pl.pallas_call(
        functools.partial(_ssm_kernel, nk, nb),
        grid=(g_n,),
        in_specs=[per_g(a) for a in ins],
        out_specs=[per_g(u), per_g(init_re), per_g(init_re)],
        out_shape=[jax.ShapeDtypeStruct(u.shape, BF16), st, st],
        scratch_shapes=[pltpu.VMEM((r, tc), F32)] + [pltpu.VMEM((r, LANES), F32)] * 4,
        compiler_params=_params(("parallel",)),
        name="ssm",
    )(*ins)


def _ssm_branch(zu, mats, st_re, st_im):
    b, l, w = zu.shape
    g_n, p_n = st_re.shape[1], st_re.shape[2]
    nk = l // SSM_T
    ug = zu.reshape(b, nk, SSM_T, g_n, SSM_GROUP).transpose(3, 1, 0, 2, 4).reshape(g_n, nk * b, SSM_T * SSM_GROUP)
    pad = ((0, 0), (0, 0), (0, LANES - p_n))
    ire = jnp.pad(st_re.transpose(1, 0, 2), pad)
    iim = jnp.pad(st_im.transpose(1, 0, 2), pad)
    y, fre, fim = _ssm(ug, mats, ire, iim, nk, b)
    y = y.reshape(g_n, nk, b, SSM_T, SSM_GROUP).transpose(2, 1, 3, 0, 4).reshape(b, l, w)
    return y, fre[:, :, :p_n].transpose(1, 0, 2), fim[:, :, :p_n].transpose(1, 0, 2)


_DSA_TQ = 256
_DSA_VALUE_STEPS = 24
_DSA_SK = 1024
_DSA_MIN_SUM = 2.0 ** -80


def _dsa_kernel(tq, sk, pos0, s_valid, n_sel, qlat_ref, qidx_ref, w_ref, ka_ref, c_ref, cmax_ref, wuv_ref,
                o_ref, key_ref, bias_ref, j_ref):
    s_pad = key_ref.shape[0]
    q0 = pos0 + pl.program_id(1) * tq
    qpos = q0 + lax.broadcasted_iota(I32, (1, tq), 1)
    vis = jnp.minimum((qpos // CHUNK + 1) * CHUNK, s_valid)
    vis_max = jnp.minimum(((q0 + tq - 1) // CHUNK + 1) * CHUNK, s_valid)
    nch = (vis_max + sk - 1) // sk
    kpos = lax.broadcasted_iota(I32, (sk, tq), 0)
    kslice = lambda j: pl.ds(pl.multiple_of(j * sk, sk), sk)
    fold = lambda a: a.reshape(sk // 8, 8, tq)

    qs = jnp.concatenate([qidx_ref[0, :, hd * LANES:(hd + 1) * LANES] for hd in range(IDX_HEADS)], axis=0)

    def score_chunk(j, carry):
        kmin, kmax = carry
        rel = jnp.maximum(_dot_nt(ka_ref[0, kslice(j), :], qs), 0.0)
        acc = jnp.zeros((sk, tq), F32)
        for hd in range(IDX_HEADS):
            acc = acc + w_ref[0, hd:hd + 1, :] * rel[:, hd * tq:(hd + 1) * tq]
        bits = pltpu.bitcast(acc, I32)
        key = bits ^ ((bits >> 31) & 0x7FFFFFFF)
        visible = j * sk + kpos < vis
        key_ref[kslice(j), :] = jnp.where(visible, key, INT_MIN)
        kmin = jnp.minimum(kmin, jnp.min(fold(jnp.where(visible, key, INT_MAX)), axis=0))
        kmax = jnp.maximum(kmax, jnp.max(fold(jnp.where(visible, key, INT_MIN)), axis=0))
        return kmin, kmax

    kmin, kmax = lax.fori_loop(0, nch, score_chunk,
                               (jnp.full((8, tq), INT_MAX, I32), jnp.full((8, tq), INT_MIN, I32)))
    kmin = jnp.min(kmin, axis=0, keepdims=True)
    kmax = jnp.max(kmax, axis=0, keepdims=True)

    def count(pred):
        def body(j, c):
            hit = jnp.where(pred(key_ref[kslice(j), :], j * sk + kpos), 1.0, 0.0)
            return c + jnp.sum(fold(hit), axis=0)
        return jnp.sum(lax.fori_loop(0, nch, body, jnp.zeros((8, tq), F32)), axis=0, keepdims=True)

    key_of = lambda v: (lambda b: b ^ ((b >> 31) & 0x7FFFFFFF))(pltpu.bitcast(v, I32))
    val_of = lambda k: pltpu.bitcast(k ^ ((k >> 31) & 0x7FFFFFFF), F32)

    def settled(lo, hi, cnt):
        return (cnt <= n_sel) | (hi - 1 <= lo)

    def halve(state):
        it, lo, hi, cnt, _ = state
        mid_v = key_of(0.5 * val_of(lo) + 0.5 * val_of(hi))
        mid_k = (lo >> 1) + (hi >> 1) + (lo & hi & 1)
        mid = jnp.where((mid_v > lo) & (mid_v < hi) & (it < _DSA_VALUE_STEPS), mid_v, mid_k)
        c = count(lambda k, col: k >= mid)
        live = jnp.logical_not(settled(lo, hi, cnt))
        up = live & (c >= n_sel)
        lo, cnt, hi = jnp.where(up, mid, lo), jnp.where(up, c, cnt), jnp.where(live & (c < n_sel), mid, hi)
        return it + 1, lo, hi, cnt, jnp.max(jnp.where(settled(lo, hi, cnt), 0, 1))

    c_pos, c_nn = count(lambda k, col: k >= 1), count(lambda k, col: k >= 0)
    pos, zero = c_pos >= n_sel, c_nn >= n_sel
    lo0 = jnp.where(pos, 1, jnp.where(zero, 0, kmin))
    hi0 = jnp.where(pos, jnp.where(kmax == INT_MAX, INT_MAX, kmax + 1), jnp.where(zero, 1, 0))
    cnt0 = jnp.where(pos, c_pos, jnp.where(zero, c_nn, vis.astype(F32)))
    state = (jnp.int32(0), lo0, hi0, cnt0, jnp.max(jnp.where(settled(lo0, hi0, cnt0), 0, 1)))
    _, thr, _, cnt, _ = lax.while_loop(lambda st: (st[0] < _DSA_VALUE_STEPS + 34) & (st[4] > 0), halve, state)
    thr = jnp.maximum(thr, INT_MIN + 1)
    excess = cnt > n_sel

    j_ref[...] = jnp.full(j_ref.shape, s_pad, I32)

    @pl.when(jnp.max(jnp.where(excess, 1.0, 0.0)) > 0.0)
    def _():
        need = n_sel - count(lambda k, col: k > thr)

        def idx_step(_, lohi):
            lo, hi = lohi
            mid = (lo + hi) >> 1
            ok = count(lambda k, col: (k == thr) & (col < mid)) >= need
            return jnp.where(ok, lo, mid), jnp.where(ok, mid, hi)

        steps = int(math.ceil(math.log2(s_pad))) + 1
        _, hi = lax.fori_loop(0, steps, idx_step, (jnp.zeros((1, tq), I32), jnp.full((1, tq), s_pad, I32)))
        j_ref[...] = jnp.broadcast_to(jnp.where(excess, hi, s_pad), j_ref.shape)

    jlim = j_ref[0:1, :]

    rows = DSA_HEADS * tq
    q = jnp.concatenate([qlat_ref[0, :, hd * DSA_LATENT:(hd + 1) * DSA_LATENT] for hd in range(DSA_HEADS)], axis=0)

    wb = min(sk, 2 * LANES)

    kpos_wb = lax.broadcasted_iota(I32, (wb, tq), 0)

    def logits(j, i, first):
        blk = pl.ds(pl.multiple_of(j * sk + i * wb, wb), wb)
        if first:
            k = key_ref[blk, :]
            sel = (k > thr) | ((k == thr) & (j * sk + i * wb + kpos_wb < jlim))
            bias = jnp.where(sel, 0.0, NEG_BIG).T
            bias_ref[:, blk] = bias
        else:
            bias = bias_ref[:, blk]
        s = _dot_nt(q, c_ref[0, blk, :])
        return (s.reshape(DSA_HEADS, tq, wb) + bias[None]).reshape(rows, wb)

    def attend(m, first):
        def acc_chunk(j, carry):
            l, acc = carry
            ps = []
            for i in range(sk // wb):
                s = logits(j, i, first)
                for t in range(wb // LANES):
                    p = jnp.exp2(s[:, t * LANES:(t + 1) * LANES] - m)
                    l = l + p
                    ps.append(p.astype(BF16))
            return l, acc + _dot(jnp.concatenate(ps, axis=1), c_ref[0, kslice(j), :])

        l, acc = lax.fori_loop(0, nch, acc_chunk, (jnp.zeros((rows, LANES), F32), jnp.zeros((rows, DSA_LATENT), F32)))
        return jnp.sum(l, axis=1, keepdims=True), acc

    def emit(l, acc):
        o = (acc / l).astype(BF16)
        for hd in range(DSA_HEADS):
            o_ref[0, :, hd * DSA_HEAD_DIM:(hd + 1) * DSA_HEAD_DIM] = _dot(o[hd * tq:(hd + 1) * tq], wuv_ref[hd]).astype(BF16)

    qf = q.astype(F32)
    bound = jnp.sqrt(jnp.sum(qf * qf, axis=1, keepdims=True)) * cmax_ref[0, 0:1, 0:1]
    l, acc = attend(jnp.broadcast_to(bound, (rows, LANES)), True)
    healthy = jnp.min(l) > _DSA_MIN_SUM

    @pl.when(healthy)
    def _():
        emit(l, acc)

    @pl.when(jnp.logical_not(healthy))
    def _():
        def max_chunk(j, mx):
            for i in range(sk // wb):
                s = logits(j, i, False)
                for t in range(wb // LANES):
                    mx = jnp.maximum(mx, s[:, t * LANES:(t + 1) * LANES])
            return mx

        mx = lax.fori_loop(0, nch, max_chunk, jnp.full((rows, LANES), NEG_BIG, F32))
        emit(*attend(jnp.broadcast_to(jnp.max(mx, axis=1, keepdims=True), (rows, LANES)), False))


def _dsa(qlat, qidx, w, kidx, c, wuv, pos0, s_valid, n_sel, tq):
    b, l_true, _ = qlat.shape
    l = -(-l_true // tq) * tq
    qlat, qidx, w = (jnp.pad(a, ((0, 0), (0, l - l_true), (0, 0))) for a in (qlat, qidx, w))
    wt = w.transpose(0, 2, 1)
    s_pad = c.shape[1]
    sk = min(_DSA_SK, s_pad)
    ka = jnp.pad(kidx, ((0, 0), (0, 0), (0, LANES - IDX_DIM)))
    cf = c.astype(F32)
    cmax = jnp.broadcast_to(jnp.sqrt(jnp.max(jnp.sum(cf * cf, axis=2), axis=1))[:, None, None], (b, 1, LANES))
    qspec = lambda a: pl.BlockSpec((1, tq, a.shape[2]), lambda i, j: (i, j, 0))
    kspec = lambda a: pl.BlockSpec((1,) + a.shape[1:], lambda i, j: (i, 0, 0))
    dh = wuv.shape[0] * wuv.shape[2]
    return pl.pallas_call(
        functools.partial(_dsa_kernel, tq, sk, pos0, s_valid, n_sel),
        grid=(b, l // tq),
        in_specs=[qspec(qlat), qspec(qidx), pl.BlockSpec((1, wt.shape[1], tq), lambda i, j: (i, 0, j)), kspec(ka), kspec(c),
                  kspec(cmax), _const_spec(wuv.shape)],
        out_specs=pl.BlockSpec((1, tq, dh), lambda i, j: (i, j, 0)),
        out_shape=jax.ShapeDtypeStruct((b, l, dh), BF16),
        scratch_shapes=[pltpu.VMEM((s_pad, tq), I32), pltpu.VMEM((tq, s_pad), F32), pltpu.VMEM((8, tq), I32)],
        compiler_params=_params(("parallel", "arbitrary")),
        name="dsa",
    )(qlat, qidx, wt, ka, c, cmax, wuv)[:, :l_true]


_PEER_WIDTH = [PEER_TOPK // (r + 1) for r in range(PEER_TOPK)]


def _top_rows(s, k):
    row = lax.broadcasted_iota(I32, s.shape, 0)
    vals, idxs = [], []
    for _ in range(k):
        m = jnp.max(s, axis=0, keepdims=True)
        first = jnp.min(jnp.where(s == m, row, s.shape[0]), axis=0, keepdims=True)
        s = jnp.where(row == first, -jnp.inf, s)
        vals.append(m)
        idxs.append(first)
    return jnp.concatenate(vals, axis=0), jnp.concatenate(idxs, axis=0)


def _peer_route_kernel(h_ref, g_ref, wq_ref, k1_ref, k2_ref, xn_ref, i1_ref, i2_ref, gate_ref):
    tn = h_ref.shape[0]
    xn = _rms(h_ref[...], g_ref[...]).astype(BF16)
    xn_ref[...] = xn
    q = _dot(xn, wq_ref[...]).astype(BF16)
    oute, outg = [], []
    for hd in range(PEER_HEADS):
        qa = q[:, (2 * hd) * PEER_HALF:(2 * hd + 1) * PEER_HALF]
        qb = q[:, (2 * hd + 1) * PEER_HALF:(2 * hd + 2) * PEER_HALF]
        v1, i1 = _top_rows(_dot_nt(k1_ref[hd], qa), PEER_TOPK)
        v2, i2 = _top_rows(_dot_nt(k2_ref[hd], qb), PEER_TOPK)
        cand = jnp.concatenate([v1[r:r + 1] + v2[0:w] for r, w in enumerate(_PEER_WIDTH)], axis=0)
        cexp = jnp.concatenate([i1[r:r + 1] * PEER_NKEYS + i2[0:w] for r, w in enumerate(_PEER_WIDTH)], axis=0)
        row = lax.broadcasted_iota(I32, cand.shape, 0)
        tops, experts = [], []
        for _ in range(PEER_TOPK):
            m = jnp.max(cand, axis=0, keepdims=True)
            first = jnp.min(jnp.where(cand == m, row, cand.shape[0]), axis=0, keepdims=True)
            hit = row == first
            experts.append(jnp.sum(jnp.where(hit, cexp, 0), axis=0, keepdims=True))
            cand = jnp.where(hit, -jnp.inf, cand)
            tops.append(m)
        top = jnp.concatenate(tops, axis=0)
        ex = jnp.exp(top - top[0:1])
        outg.append(ex / jnp.sum(ex, axis=0, keepdims=True))
        oute.append(jnp.concatenate(experts, axis=0))
    expert = jnp.concatenate(oute, axis=0)
    as_rows = lambda a: pltpu.bitcast(pltpu.bitcast(a, F32).T, I32)
    i1_ref[...] = as_rows(expert >> 7)
    i2_ref[...] = as_rows(expert & (PEER_NKEYS - 1))
    gate_ref[...] = jnp.concatenate(outg, axis=0).T


def _peer_route(h, g, wq, k1, k2):
    n, d = h.shape
    tn = LANES
    slots = PEER_HEADS * PEER_TOPK
    row = lambda w: pl.BlockSpec((tn, w), lambda i: (i, 0))
    return pl.pallas_call(
        _peer_route_kernel,
        grid=(n // tn,),
        in_specs=[row(d), _const_spec(g.shape), _const_spec(wq.shape), _const_spec(k1.shape), _const_spec(k2.shape)],
        out_specs=[row(d), row(slots), row(slots), row(slots)],
        out_shape=[jax.ShapeDtypeStruct((n, d), BF16), jax.ShapeDtypeStruct((n, slots), I32),
                   jax.ShapeDtypeStruct((n, slots), I32), jax.ShapeDtypeStruct((n, slots), F32)],
        compiler_params=_params(("parallel",)),
        name="peer_route",
    )(h, g, wq, k1, k2)


_PLANE_PAD = 4


def _peer_gates_kernel(i1_ref, i2_ref, gate_ref, a_ref, planes_ref):
    tn, slots = i1_ref.shape
    plane = tn + _PLANE_PAD
    sub = lax.broadcasted_iota(I32, (PEER_NKEYS, slots), 0)

    def token(n, carry):
        r = pl.ds(n, 1)
        pt = jnp.where(sub == i1_ref[r, :], gate_ref[r, :], 0.0).astype(BF16)
        qt = jnp.where(sub == i2_ref[r, :], 1.0, 0.0).astype(BF16)
        planes_ref[pl.ds(n, PEER_NKEYS, stride=plane), :] = _dot_nt(pt, qt)
        return carry

    lax.fori_loop(0, tn, token, 0, unroll=32)
    for k in range(PEER_NKEYS):
        a_ref[:, k * PEER_NKEYS:(k + 1) * PEER_NKEYS] = planes_ref[pl.ds(k * plane, tn), :].astype(BF16)


def _peer_gates(i1, i2, gate):
    n, slots = i1.shape
    tn = _pick_tile(n, 128)
    ne = PEER_NKEYS * PEER_NKEYS
    row = lambda w: pl.BlockSpec((tn, w), lambda i: (i, 0))
    return pl.pallas_call(
        _peer_gates_kernel,
        grid=(n // tn,),
        in_specs=[row(slots)] * 3,
        out_specs=row(ne),
        out_shape=jax.ShapeDtypeStruct((n, ne), BF16),
        scratch_shapes=[pltpu.VMEM((PEER_NKEYS * (tn + _PLANE_PAD), PEER_NKEYS), F32)],
        compiler_params=_params(("parallel",)),
        name="peer_gates",
    )(i1, i2, gate)


_PEER_SUB = 256
def _peer_apply_kernel(final_norm, xn_ref, a_ref, u_ref, v_ref, h_ref, g_ref, o_ref, acc_ref):
    j = pl.program_id(1)

    @pl.when(j == 0)
    def _():
        acc_ref[...] = jnp.zeros_like(acc_ref)

    xn = xn_ref[...]
    ws = []
    for s in range(u_ref.shape[0] // _PEER_SUB):
        sub = slice(s * _PEER_SUB, (s + 1) * _PEER_SUB)
        act = _gelu(_dot_nt(xn, u_ref[sub, :]))
        ws.append((act * a_ref[:, sub].astype(F32)).astype(BF16))
    acc_ref[...] += _dot(jnp.concatenate(ws, axis=1), v_ref[...])

    @pl.when(j == pl.num_programs(1) - 1)
    def _():
        y = h_ref[...] + acc_ref[...]
        o_ref[...] = _rms(y, g_ref[...]) if final_norm else y


def _peer_apply(xn, a, u, v, h, g, final_norm):
    n, d = xn.shape
    ne = u.shape[0]
    tn = _pick_tile(n, 512)
    te = 8 * _PEER_SUB
    return pl.pallas_call(
        functools.partial(_peer_apply_kernel, final_norm),
        grid=(n // tn, ne // te),
        in_specs=[pl.BlockSpec((tn, d), lambda i, j: (i, 0)), pl.BlockSpec((tn, te), lambda i, j: (i, j)),
                  pl.BlockSpec((te, d), lambda i, j: (j, 0)), pl.BlockSpec((te, d), lambda i, j: (j, 0)),
                  pl.BlockSpec((tn, d), lambda i, j: (i, 0)), _const_spec(g.shape)],
        out_specs=pl.BlockSpec((tn, d), lambda i, j: (i, 0)),
        out_shape=jax.ShapeDtypeStruct((n, d), F32),
        scratch_shapes=[pltpu.VMEM((tn, d), F32)],
        compiler_params=_params(("parallel", "arbitrary")),
        name="peer_apply",
    )(xn, a, u, v, h, g)


def _split_w_in(w_in, d):
    widths = (d, DSA_HEADS * DSA_LATENT, DSA_LATENT, IDX_HEADS * IDX_DIM, IDX_DIM, IDX_HEADS,
              MEM_HEADS * MEM_HEAD_DIM, N_BRANCH * d)
    offs = np.cumsum((0,) + widths)
    wu, wq, wc, wqi, wki, wwi, wmq, wg = [w_in[:, offs[i]:offs[i + 1]] for i in range(8)]
    wkw = jnp.pad(jnp.concatenate([wki, wwi], axis=1), ((0, 0), (0, LANES - IDX_DIM - IDX_HEADS)))
    kw_scale = jnp.concatenate([jnp.ones((IDX_DIM,), F32), jnp.full((IDX_HEADS,), IDX_HEADS ** -0.5, F32),
                                jnp.zeros((LANES - IDX_DIM - IDX_HEADS,), F32)])[None, :]
    wqi = jnp.pad(wqi.reshape(-1, IDX_HEADS, IDX_DIM), ((0, 0), (0, 0), (0, LANES - IDX_DIM))).reshape(-1, IDX_HEADS * LANES)
    ws = [wu, wq * (DSA_LATENT ** -0.5 * LOG2E), wc, wqi, wkw, wmq * MEM_HEAD_DIM ** -0.5, wg]
    return [w.astype(BF16) for w in ws], kw_scale


def _pad_rows(a, n):
    return a if a.shape[0] == n else jnp.pad(a, ((0, n - a.shape[0]),) + ((0, 0),) * (a.ndim - 1))


def _pad_keys(a, s_pad):
    return jnp.pad(a, ((0, 0), (0, s_pad - a.shape[1]), (0, 0)))


def _token_mix(h, seq_shape, lw, mats, state, cache, mem_kv, pos0):
    b, l = seq_shape
    bf = lambda a: a.astype(BF16)
    seq = lambda a: a.reshape(b, l, -1)
    u, q, c, qi, kw, mq, g = _inproj(h, lw["g_norm1"], lw["g_kv"], lw["kw_scale"], lw["w_in"])
    ki, wi = kw[:, :IDX_DIM], kw[:, IDX_DIM:IDX_DIM + IDX_HEADS]

    ys, s_re, s_im = _ssm_branch(seq(u), mats, state[0], state[1])

    c_all, k_all = bf(seq(c)), bf(seq(ki))
    if cache is not None:
        c_all = jnp.concatenate([bf(cache[0]), c_all], axis=1)
        k_all = jnp.concatenate([bf(cache[1]), k_all], axis=1)
    s_all = c_all.shape[1]
    s_pad = -(-s_all // _DSA_SK) * _DSA_SK if cache is not None else s_all
    yd = _dsa(seq(q), seq(qi), seq(wi), _pad_keys(k_all, s_pad), _pad_keys(c_all, s_pad), lw["w_uv"], pos0, s_all,
              min(DSA_TOPK, s_all // 4), _pick_tile(max(l, LANES), _DSA_TQ))

    ym = _memattn(seq(mq), mem_kv[0], mem_kv[1])

    flat = lambda a: a.reshape(b * l, -1)
    h2 = _merge(h, flat(ys), flat(yd), flat(ym), g, lw["w_glu"], lw["b_glu"], lw["w_br_ssm"], lw["w_br_dsa"],
                lw["w_br_mem"], lw["w_out"])
    return h2, seq(c), seq(ki), s_re, s_im


def _channel_mix(h2, lw, g_final, final_norm):
    n = h2.shape[0]
    h2 = _pad_rows(h2, -(-n // LANES) * LANES)
    xn, i1, i2, gate = _peer_route(h2, lw["g_norm2"], lw["w_peer_q"], lw["peer_sub_k1"], lw["peer_sub_k2"])
    a = _peer_gates(i1, i2, gate)
    return _peer_apply(xn, a, lw["peer_u"], lw["peer_v"], h2, g_final, final_norm)[:n]


def kernel(x_prompt, x_sample, mem_prompt, cache_dsa_latent, cache_dsa_idx_k, state_ssm_re, state_ssm_im, cache_mem_k, cache_mem_v, g_norm1, w_in, g_kv, w_uv, ssm_lam_re, ssm_lam_im, ssm_log_dt, ssm_b_re, ssm_b_im, ssm_c_re, ssm_c_im, ssm_d, w_glu, b_glu, g_mem, w_mem_kv, w_br_ssm, w_br_dsa, w_br_mem, w_out, g_norm2, w_peer_q, peer_sub_k1, peer_sub_k2, peer_u, peer_v, g_final):
    depth = w_in.shape[0]
    bp, lp, d = x_prompt.shape
    bs, ls, _ = x_sample.shape
    past = cache_dsa_latent.shape[2]
    mem_w = MEM_HEADS * MEM_HEAD_DIM
    n_mem = mem_prompt.shape[1]
    heads = (n_mem, MEM_HEADS, MEM_HEAD_DIM)
    bf = lambda a: a.astype(BF16)
    row = lambda a: a[None, :]

    hp, hs = x_prompt.reshape(bp * lp, d), x_sample.reshape(bs * ls, d)
    outs = [[] for _ in range(10)]
    for l in range(depth):
        ws, kw_scale = _split_w_in(w_in[l], d)
        lw = dict(g_norm1=row(g_norm1[l]), g_kv=row(g_kv[l]), kw_scale=kw_scale, w_in=ws, w_uv=bf(w_uv[l]),
                  w_glu=bf(w_glu[l]), b_glu=row(b_glu[l]), w_br_ssm=bf(w_br_ssm[l]), w_br_dsa=bf(w_br_dsa[l]),
                  w_br_mem=bf(w_br_mem[l]), w_out=bf(w_out[l]), g_norm2=row(g_norm2[l]), w_peer_q=bf(w_peer_q[l]),
                  peer_sub_k1=bf(peer_sub_k1[l]), peer_sub_k2=bf(peer_sub_k2[l]), peer_u=bf(peer_u[l]),
                  peer_v=bf(peer_v[l]))
        mats = _ssm_matrices(ssm_lam_re[l], ssm_lam_im[l], ssm_log_dt[l], ssm_b_re[l], ssm_b_im[l],
                             ssm_c_re[l], ssm_c_im[l], ssm_d[l], SSM_T)
        kv = _memkv(mem_prompt.reshape(bp * n_mem, d), row(g_mem[l]), bf(w_mem_kv[l]))
        mk_p, mv_p = kv[:, :mem_w].reshape(bp, n_mem, mem_w), kv[:, mem_w:].reshape(bp, n_mem, mem_w)
        zeros = jnp.zeros((bp,) + state_ssm_re.shape[2:], F32)

        h2p, c_p, ki_p, sre_p, sim_p = _token_mix(hp, (bp, lp), lw, mats, (zeros, zeros), None, (bf(mk_p), bf(mv_p)), 0)
        mem_s = (bf(cache_mem_k[l].reshape(bs, n_mem, mem_w)), bf(cache_mem_v[l].reshape(bs, n_mem, mem_w)))
        h2s, c_s, ki_s, sre_s, sim_s = _token_mix(hs, (bs, ls), lw, mats, (state_ssm_re[l], state_ssm_im[l]),
                                                  (cache_dsa_latent[l], cache_dsa_idx_k[l]), mem_s, past)
        final = l == depth - 1
        hp = _channel_mix(h2p, lw, row(g_final), final)
        hs = _channel_mix(h2s, lw, row(g_final), final)
        for lst, val in zip(outs, (c_p, ki_p, sre_p, sim_p, mk_p.reshape((bp,) + heads), mv_p.reshape((bp,) + heads),
                                   c_s, ki_s, sre_s, sim_s)):
            lst.append(val)
    return (hp.reshape(bp, lp, d), hs.reshape(bs, ls, d)) + tuple(jnp.stack(o) for o in outs)
```

```python
import functools
import math

import jax
import jax.numpy as jnp
import numpy as np
from jax import lax
from jax.experimental import pallas as pl
from jax.experimental.pallas import tpu as pltpu

F32 = jnp.float32
BF16 = jnp.bfloat16
I32 = jnp.int32

EPS = 1e-6
CHUNK = 64
SSM_GROUP = 16
SSM_STATE = 64
SSM_T = 32
DSA_HEADS = 8
DSA_LATENT = 256
DSA_HEAD_DIM = 128
IDX_HEADS = 8
IDX_DIM = 64
DSA_TOPK = 256
MEM_HEADS = 4
MEM_HEAD_DIM = 256
PEER_HEADS = 8
PEER_NKEYS = 128
PEER_HALF = 128
PEER_TOPK = 16
N_BRANCH = 3

LANES = 128
VMEM_LIMIT = 56 * 1024 * 1024
INT_MIN = -2 ** 31
INT_MAX = 2 ** 31 - 1
NEG_BIG = -1e30
LOG2E = 1.4426950408889634


def _pick_tile(n, target):
    if n <= target:
        return n
    for t in range(target, 7, -1):
        if n % t == 0 and t % 8 == 0:
            return t
    return n


def _params(sem):
    return pltpu.CompilerParams(dimension_semantics=sem, vmem_limit_bytes=VMEM_LIMIT)


def _const_spec(shape):
    nd = len(shape)
    return pl.BlockSpec(shape, lambda *_: (0,) * nd, pipeline_mode=pl.Buffered(1))


def _rms(x, g):
    return x * lax.rsqrt(jnp.mean(x * x, axis=-1, keepdims=True) + EPS) * g


def _gelu(x):
    return 0.5 * x * (1.0 + jnp.tanh(0.7978845608028654 * (x + 0.044715 * x * x * x)))


def _sigmoid(x):
    return 1.0 / (1.0 + jnp.exp(-x))


def _dot_nt(a, b):
    return lax.dot_general(a, b, (((1,), (1,)), ((), ())), preferred_element_type=F32)


def _dot(a, b):
    return jnp.dot(a, b, preferred_element_type=F32)


def _inproj_kernel(x_ref, g1_ref, gkv_ref, kws_ref, wu_ref, wq_ref, wc_ref, wqi_ref, wkw_ref, wmq_ref, wg_ref,
                   u_ref, q_ref, c_ref, qi_ref, kw_ref, mq_ref, g_ref):
    xn = _rms(x_ref[...], g1_ref[...]).astype(BF16)
    u_ref[...] = _dot(xn, wu_ref[...]).astype(BF16)
    q_ref[...] = _dot(xn, wq_ref[...]).astype(BF16)
    c_ref[...] = _rms(_dot(xn, wc_ref[...]), gkv_ref[...])
    qi_ref[...] = _dot(xn, wqi_ref[...]).astype(BF16)
    kw_ref[...] = _dot(xn, wkw_ref[...]) * kws_ref[...]
    mq_ref[...] = _dot(xn, wmq_ref[...]).astype(BF16)
    g_ref[...] = _sigmoid(_dot(xn, wg_ref[...])).astype(BF16)


def _inproj(h, g1, gkv, kw_scale, ws):
    n, d = h.shape
    tm = _pick_tile(n, 256)
    widths = [w.shape[1] for w in ws]
    dts = [BF16, BF16, F32, BF16, F32, BF16, BF16]
    row = lambda w: pl.BlockSpec((tm, w), lambda i: (i, 0))
    return pl.pallas_call(
        _inproj_kernel,
        grid=(n // tm,),
        in_specs=[row(d), _const_spec(g1.shape), _const_spec(gkv.shape), _const_spec(kw_scale.shape)]
        + [_const_spec(w.shape) for w in ws],
        out_specs=[row(w) for w in widths],
        out_shape=[jax.ShapeDtypeStruct((n, w), dt) for w, dt in zip(widths, dts)],
        compiler_params=_params(("parallel",)),
        name="inproj",
    )(h, g1, gkv, kw_scale, *ws)


def _memkv_kernel(x_ref, g_ref, w_ref, o_ref):
    xn = _rms(x_ref[...], g_ref[...]).astype(BF16)
    o_ref[...] = _dot(xn, w_ref[...])


def _memkv(mem, g, w):
    n, d = mem.shape
    tm = _pick_tile(n, 256)
    return pl.pallas_call(
        _memkv_kernel,
        grid=(n // tm,),
        in_specs=[pl.BlockSpec((tm, d), lambda i: (i, 0)), _const_spec(g.shape), _const_spec(w.shape)],
        out_specs=pl.BlockSpec((tm, w.shape[1]), lambda i: (i, 0)),
        out_shape=jax.ShapeDtypeStruct((n, w.shape[1]), F32),
        compiler_params=_params(("parallel",)),
        name="memkv",
    )(mem, g, w)


def _memattn_kernel(q_ref, k_ref, v_ref, o_ref):
    for hd in range(MEM_HEADS):
        sl = slice(hd * MEM_HEAD_DIM, (hd + 1) * MEM_HEAD_DIM)
        logits = _dot_nt(q_ref[0, :, sl], k_ref[0, :, sl])
        m = jnp.max(logits, axis=-1, keepdims=True)
        p = jnp.exp(logits - m)
        l = jnp.sum(p, axis=-1, keepdims=True)
        o = _dot(p.astype(BF16), v_ref[0, :, sl]) / l
        o_ref[0, :, sl] = o.astype(BF16)


def _memattn(q, k, v):
    b, l, w = q.shape
    tl = _pick_tile(l, 512)
    nm = k.shape[1]
    return pl.pallas_call(
        _memattn_kernel,
        grid=(b, l // tl),
        in_specs=[pl.BlockSpec((1, tl, w), lambda i, j: (i, j, 0)),
                  pl.BlockSpec((1, nm, w), lambda i, j: (i, 0, 0)),
                  pl.BlockSpec((1, nm, w), lambda i, j: (i, 0, 0))],
        out_specs=pl.BlockSpec((1, tl, w), lambda i, j: (i, j, 0)),
        out_shape=jax.ShapeDtypeStruct((b, l, w), BF16),
        compiler_params=_params(("parallel", "parallel")),
        name="memattn",
    )(q, k, v)


def _merge_kernel(h_ref, ys_ref, yd_ref, ym_ref, g_ref, wglu_ref, bglu_ref, wbs_ref, wbd_ref, wbm_ref, wout_ref,
                  o_ref):
    d = h_ref.shape[1]
    z = _gelu(ys_ref[...].astype(F32))
    gate = _sigmoid(_dot(z.astype(BF16), wglu_ref[...]) + bglu_ref[...])
    a = _dot((z * gate).astype(BF16), wbs_ref[...])
    b = _dot(yd_ref[...], wbd_ref[...])
    c = _dot(ym_ref[...], wbm_ref[...])
    g = g_ref[...].astype(F32)
    merged = g[:, 0:d] * a + g[:, d:2 * d] * b + g[:, 2 * d:3 * d] * c
    o_ref[...] = h_ref[...] + _dot(merged.astype(BF16), wout_ref[...])


def _merge(h, ys, yd, ym, g, wglu, bglu, wbs, wbd, wbm, wout):
    n, d = h.shape
    tm = _pick_tile(n, 512)
    row = lambda w: pl.BlockSpec((tm, w), lambda i: (i, 0))
    consts = [wglu, bglu, wbs, wbd, wbm, wout]
    return pl.pallas_call(
        _merge_kernel,
        grid=(n // tm,),
        in_specs=[row(d), row(d), row(d), row(d), row(3 * d)] + [_const_spec(c.shape) for c in consts],
        out_specs=row(d),
        out_shape=jax.ShapeDtypeStruct((n, d), F32),
        compiler_params=_params(("parallel",)),
        name="merge",
    )(h, ys, yd, ym, g, *consts)


def _ssm_matrices(lam_re, lam_im, log_dt, b_re, b_im, c_re, c_im, d, t_len):
    hi = lax.Precision.HIGHEST
    g_n, p_n = lam_re.shape
    dt = jnp.exp(log_dt)[:, None]
    mag = jnp.exp(lam_re * dt)
    ar, ai = mag * jnp.cos(lam_im * dt), mag * jnp.sin(lam_im * dt)
    den = lam_re * lam_re + lam_im * lam_im
    nr, ni = ar - 1.0, ai
    kr = ((nr * lam_re + ni * lam_im) / den)[..., None]
    ki = ((ni * lam_re - nr * lam_im) / den)[..., None]
    bbr, bbi = kr * b_re - ki * b_im, kr * b_im + ki * b_re
    j = jnp.arange(t_len + 1, dtype=F32)[:, None, None]
    pmag = jnp.exp(j * (lam_re * dt))
    pr, pi = pmag * jnp.cos(j * (lam_im * dt)), pmag * jnp.sin(j * (lam_im * dt))
    mr = pr[:t_len, ..., None] * bbr - pi[:t_len, ..., None] * bbi
    mi = pr[:t_len, ..., None] * bbi + pi[:t_len, ..., None] * bbr
    kern = (jnp.einsum('gdp,jgpc->jgdc', c_re, mr, precision=hi)
            - jnp.einsum('gdp,jgpc->jgdc', c_im, mi, precision=hi))
    s_i = jnp.arange(t_len)[:, None]
    t_i = jnp.arange(t_len)[None, :]
    lag = t_i - s_i
    kg = jnp.where((lag >= 0)[:, :, None, None, None], kern[jnp.clip(lag, 0)], 0.0)
    tz = kg.transpose(2, 0, 4, 1, 3).reshape(g_n, t_len * SSM_GROUP, t_len * SSM_GROUP)
    pad = ((0, 0), (0, 0), (0, LANES - p_n))
    vr = jnp.pad(mr[::-1].transpose(1, 0, 3, 2).reshape(g_n, t_len * SSM_GROUP, p_n), pad)
    vi = jnp.pad(mi[::-1].transpose(1, 0, 3, 2).reshape(g_n, t_len * SSM_GROUP, p_n), pad)
    tzv = jnp.concatenate([tz, vr, vi], axis=-1).astype(BF16)
    pr1, pi1 = pr[1:].transpose(1, 2, 0), pi[1:].transpose(1, 2, 0)
    crt, cit = c_re.transpose(0, 2, 1), c_im.transpose(0, 2, 1)
    wre = crt[:, :, None, :] * pr1[..., None] - cit[:, :, None, :] * pi1[..., None]
    wim = -(crt[:, :, None, :] * pi1[..., None] + cit[:, :, None, :] * pr1[..., None])
    rpad = ((0, 0), (0, LANES - p_n), (0, 0))
    wre = jnp.pad(wre.reshape(g_n, p_n, -1), rpad).astype(BF16)
    wim = jnp.pad(wim.reshape(g_n, p_n, -1), rpad).astype(BF16)
    atr = jnp.pad(pr[t_len], ((0, 0), (0, LANES - p_n)))[:, None, :]
    ati = jnp.pad(pi[t_len], ((0, 0), (0, LANES - p_n)))[:, None, :]
    drow = jnp.tile(d.reshape(g_n, 1, SSM_GROUP), (1, t_len, 1)).reshape(g_n, 1, t_len * SSM_GROUP)
    return tzv, wre, wim, atr, ati, drow


def _ssm_kernel(nk, nb, u_ref, tzv_ref, wre_ref, wim_ref, atr_ref, ati_ref, d_ref, ire_ref, iim_ref,
                y_ref, fre_ref, fim_ref, yi_ref, sr_ref, si_ref, xr_ref, xi_ref):
    tc = u_ref.shape[2]
    u = u_ref[0]
    full = _dot(u, tzv_ref[0])
    yi_ref[...] = full[:, :tc]
    sr_ref[...] = full[:, tc:tc + LANES]
    si_ref[...] = full[:, tc + LANES:]
    atr, ati = atr_ref[0], ati_ref[0]

    def step(k, carry):
        xr, xi = carry
        rows = pl.ds(pl.multiple_of(k * nb, nb), nb)
        xr_ref[rows, :] = xr
        xi_ref[rows, :] = xi
        return (atr * xr - ati * xi + sr_ref[rows, :], atr * xi + ati * xr + si_ref[rows, :])

    xr, xi = lax.fori_loop(0, nk, step, (ire_ref[0], iim_ref[0]))
    fre_ref[0] = xr
    fim_ref[0] = xi
    y = (yi_ref[...] + _dot(xr_ref[...].astype(BF16), wre_ref[0]) + _dot(xi_ref[...].astype(BF16), wim_ref[0])
         + d_ref[0] * u.astype(F32))
    y_ref[0] = y.astype(BF16)


def _ssm(u, mats, init_re, init_im, nk, nb):
    tzv, wre, wim, atr, ati, drow = mats
    g_n, r, tc = u.shape
    per_g = lambda a: pl.BlockSpec((1,) + a.shape[1:], lambda g: (g, 0, 0))
    ins = [u, tzv, wre, wim, atr, ati, drow, init_re, init_im]
    st = jax.ShapeDtypeStruct((g_n, nb, LANES), F32)
    return pl.pallas_call(
        functools.partial(_ssm_kernel, nk, nb),
        grid=(g_n,),
        in_specs=[per_g(a) for a in ins],
        out_specs=[per_g(u), per_g(init_re), per_g(init_re)],
        out_shape=[jax.ShapeDtypeStruct(u.shape, BF16), st, st],
        scratch_shapes=[pltpu.VMEM((r, tc), F32)] + [pltpu.VMEM((r, LANES), F32)] * 4,
        compiler_params=_params(("parallel",)),
        name="ssm",
    )(*ins)


def _ssm_branch(zu, mats, st_re, st_im):
    b, l, w = zu.shape
    g_n, p_n = st_re.shape[1], st_re.shape[2]
    nk = l // SSM_T
    ug = zu.reshape(b, nk, SSM_T, g_n, SSM_GROUP).transpose(3, 1, 0, 2, 4).reshape(g_n, nk * b, SSM_T * SSM_GROUP)
    pad = ((0, 0), (0, 0), (0, LANES - p_n))
    ire = jnp.pad(st_re.transpose(1, 0, 2), pad)
    iim = jnp.pad(st_im.transpose(1, 0, 2), pad)
    y, fre, fim = _ssm(ug, mats, ire, iim, nk, b)
    y = y.reshape(g_n, nk, b, SSM_T, SSM_GROUP).transpose(2, 1, 3, 0, 4).reshape(b, l, w)
    return y, fre[:, :, :p_n].transpose(1, 0, 2), fim[:, :, :p_n].transpose(1, 0, 2)


_DSA_TQ = 256
_DSA_VALUE_STEPS = 24
_DSA_SK = 1024
_DSA_MIN_SUM = 2.0 ** -80


def _dsa_kernel(tq, tv, sk, pos0, s_valid, n_sel, qlat_ref, qidx_ref, w_ref, ka_ref, c_ref, cmax_ref, wuv_ref,
                o_ref, key_ref, bias_ref, j_ref):
    s_pad = key_ref.shape[0]
    q0 = pos0 + pl.program_id(1) * tq
    qpos = q0 + lax.broadcasted_iota(I32, (1, tq), 1)
    vis = jnp.minimum((qpos // CHUNK + 1) * CHUNK, s_valid)
    vis_max = jnp.minimum(((q0 + tq - 1) // CHUNK + 1) * CHUNK, s_valid)
    nch = (vis_max + sk - 1) // sk
    kpos = lax.broadcasted_iota(I32, (sk, tq), 0)
    kslice = lambda j: pl.ds(pl.multiple_of(j * sk, sk), sk)
    fold = lambda a: a.reshape(sk // 8, 8, tq)

    qs = jnp.concatenate([qidx_ref[0, :, hd * LANES:(hd + 1) * LANES] for hd in range(IDX_HEADS)], axis=0)

    def score_chunk(j, carry):
        kmin, kmax = carry
        rel = jnp.maximum(_dot_nt(ka_ref[0, kslice(j), :], qs), 0.0)
        acc = jnp.zeros((sk, tq), F32)
        for hd in range(IDX_HEADS):
            acc = acc + w_ref[0, hd:hd + 1, :] * rel[:, hd * tq:(hd + 1) * tq]
        bits = pltpu.bitcast(acc, I32)
        key = bits ^ ((bits >> 31) & 0x7FFFFFFF)
        visible = j * sk + kpos < vis
        key_ref[kslice(j), :] = jnp.where(visible, key, INT_MIN)
        kmin = jnp.minimum(kmin, jnp.min(fold(jnp.where(visible, key, INT_MAX)), axis=0))
        kmax = jnp.maximum(kmax, jnp.max(fold(jnp.where(visible, key, INT_MIN)), axis=0))
        return kmin, kmax

    kmin, kmax = lax.fori_loop(0, nch, score_chunk,
                               (jnp.full((8, tq), INT_MAX, I32), jnp.full((8, tq), INT_MIN, I32)))
    kmin = jnp.min(kmin, axis=0, keepdims=True)
    kmax = jnp.max(kmax, axis=0, keepdims=True)

    def count(pred):
        def body(j, c):
            hit = jnp.where(pred(key_ref[kslice(j), :], j * sk + kpos), 1.0, 0.0)
            return c + jnp.sum(fold(hit), axis=0)
        return jnp.sum(lax.fori_loop(0, nch, body, jnp.zeros((8, tq), F32)), axis=0, keepdims=True)

    key_of = lambda v: (lambda b: b ^ ((b >> 31) & 0x7FFFFFFF))(pltpu.bitcast(v, I32))
    val_of = lambda k: pltpu.bitcast(k ^ ((k >> 31) & 0x7FFFFFFF), F32)

    def settled(lo, hi, cnt):
        return (cnt <= n_sel) | (hi - 1 <= lo)

    def halve(it, lo, hi, cnt):
        mid_v =key_of(0.5 * val_of(lo) + 0.5 * val_of(hi))
        mid_k = (lo >> 1) + (hi >> 1) + (lo & hi & 1)
        mid = jnp.where((mid_v > lo) & (mid_v < hi) & (it < _DSA_VALUE_STEPS), mid_v, mid_k)
        c = count(lambda k, col: k >= mid)
        live = jnp.logical_not(settled(lo, hi, cnt))
        up = live & (c >= n_sel)
        return jnp.where(up, mid, lo), jnp.where(live & (c < n_sel), mid, hi), jnp.where(up, c, cnt)

    def halve_twice(state):
        it, lo, hi, cnt, _ = state
        lo, hi, cnt = halve(it, lo, hi, cnt)
        lo, hi, cnt = halve(it + 1, lo, hi, cnt)
        return it + 2, lo, hi, cnt, jnp.max(jnp.where(settled(lo, hi, cnt), 0, 1))

    c_pos, c_nn = count(lambda k, col: k >= 1), count(lambda k, col: k >= 0)
    pos, zero = c_pos >= n_sel, c_nn >= n_sel
    lo0 = jnp.where(pos, 1, jnp.where(zero, 0, kmin))
    hi0 = jnp.where(pos, jnp.where(kmax == INT_MAX, INT_MAX, kmax + 1), jnp.where(zero, 1, 0))
    cnt0 = jnp.where(pos, c_pos, jnp.where(zero, c_nn, vis.astype(F32)))
    state = (jnp.int32(0), lo0, hi0, cnt0, jnp.max(jnp.where(settled(lo0, hi0, cnt0), 0, 1)))
    _, thr, _, cnt, _ = lax.while_loop(lambda st: (st[0] < _DSA_VALUE_STEPS + 34) & (st[4] > 0), halve_twice, state)
    thr = jnp.maximum(thr, INT_MIN + 1)
    excess = cnt > n_sel

    j_ref[...] = jnp.full(j_ref.shape, s_pad, I32)

    @pl.when(jnp.max(jnp.where(excess, 1.0, 0.0)) > 0.0)
    def _():
        need = n_sel - count(lambda k, col: k > thr)

        def idx_step(_, lohi):
            lo, hi = lohi
            mid = (lo + hi) >> 1
            ok = count(lambda k, col: (k == thr) & (col < mid)) >= need
            return jnp.where(ok, lo, mid), jnp.where(ok, mid, hi)

        steps = int(math.ceil(math.log2(s_pad))) + 1
        _, hi = lax.fori_loop(0, steps, idx_step, (jnp.zeros((1, tq), I32), jnp.full((1, tq), s_pad, I32)))
        j_ref[...] = jnp.broadcast_to(jnp.where(excess, hi, s_pad), j_ref.shape)

    jlim = j_ref[0:1, :]

    rows = DSA_HEADS * tv
    q = jnp.concatenate([qlat_ref[0, :tv, hd * DSA_LATENT:(hd + 1) * DSA_LATENT] for hd in range(DSA_HEADS)], axis=0)

    wb = min(sk, 2 * LANES)

    kpos_wb = lax.broadcasted_iota(I32, (wb, tq), 0)

    def logits(j, i, first):
        blk = pl.ds(pl.multiple_of(j * sk + i * wb, wb), wb)
        if first:
            k = key_ref[blk, :]
            sel = (k > thr) | ((k == thr) & (j * sk + i * wb + kpos_wb < jlim))
            bias = jnp.where(sel, 0.0, NEG_BIG).T[:tv]
            bias_ref[:, blk] = bias
        else:
            bias = bias_ref[:, blk]
        s = _dot_nt(q, c_ref[0, blk, :])
        return (s.reshape(DSA_HEADS, tv, wb) + bias[None]).reshape(rows, wb)

    def attend(m, first):
        def acc_chunk(j, carry):
            l, acc = carry
            ps = []
            for i in range(sk // wb):
                s = logits(j, i, first)
                for t in range(wb // LANES):
                    p = jnp.exp2(s[:, t * LANES:(t + 1) * LANES] - m)
                    l = l + p
                    ps.append(p.astype(BF16))
            return l, acc + _dot(jnp.concatenate(ps, axis=1), c_ref[0, kslice(j), :])

        l, acc = lax.fori_loop(0, nch, acc_chunk, (jnp.zeros((rows, LANES), F32), jnp.zeros((rows, DSA_LATENT), F32)))
        return jnp.sum(l, axis=1, keepdims=True), acc

    def emit(l, acc):
        o = (acc / l).astype(BF16)
        for hd in range(DSA_HEADS):
            o_ref[0, :tv, hd * DSA_HEAD_DIM:(hd + 1) * DSA_HEAD_DIM] = _dot(o[hd * tv:(hd + 1) * tv], wuv_ref[hd]).astype(BF16)
        if tv < tq:
            o_ref[0, tv:, :] = jnp.zeros((tq - tv, o_ref.shape[2]), BF16)

    qf = q.astype(F32)
    bound = jnp.sqrt(jnp.sum(qf * qf, axis=1, keepdims=True)) * cmax_ref[0, 0:1, 0:1]
    l, acc = attend(jnp.broadcast_to(bound, (rows, LANES)), True)
    healthy = jnp.min(l) > _DSA_MIN_SUM

    @pl.when(healthy)
    def _():
        emit(l, acc)

    @pl.when(jnp.logical_not(healthy))
    def _():
        def max_chunk(j, mx):
            for i in range(sk // wb):
                s = logits(j, i, False)
                for t in range(wb // LANES):
                    mx = jnp.maximum(mx, s[:, t * LANES:(t + 1) * LANES])
            return mx

        mx = lax.fori_loop(0, nch, max_chunk, jnp.full((rows, LANES), NEG_BIG, F32))
        emit(*attend(jnp.broadcast_to(jnp.max(mx, axis=1, keepdims=True), (rows, LANES)), False))


def _dsa(qlat, qidx, w, kidx, c, wuv, pos0, s_valid, n_sel, tq):
    b, l_true, _ = qlat.shape
    l = -(-l_true // tq) * tq
    qlat, qidx, w = (jnp.pad(a, ((0, 0), (0, l - l_true), (0, 0))) for a in (qlat, qidx, w))
    wt = w.transpose(0, 2, 1)
    s_pad = c.shape[1]
    sk = min(_DSA_SK, s_pad)
    ka = jnp.pad(kidx, ((0, 0), (0, 0), (0, LANES - IDX_DIM)))
    cf = c.astype(F32)
    cmax = jnp.broadcast_to(jnp.sqrt(jnp.max(jnp.sum(cf * cf, axis=2), axis=1))[:, None, None], (b, 1, LANES))
    qspec = lambda a: pl.BlockSpec((1, tq, a.shape[2]), lambda i, j: (i, j, 0))
    kspec = lambda a: pl.BlockSpec((1,) + a.shape[1:], lambda i, j: (i, 0, 0))
    dh = wuv.shape[0] * wuv.shape[2]
    return pl.pallas_call(
        functools.partial(_dsa_kernel, tq, min(tq, l_true), sk, pos0, s_valid, n_sel),
        grid=(b, l // tq),
        in_specs=[qspec(qlat), qspec(qidx), pl.BlockSpec((1, wt.shape[1], tq), lambda i, j: (i, 0, j)), kspec(ka), kspec(c),
                  kspec(cmax), _const_spec(wuv.shape)],
        out_specs=pl.BlockSpec((1, tq, dh), lambda i, j: (i, j, 0)),
        out_shape=jax.ShapeDtypeStruct((b, l, dh), BF16),
        scratch_shapes=[pltpu.VMEM((s_pad, tq), I32), pltpu.VMEM((min(tq, l_true), s_pad), F32), pltpu.VMEM((8, tq), I32)],
        compiler_params=_params(("parallel", "arbitrary")),
        name="dsa",
    )(qlat, qidx, wt, ka, c, cmax, wuv)[:, :l_true]


_PEER_WIDTH = [PEER_TOPK // (r + 1) for r in range(PEER_TOPK)]


def _top_rows(s, k):
    row = lax.broadcasted_iota(I32, s.shape, 0)
    vals, idxs = [], []
    for _ in range(k):
        m = jnp.max(s, axis=0, keepdims=True)
        first = jnp.min(jnp.where(s == m, row, s.shape[0]), axis=0, keepdims=True)
        s = jnp.where(row == first, -jnp.inf, s)
        vals.append(m)
        idxs.append(first)
    return jnp.concatenate(vals, axis=0), jnp.concatenate(idxs, axis=0)


def _peer_route_kernel(h_ref, g_ref, wq_ref, k1_ref, k2_ref, xn_ref, i1_ref, i2_ref, gate_ref):
    tn = h_ref.shape[0]
    xn = _rms(h_ref[...], g_ref[...]).astype(BF16)
    xn_ref[...] = xn
    q = _dot(xn, wq_ref[...]).astype(BF16)
    oute, outg = [], []
    for hd in range(PEER_HEADS):
        qa = q[:, (2 * hd) * PEER_HALF:(2 * hd + 1) * PEER_HALF]
        qb = q[:, (2 * hd + 1) * PEER_HALF:(2 * hd + 2) * PEER_HALF]
        v1, i1 = _top_rows(_dot_nt(k1_ref[hd], qa), PEER_TOPK)
        v2, i2 = _top_rows(_dot_nt(k2_ref[hd], qb), PEER_TOPK)
        cand = jnp.concatenate([v1[r:r + 1] + v2[0:w] for r, w in enumerate(_PEER_WIDTH)], axis=0)
        cexp = jnp.concatenate([i1[r:r + 1] * PEER_NKEYS + i2[0:w] for r, w in enumerate(_PEER_WIDTH)], axis=0)
        row = lax.broadcasted_iota(I32, cand.shape, 0)
        tops, experts = [], []
        for _ in range(PEER_TOPK):
            m = jnp.max(cand, axis=0, keepdims=True)
            first = jnp.min(jnp.where(cand == m, row, cand.shape[0]), axis=0, keepdims=True)
            hit = row == first
            experts.append(jnp.sum(jnp.where(hit, cexp, 0), axis=0, keepdims=True))
            cand = jnp.where(hit, -jnp.inf, cand)
            tops.append(m)
        top = jnp.concatenate(tops, axis=0)
        ex = jnp.exp(top - top[0:1])
        outg.append(ex / jnp.sum(ex, axis=0, keepdims=True))
        oute.append(jnp.concatenate(experts, axis=0))
    expert = jnp.concatenate(oute, axis=0)
    as_rows = lambda a: pltpu.bitcast(pltpu.bitcast(a, F32).T, I32)
    i1_ref[...] = as_rows(expert >> 7)
    i2_ref[...] = as_rows(expert & (PEER_NKEYS - 1))
    gate_ref[...] = jnp.concatenate(outg, axis=0).T


def _peer_route(h, g, wq, k1, k2):
    n, d = h.shape
    tn = LANES
    slots = PEER_HEADS * PEER_TOPK
    row = lambda w: pl.BlockSpec((tn, w), lambda i: (i, 0))
    return pl.pallas_call(
        _peer_route_kernel,
        grid=(n // tn,),
        in_specs=[row(d), _const_spec(g.shape), _const_spec(wq.shape), _const_spec(k1.shape), _const_spec(k2.shape)],
        out_specs=[row(d), row(slots), row(slots), row(slots)],
        out_shape=[jax.ShapeDtypeStruct((n, d), BF16), jax.ShapeDtypeStruct((n, slots), I32),
                   jax.ShapeDtypeStruct((n, slots), I32), jax.ShapeDtypeStruct((n, slots), F32)],
        compiler_params=_params(("parallel",)),
        name="peer_route",
    )(h, g, wq, k1, k2)


_PLANE_PAD = 4


def _peer_gates_kernel(i1_ref, i2_ref, gate_ref, a_ref, planes_ref):
    tn, slots = i1_ref.shape
    plane = tn + _PLANE_PAD
    sub = lax.broadcasted_iota(I32, (PEER_NKEYS, slots), 0)

    def token(n, carry):
        r = pl.ds(n, 1)
        pt = jnp.where(sub == i1_ref[r, :], gate_ref[r, :], 0.0).astype(BF16)
        qt = jnp.where(sub == i2_ref[r, :], 1.0, 0.0).astype(BF16)
        planes_ref[pl.ds(n, PEER_NKEYS, stride=plane), :] = _dot_nt(pt, qt)
        return carry

    lax.fori_loop(0, tn, token, 0, unroll=32)
    for k in range(PEER_NKEYS):
        a_ref[:, k * PEER_NKEYS:(k + 1) * PEER_NKEYS] = planes_ref[pl.ds(k * plane, tn), :].astype(BF16)


def _peer_gates(i1, i2, gate):
    n, slots = i1.shape
    tn = _pick_tile(n, 128)
    ne = PEER_NKEYS * PEER_NKEYS
    row = lambda w: pl.BlockSpec((tn, w), lambda i: (i, 0))
    return pl.pallas_call(
        _peer_gates_kernel,
        grid=(n // tn,),
        in_specs=[row(slots)] * 3,
        out_specs=row(ne),
        out_shape=jax.ShapeDtypeStruct((n, ne), BF16),
        scratch_shapes=[pltpu.VMEM((PEER_NKEYS * (tn + _PLANE_PAD), PEER_NKEYS), F32)],
        compiler_params=_params(("parallel",)),
        name="peer_gates",
    )(i1, i2, gate)


_PEER_SUB = 256
def _peer_apply_kernel(final_norm, xn_ref, a_ref, u_ref, v_ref, h_ref, g_ref, o_ref, acc_ref):
    j = pl.program_id(1)

    @pl.when(j == 0)
    def _():
        acc_ref[...] = jnp.zeros_like(acc_ref)

    xn = xn_ref[...]
    ws = []
    for s in range(u_ref.shape[0] // _PEER_SUB):
        sub = slice(s * _PEER_SUB, (s + 1) * _PEER_SUB)
        act = _gelu(_dot_nt(xn, u_ref[sub, :]))
        ws.append((act * a_ref[:, sub].astype(F32)).astype(BF16))
    acc_ref[...] += _dot(jnp.concatenate(ws, axis=1), v_ref[...])

    @pl.when(j == pl.num_programs(1) - 1)
    def _():
        y = h_ref[...] + acc_ref[...]
        o_ref[...] = _rms(y, g_ref[...]) if final_norm else y


def _peer_apply(xn, a, u, v, h, g, final_norm):
    n, d = xn.shape
    ne = u.shape[0]
    tn = _pick_tile(n, 512)
    te = 8 * _PEER_SUB
    return pl.pallas_call(
        functools.partial(_peer_apply_kernel, final_norm),
        grid=(n // tn, ne // te),
        in_specs=[pl.BlockSpec((tn, d), lambda i, j: (i, 0)), pl.BlockSpec((tn, te), lambda i, j: (i, j)),
                  pl.BlockSpec((te, d), lambda i, j: (j, 0)), pl.BlockSpec((te, d), lambda i, j: (j, 0)),
                  pl.BlockSpec((tn, d), lambda i, j: (i, 0)), _const_spec(g.shape)],
        out_specs=pl.BlockSpec((tn, d), lambda i, j: (i, 0)),
        out_shape=jax.ShapeDtypeStruct((n, d), F32),
        scratch_shapes=[pltpu.VMEM((tn, d), F32)],
        compiler_params=_params(("parallel", "arbitrary")),
        name="peer_apply",
    )(xn, a, u, v, h, g)


def _split_w_in(w_in, d):
    widths = (d, DSA_HEADS * DSA_LATENT, DSA_LATENT, IDX_HEADS * IDX_DIM, IDX_DIM, IDX_HEADS,
              MEM_HEADS * MEM_HEAD_DIM, N_BRANCH * d)
    offs = np.cumsum((0,) + widths)
    wu, wq, wc, wqi, wki, wwi, wmq, wg = [w_in[:, offs[i]:offs[i + 1]] for i in range(8)]
    wkw = jnp.pad(jnp.concatenate([wki, wwi], axis=1), ((0, 0), (0, LANES - IDX_DIM - IDX_HEADS)))
    kw_scale = jnp.concatenate([jnp.ones((IDX_DIM,), F32), jnp.full((IDX_HEADS,), IDX_HEADS ** -0.5, F32),
                                jnp.zeros((LANES - IDX_DIM - IDX_HEADS,), F32)])[None, :]
    wqi = jnp.pad(wqi.reshape(-1, IDX_HEADS, IDX_DIM), ((0, 0), (0, 0), (0, LANES - IDX_DIM))).reshape(-1, IDX_HEADS * LANES)
    ws = [wu, wq * (DSA_LATENT ** -0.5 * LOG2E), wc, wqi, wkw, wmq * MEM_HEAD_DIM ** -0.5, wg]
    return [w.astype(BF16) for w in ws], kw_scale


def _pad_rows(a, n):
    return a if a.shape[0] == n else jnp.pad(a, ((0, n - a.shape[0]),) + ((0, 0),) * (a.ndim - 1))


def _pad_keys(a, s_pad):
    return jnp.pad(a, ((0, 0), (0, s_pad - a.shape[1]), (0, 0)))


def _token_mix(h, seq_shape, lw, mats, state, cache, mem_kv, pos0):
    b, l = seq_shape
    bf = lambda a: a.astype(BF16)
    seq = lambda a: a.reshape(b, l, -1)
    u, q, c, qi, kw, mq, g = _inproj(h, lw["g_norm1"], lw["g_kv"], lw["kw_scale"], lw["w_in"])
    ki, wi = kw[:, :IDX_DIM], kw[:, IDX_DIM:IDX_DIM + IDX_HEADS]

    ys, s_re, s_im = _ssm_branch(seq(u), mats, state[0], state[1])

    c_all, k_all = bf(seq(c)), bf(seq(ki))
    if cache is not None:
        c_all = jnp.concatenate([bf(cache[0]), c_all], axis=1)
        k_all = jnp.concatenate([bf(cache[1]), k_all], axis=1)
    s_all = c_all.shape[1]
    s_pad = -(-s_all // _DSA_SK) * _DSA_SK if cache is not None else s_all
    yd = _dsa(seq(q), seq(qi), seq(wi), _pad_keys(k_all, s_pad), _pad_keys(c_all, s_pad), lw["w_uv"], pos0, s_all,
              min(DSA_TOPK, s_all // 4), _pick_tile(max(l, LANES), _DSA_TQ))

    ym = _memattn(seq(mq), mem_kv[0], mem_kv[1])

    flat = lambda a: a.reshape(b * l, -1)
    h2 = _merge(h, flat(ys), flat(yd), flat(ym), g, lw["w_glu"], lw["b_glu"], lw["w_br_ssm"], lw["w_br_dsa"],
                lw["w_br_mem"], lw["w_out"])
    return h2, seq(c), seq(ki), s_re, s_im


def _channel_mix(h2, lw, g_final, final_norm):
    n = h2.shape[0]
    h2 = _pad_rows(h2, -(-n // LANES) * LANES)
    xn, i1, i2, gate = _peer_route(h2, lw["g_norm2"], lw["w_peer_q"], lw["peer_sub_k1"], lw["peer_sub_k2"])
    a = _peer_gates(i1, i2, gate)
    return _peer_apply(xn, a, lw["peer_u"], lw["peer_v"], h2, g_final, final_norm)[:n]


def kernel(x_prompt, x_sample, mem_prompt, cache_dsa_latent, cache_dsa_idx_k, state_ssm_re, state_ssm_im, cache_mem_k, cache_mem_v, g_norm1, w_in, g_kv, w_uv, ssm_lam_re, ssm_lam_im, ssm_log_dt, ssm_b_re, ssm_b_im, ssm_c_re, ssm_c_im, ssm_d, w_glu, b_glu, g_mem, w_mem_kv, w_br_ssm, w_br_dsa, w_br_mem, w_out, g_norm2, w_peer_q, peer_sub_k1, peer_sub_k2, peer_u, peer_v, g_final):
    depth = w_in.shape[0]
    bp, lp, d = x_prompt.shape
    bs, ls, _ = x_sample.shape
    past = cache_dsa_latent.shape[2]
    mem_w = MEM_HEADS * MEM_HEAD_DIM
    n_mem = mem_prompt.shape[1]
    heads = (n_mem, MEM_HEADS, MEM_HEAD_DIM)
    bf = lambda a: a.astype(BF16)
    row = lambda a: a[None, :]

    hp, hs = x_prompt.reshape(bp * lp, d), x_sample.reshape(bs * ls, d)
    outs = [[] for _ in range(10)]
    for l in range(depth):
        ws, kw_scale = _split_w_in(w_in[l], d)
        lw = dict(g_norm1=row(g_norm1[l]), g_kv=row(g_kv[l]), kw_scale=kw_scale, w_in=ws, w_uv=bf(w_uv[l]),
                  w_glu=bf(w_glu[l]), b_glu=row(b_glu[l]), w_br_ssm=bf(w_br_ssm[l]), w_br_dsa=bf(w_br_dsa[l]),
                  w_br_mem=bf(w_br_mem[l]), w_out=bf(w_out[l]), g_norm2=row(g_norm2[l]), w_peer_q=bf(w_peer_q[l]),
                  peer_sub_k1=bf(peer_sub_k1[l]), peer_sub_k2=bf(peer_sub_k2[l]), peer_u=bf(peer_u[l]),
                  peer_v=bf(peer_v[l]))
        mats = _ssm_matrices(ssm_lam_re[l], ssm_lam_im[l], ssm_log_dt[l], ssm_b_re[l], ssm_b_im[l],
                             ssm_c_re[l], ssm_c_im[l], ssm_d[l], SSM_T)
        kv = _memkv(mem_prompt.reshape(bp * n_mem, d), row(g_mem[l]), bf(w_mem_kv[l]))
        mk_p, mv_p = kv[:, :mem_w].reshape(bp, n_mem, mem_w), kv[:, mem_w:].reshape(bp, n_mem, mem_w)
        zeros = jnp.zeros((bp,) + state_ssm_re.shape[2:], F32)

        h2p, c_p, ki_p, sre_p, sim_p = _token_mix(hp, (bp, lp), lw, mats, (zeros, zeros), None, (bf(mk_p), bf(mv_p)), 0)
        mem_s = (bf(cache_mem_k[l].reshape(bs, n_mem, mem_w)), bf(cache_mem_v[l].reshape(bs, n_mem, mem_w)))
        h2s, c_s, ki_s, sre_s, sim_s = _token_mix(hs, (bs, ls), lw, mats, (state_ssm_re[l], state_ssm_im[l]),
                                                  (cache_dsa_latent[l], cache_dsa_idx_k[l]), mem_s, past)
        final = l == depth - 1
        hp = _channel_mix(h2p, lw, row(g_final), final)
        hs = _channel_mix(h2s, lw, row(g_final), final)
        for lst, val in zip(outs, (c_p, ki_p, sre_p, sim_p, mk_p.reshape((bp,) + heads), mv_p.reshape((bp,) + heads),
                                   c_s, ki_s, sre_s, sim_s)):
            lst.append(val)
    return (hp.reshape(bp, lp, d), hs.reshape(bs, ls, d)) + tuple(jnp.stack(o) for o in outs)
```

```python
import functools
import math

import jax
import jax.numpy as jnp
import numpy as np
from jax import lax
from jax.experimental import pallas as pl
from jax.experimental.pallas import tpu as pltpu

F32 = jnp.float32
BF16 = jnp.bfloat16
I32 = jnp.int32

EPS = 1e-6
CHUNK = 64
SSM_GROUP = 16
SSM_STATE = 64
SSM_T = 32
DSA_HEADS = 8
DSA_LATENT = 256
DSA_HEAD_DIM = 128
IDX_HEADS = 8
IDX_DIM = 64
DSA_TOPK = 256
MEM_HEADS = 4
MEM_HEAD_DIM = 256
PEER_HEADS = 8
PEER_NKEYS = 128
PEER_HALF = 128
PEER_TOPK = 16
N_BRANCH = 3

LANES = 128
VMEM_LIMIT = 56 * 1024 * 1024
INT_MIN = -2 ** 31
INT_MAX = 2 ** 31 - 1
NEG_BIG = -1e30
LOG2E = 1.4426950408889634


def _pick_tile(n, target):
    if n <= target:
        return n
    for t in range(target, 7, -1):
        if n % t == 0 and t % 8 == 0:
            return t
    return n


def _params(sem):
    return pltpu.CompilerParams(dimension_semantics=sem, vmem_limit_bytes=VMEM_LIMIT)


def _const_spec(shape):
    nd = len(shape)
    return pl.BlockSpec(shape, lambda *_: (0,) * nd, pipeline_mode=pl.Buffered(1))


def _rms(x, g):
    return x * lax.rsqrt(jnp.mean(x * x, axis=-1, keepdims=True) + EPS) * g


def _gelu(x):
    return 0.5 * x * (1.0 + jnp.tanh(0.7978845608028654 * (x + 0.044715 * x * x * x)))


def _sigmoid(x):
    return 1.0 / (1.0 + jnp.exp(-x))


def _dot_nt(a, b):
    return lax.dot_general(a, b, (((1,), (1,)), ((), ())), preferred_element_type=F32)


def _dot(a, b):
    return jnp.dot(a, b, preferred_element_type=F32)


def _inproj_kernel(x_ref, g1_ref, gkv_ref, kws_ref, wu_ref, wq_ref, wc_ref, wqi_ref, wkw_ref, wmq_ref, wg_ref,
                   u_ref, q_ref, c_ref, c16_ref, qi_ref, kw_ref, k16_ref, mq_ref, g_ref):
    xn = _rms(x_ref[...], g1_ref[...]).astype(BF16)
    u_ref[...] = _dot(xn, wu_ref[...]).astype(BF16)
    q_ref[...] = _dot(xn, wq_ref[...]).astype(BF16)
    c = _rms(_dot(xn, wc_ref[...]), gkv_ref[...])
    c_ref[...] = c
    c16_ref[...] = c.astype(BF16)
    qi_ref[...] = _dot(xn, wqi_ref[...]).astype(BF16)
    kw = _dot(xn, wkw_ref[...]) * kws_ref[...]
    kw_ref[...] = kw
    is_key = lax.broadcasted_iota(I32, kw.shape, 1) < IDX_DIM
    k16_ref[...] = jnp.where(is_key, kw, 0.0).astype(BF16)
    mq_ref[...] = _dot(xn, wmq_ref[...]).astype(BF16)
    g_ref[...] = _sigmoid(_dot(xn, wg_ref[...])).astype(BF16)


def _inproj(h, g1, gkv, kw_scale, ws):
    n, d = h.shape
    tm = _pick_tile(n, 256)
    wu, wq, wc, wqi, wkw, wmq, wg = (w.shape[1] for w in ws)
    outs = [(wu, BF16), (wq, BF16), (wc, F32), (wc, BF16), (wqi, BF16), (wkw, F32), (wkw, BF16), (wmq, BF16), (wg, BF16)]
    row = lambda w: pl.BlockSpec((tm, w), lambda i: (i, 0))
    return pl.pallas_call(
        _inproj_kernel,
        grid=(n // tm,),
        in_specs=[row(d), _const_spec(g1.shape), _const_spec(gkv.shape), _const_spec(kw_scale.shape)]
        + [_const_spec(w.shape) for w in ws],
        out_specs=[row(w) for w, _ in outs],
        out_shape=[jax.ShapeDtypeStruct((n, w), dt) for w, dt in outs],
        compiler_params=_params(("parallel",)),
        name="inproj",
    )(h, g1, gkv, kw_scale, *ws)


def _memkv_kernel(x_ref, g_ref, w_ref, o_ref):
    xn = _rms(x_ref[...], g_ref[...]).astype(BF16)
    o_ref[...] = _dot(xn, w_ref[...])


def _memkv(mem, g, w):
    n, d = mem.shape
    tm = _pick_tile(n, 256)
    return pl.pallas_call(
        _memkv_kernel,
        grid=(n // tm,),
        in_specs=[pl.BlockSpec((tm, d), lambda i: (i, 0)), _const_spec(g.shape), _const_spec(w.shape)],
        out_specs=pl.BlockSpec((tm, w.shape[1]), lambda i: (i, 0)),
        out_shape=jax.ShapeDtypeStruct((n, w.shape[1]), F32),
        compiler_params=_params(("parallel",)),
        name="memkv",
    )(mem, g, w)


def _memattn_kernel(q_ref, k_ref, v_ref, o_ref):
    for hd in range(MEM_HEADS):
        sl = slice(hd * MEM_HEAD_DIM, (hd + 1) * MEM_HEAD_DIM)
        logits = _dot_nt(q_ref[0, :, sl], k_ref[0, :, sl])
        m = jnp.max(logits, axis=-1, keepdims=True)
        p = jnp.exp(logits - m)
        l = jnp.sum(p, axis=-1, keepdims=True)
        o = _dot(p.astype(BF16), v_ref[0, :, sl]) / l
        o_ref[0, :, sl] = o.astype(BF16)


def _memattn(q, k, v):
    b, l, w = q.shape
    tl = _pick_tile(l, 512)
    nm = k.shape[1]
    return pl.pallas_call(
        _memattn_kernel,
        grid=(b, l // tl),
        in_specs=[pl.BlockSpec((1, tl, w), lambda i, j: (i, j, 0)),
                  pl.BlockSpec((1, nm, w), lambda i, j: (i, 0, 0)),
                  pl.BlockSpec((1, nm, w), lambda i, j: (i, 0, 0))],
        out_specs=pl.BlockSpec((1, tl, w), lambda i, j: (i, j, 0)),
        out_shape=jax.ShapeDtypeStruct((b, l, w), BF16),
        compiler_params=_params(("parallel", "parallel")),
        name="memattn",
    )(q, k, v)


def _merge_kernel(h_ref, ys_ref, yd_ref, ym_ref, g_ref, wglu_ref, bglu_ref, wbs_ref, wbd_ref, wbm_ref, wout_ref,
                  o_ref):
    d = h_ref.shape[1]
    z = _gelu(ys_ref[...].astype(F32))
    gate = _sigmoid(_dot(z.astype(BF16), wglu_ref[...]) + bglu_ref[...])
    a = _dot((z * gate).astype(BF16), wbs_ref[...])
    b = _dot(yd_ref[...], wbd_ref[...])
    c = _dot(ym_ref[...], wbm_ref[...])
    g = g_ref[...].astype(F32)
    merged = g[:, 0:d] * a + g[:, d:2 * d] * b + g[:, 2 * d:3 * d] * c
    o_ref[...] = h_ref[...] + _dot(merged.astype(BF16), wout_ref[...])


def _merge(h, ys, yd, ym, g, wglu, bglu, wbs, wbd, wbm, wout):
    n, d = h.shape
    tm = _pick_tile(n, 512)
    row = lambda w: pl.BlockSpec((tm, w), lambda i: (i, 0))
    consts = [wglu, bglu, wbs, wbd, wbm, wout]
    return pl.pallas_call(
        _merge_kernel,
        grid=(n // tm,),
        in_specs=[row(d), row(d), row(d), row(d), row(3 * d)] + [_const_spec(c.shape) for c in consts],
        out_specs=row(d),
        out_shape=jax.ShapeDtypeStruct((n, d), F32),
        compiler_params=_params(("parallel",)),
        name="merge",
    )(h, ys, yd, ym, g, *consts)


def _ssm_matrices(lam_re, lam_im, log_dt, b_re, b_im, c_re, c_im, d, t_len):
    hi = lax.Precision.HIGHEST
    g_n, p_n = lam_re.shape
    dt = jnp.exp(log_dt)[:, None]
    mag = jnp.exp(lam_re * dt)
    ar, ai = mag * jnp.cos(lam_im * dt), mag * jnp.sin(lam_im * dt)
    den = lam_re * lam_re + lam_im * lam_im
    nr, ni = ar - 1.0, ai
    kr = ((nr * lam_re + ni * lam_im) / den)[..., None]
    ki = ((ni * lam_re - nr * lam_im) / den)[..., None]
    bbr, bbi = kr * b_re - ki * b_im, kr * b_im + ki * b_re
    j = jnp.arange(t_len + 1, dtype=F32)[:, None, None]
    pmag = jnp.exp(j * (lam_re * dt))
    pr, pi = pmag * jnp.cos(j * (lam_im * dt)), pmag * jnp.sin(j * (lam_im * dt))
    mr = pr[:t_len, ..., None] * bbr - pi[:t_len, ..., None] * bbi
    mi = pr[:t_len, ..., None] * bbi + pi[:t_len, ..., None] * bbr
    kern = (jnp.einsum('gdp,jgpc->jgdc', c_re, mr, precision=hi)
            - jnp.einsum('gdp,jgpc->jgdc', c_im, mi, precision=hi))
    s_i = jnp.arange(t_len)[:, None]
    t_i = jnp.arange(t_len)[None, :]
    lag = t_i - s_i
    kg = jnp.where((lag >= 0)[:, :, None, None, None], kern[jnp.clip(lag, 0)], 0.0)
    tz = kg.transpose(2, 0, 4, 1, 3).reshape(g_n, t_len * SSM_GROUP, t_len * SSM_GROUP)
    pad = ((0, 0), (0, 0), (0, LANES - p_n))
    vr = jnp.pad(mr[::-1].transpose(1, 0, 3, 2).reshape(g_n, t_len * SSM_GROUP, p_n), pad)
    vi = jnp.pad(mi[::-1].transpose(1, 0, 3, 2).reshape(g_n, t_len * SSM_GROUP, p_n), pad)
    tzv = jnp.concatenate([tz, vr, vi], axis=-1).astype(BF16)
    pr1, pi1 = pr[1:].transpose(1, 2, 0), pi[1:].transpose(1, 2, 0)
    crt, cit = c_re.transpose(0, 2, 1), c_im.transpose(0, 2, 1)
    wre = crt[:, :, None, :] * pr1[..., None] - cit[:, :, None, :] * pi1[..., None]
    wim = -(crt[:, :, None, :] * pi1[..., None] + cit[:, :, None, :] * pr1[..., None])
    rpad = ((0, 0), (0, LANES - p_n), (0, 0))
    wre = jnp.pad(wre.reshape(g_n, p_n, -1), rpad).astype(BF16)
    wim = jnp.pad(wim.reshape(g_n, p_n, -1), rpad).astype(BF16)
    atr = jnp.pad(pr[t_len], ((0, 0), (0, LANES - p_n)))[:, None, :]
    ati = jnp.pad(pi[t_len], ((0, 0), (0, LANES - p_n)))[:, None, :]
    drow = jnp.tile(d.reshape(g_n, 1, SSM_GROUP), (1, t_len, 1)).reshape(g_n, 1, t_len * SSM_GROUP)
    return tzv, wre, wim, atr, ati, drow


def _ssm_kernel(nk, nb, u_ref, tzv_ref, wre_ref, wim_ref, atr_ref, ati_ref, d_ref, ire_ref, iim_ref,
                y_ref, fre_ref, fim_ref, yi_ref, sr_ref, si_ref, xr_ref, xi_ref):
    tc = u_ref.shape[2]
    u = u_ref[0]
    full = _dot(u, tzv_ref[0])
    yi_ref[...] = full[:, :tc]
    sr_ref[...] = full[:, tc:tc + LANES]
    si_ref[...] = full[:, tc + LANES:]
    atr, ati = atr_ref[0], ati_ref[0]

    def step(k, carry):
        xr, xi = carry
        rows = pl.ds(pl.multiple_of(k * nb, nb), nb)
        xr_ref[rows, :] = xr
        xi_ref[rows, :] = xi
        return (atr * xr - ati * xi + sr_ref[rows, :], atr * xi + ati * xr + si_ref[rows, :])

    xr, xi = lax.fori_loop(0, nk, step, (ire_ref[0], iim_ref[0]))
    fre_ref[0] = xr
    fim_ref[0] = xi
    y = (yi_ref[...] + _dot(xr_ref[...].astype(BF16), wre_ref[0]) + _dot(xi_ref[...].astype(BF16), wim_ref[0])
         + d_ref[0] * u.astype(F32))
    y_ref[0] = y.astype(BF16)


def _ssm(u, mats, init_re, init_im, nk, nb):
    tzv, wre, wim, atr, ati, drow = mats
    g_n, r, tc = u.shape
    per_g = lambda a: pl.BlockSpec((1,) + a.shape[1:], lambda g: (g, 0, 0))
    ins = [u, tzv, wre, wim, atr, ati, drow, init_re, init_im]
    st = jax.ShapeDtypeStruct((g_n, nb, LANES), F32)
    return pl.pallas_call(
        functools.partial(_ssm_kernel, nk, nb),
        grid=(g_n,),
        in_specs=[per_g(a) for a in ins],
        out_specs=[per_g(u), per_g(init_re), per_g(init_re)],
        out_shape=[jax.ShapeDtypeStruct(u.shape, BF16), st, st],
        scratch_shapes=[pltpu.VMEM((r, tc), F32)] + [pltpu.VMEM((r, LANES), F32)] * 4,
        compiler_params=_params(("parallel",)),
        name="ssm",
    )(*ins)


def _ssm_branch(zu, mats, st_re, st_im):
    b, l, w = zu.shape
    g_n, p_n = st_re.shape[1], st_re.shape[2]
    nk = l // SSM_T
    ug = zu.reshape(b, nk, SSM_T, g_n, SSM_GROUP).transpose(3, 1, 0, 2, 4).reshape(g_n, nk * b, SSM_T * SSM_GROUP)
    pad = ((0, 0), (0, 0), (0, LANES - p_n))
    ire = jnp.pad(st_re.transpose(1, 0, 2), pad)
    iim = jnp.pad(st_im.transpose(1, 0, 2), pad)
    y, fre, fim = _ssm(ug, mats, ire, iim, nk, b)
    y = y.reshape(g_n, nk, b, SSM_T, SSM_GROUP).transpose(2, 1, 3, 0, 4).reshape(b, l, w)
    return y, fre[:, :, :p_n].transpose(1, 0, 2), fim[:, :, :p_n].transpose(1, 0, 2)


_DSA_TQ = 256
_DSA_VALUE_STEPS = 24
_DSA_SK = 1024
_DSA_MIN_SUM = 2.0 ** -80


def _dsa_kernel(tq, tv, sk, pos0, s_valid, n_sel, qlat_ref, qidx_ref, w_ref, ka_ref, c_ref, cmax_ref, wuv_ref,
                o_ref, key_ref, bias_ref, j_ref):
    s_pad = key_ref.shape[0]
    q0 = pos0 + pl.program_id(1) * tq
    qpos = q0 + lax.broadcasted_iota(I32, (1, tq), 1)
    vis = jnp.minimum((qpos // CHUNK + 1) * CHUNK, s_valid)
    vis_max = jnp.minimum(((q0 + tq - 1) // CHUNK + 1) * CHUNK, s_valid)
    nch = (vis_max + sk - 1) // sk
    kpos = lax.broadcasted_iota(I32, (sk, tq), 0)
    kslice = lambda j: pl.ds(pl.multiple_of(j * sk, sk), sk)
    fold = lambda a: a.reshape(sk // 8, 8, tq)

    qs = jnp.concatenate([qidx_ref[0, :, hd * LANES:(hd + 1) * LANES] for hd in range(IDX_HEADS)], axis=0)

    def score_chunk(j, carry):
        kmin, kmax = carry
        rel = jnp.maximum(_dot_nt(ka_ref[0, kslice(j), :], qs), 0.0)
        acc = jnp.zeros((sk, tq), F32)
        for hd in range(IDX_HEADS):
            acc = acc + w_ref[0, hd:hd + 1, :] * rel[:, hd * tq:(hd + 1) * tq]
        bits = pltpu.bitcast(acc, I32)
        key = bits ^ ((bits >> 31) & 0x7FFFFFFF)
        visible = j * sk + kpos < vis
        key_ref[kslice(j), :] = jnp.where(visible, key, INT_MIN)
        kmin = jnp.minimum(kmin, jnp.min(fold(jnp.where(visible, key, INT_MAX)), axis=0))
        kmax = jnp.maximum(kmax, jnp.max(fold(jnp.where(visible, key, INT_MIN)), axis=0))
        return kmin, kmax

    kmin, kmax = lax.fori_loop(0, nch, score_chunk,
                               (jnp.full((8, tq), INT_MAX, I32), jnp.full((8, tq), INT_MIN, I32)))
    kmin = jnp.min(kmin, axis=0, keepdims=True)
    kmax = jnp.max(kmax, axis=0, keepdims=True)

    def count(pred):
        def body(j, c):
            hit = jnp.where(pred(key_ref[kslice(j), :], j * sk + kpos), 1.0, 0.0)
            return c + jnp.sum(fold(hit), axis=0)
        return jnp.sum(lax.fori_loop(0, nch, body, jnp.zeros((8, tq), F32)), axis=0, keepdims=True)

    key_of = lambda v: (lambda b: b ^ ((b >> 31) & 0x7FFFFFFF))(pltpu.bitcast(v, I32))
    val_of = lambda k: pltpu.bitcast(k ^ ((k >> 31) & 0x7FFFFFFF), F32)

    def settled(lo, hi, cnt):
        return (cnt <= n_sel) | (hi - 1 <= lo)

    def halve(it, lo, hi, cnt):
        mid_v = key_of(0.5 * val_of(lo) + 0.5 * val_of(hi))
        mid_k = (lo >> 1) + (hi >> 1) + (lo & hi & 1)
        mid = jnp.where((mid_v > lo) & (mid_v < hi) & (it < _DSA_VALUE_STEPS), mid_v, mid_k)
        c = count(lambda k, col: k >= mid)
        live = jnp.logical_not(settled(lo, hi, cnt))
        up = live & (c >= n_sel)
        return jnp.where(up, mid, lo), jnp.where(live & (c < n_sel), mid, hi), jnp.where(up, c, cnt)

    def halve_twice(state):
        it, lo, hi, cnt, _ = state
        lo, hi, cnt = halve(it, lo, hi, cnt)
        lo, hi, cnt = halve(it + 1, lo, hi, cnt)
        return it + 2, lo, hi, cnt, jnp.max(jnp.where(settled(lo, hi, cnt), 0, 1))

    c_pos, c_nn = count(lambda k, col: k >= 1), count(lambda k, col: k >= 0)
    pos, zero = c_pos >= n_sel, c_nn >= n_sel
    lo0 = jnp.where(pos, 1, jnp.where(zero, 0, kmin))
    hi0 = jnp.where(pos, jnp.where(kmax == INT_MAX, INT_MAX, kmax + 1), jnp.where(zero, 1, 0))
    cnt0 = jnp.where(pos, c_pos, jnp.where(zero, c_nn, vis.astype(F32)))
    state = (jnp.int32(0), lo0, hi0, cnt0, jnp.max(jnp.where(settled(lo0, hi0, cnt0), 0, 1)))
    _, thr, _, cnt, _ = lax.while_loop(lambda st: (st[0] < _DSA_VALUE_STEPS + 34) & (st[4] > 0), halve_twice, state)
    thr = jnp.maximum(thr, INT_MIN + 1)
    excess = cnt > n_sel

    j_ref[...] = jnp.full(j_ref.shape, s_pad, I32)

    @pl.when(jnp.max(jnp.where(excess, 1.0, 0.0)) > 0.0)
    def _():
        need = n_sel - count(lambda k, col: k > thr)

        def idx_step(_, lohi):
            lo, hi = lohi
            mid = (lo + hi) >> 1
            ok = count(lambda k, col: (k == thr) & (col < mid)) >= need
            return jnp.where(ok, lo, mid), jnp.where(ok, mid, hi)

        steps = int(math.ceil(math.log2(s_pad))) + 1
        _, hi = lax.fori_loop(0, steps, idx_step, (jnp.zeros((1, tq), I32), jnp.full((1, tq), s_pad, I32)))
        j_ref[...] = jnp.broadcast_to(jnp.where(excess, hi, s_pad), j_ref.shape)

    jlim = j_ref[0:1, :]

    rows = DSA_HEADS * tv
    q = jnp.concatenate([qlat_ref[0, :tv, hd * DSA_LATENT:(hd + 1) * DSA_LATENT] for hd in range(DSA_HEADS)], axis=0)

    wb = min(sk, 2 * LANES)

    kpos_wb = lax.broadcasted_iota(I32, (wb, tq), 0)

    def logits(j, i, first):
        blk = pl.ds(pl.multiple_of(j * sk + i * wb, wb), wb)
        if first:
            k = key_ref[blk, :]
            sel = (k > thr) | ((k == thr) & (j * sk + i * wb + kpos_wb < jlim))
            bias = jnp.where(sel, 0.0, NEG_BIG).T[:tv]
            bias_ref[:, blk] = bias
        else:
            bias = bias_ref[:, blk]
        s = _dot_nt(q, c_ref[0, blk, :])
        return (s.reshape(DSA_HEADS, tv, wb) + bias[None]).reshape(rows, wb)

    def attend(m, first):
        def acc_chunk(j, carry):
            l, acc = carry
            ps = []
            for i in range(sk // wb):
                s = logits(j, i, first)
                for t in range(wb // LANES):
                    p = jnp.exp2(s[:, t * LANES:(t + 1) * LANES] - m)
                    l = l + p
                    ps.append(p.astype(BF16))
            return l, acc + _dot(jnp.concatenate(ps, axis=1), c_ref[0, kslice(j), :])

        l, acc = lax.fori_loop(0, nch, acc_chunk, (jnp.zeros((rows, LANES), F32), jnp.zeros((rows, DSA_LATENT), F32)))
        return jnp.sum(l, axis=1, keepdims=True), acc

    def emit(l, acc):
        o = (acc / l).astype(BF16)
        for hd in range(DSA_HEADS):
            o_ref[0, :tv, hd * DSA_HEAD_DIM:(hd + 1) * DSA_HEAD_DIM] = _dot(o[hd * tv:(hd + 1) * tv], wuv_ref[hd]).astype(BF16)
        if tv < tq:
            o_ref[0, tv:, :] = jnp.zeros((tq - tv, o_ref.shape[2]), BF16)

    qf = q.astype(F32)
    bound = jnp.sqrt(jnp.sum(qf * qf, axis=1, keepdims=True)) * cmax_ref[0, 0:1, 0:1]
    l, acc = attend(jnp.broadcast_to(bound, (rows, LANES)), True)
    healthy = jnp.min(l) > _DSA_MIN_SUM

    @pl.when(healthy)
    def _():
        emit(l, acc)

    @pl.when(jnp.logical_not(healthy))
    def _():
        def max_chunk(j, mx):
            for i in range(sk // wb):
                s = logits(j, i, False)
                for t in range(wb // LANES):
                    mx = jnp.maximum(mx, s[:, t * LANES:(t + 1) * LANES])
            return mx

        mx = lax.fori_loop(0, nch, max_chunk, jnp.full((rows, LANES), NEG_BIG, F32))
        emit(*attend(jnp.broadcast_to(jnp.max(mx, axis=1, keepdims=True), (rows, LANES)), False))


def _dsa(qlat, qidx, w, ka, c, wuv, pos0, s_valid, n_sel, tq):
    b, l_true, _ = qlat.shape
    l = -(-l_true // tq) * tq
    qlat, qidx, w = (jnp.pad(a, ((0, 0), (0, l - l_true), (0, 0))) for a in (qlat, qidx, w))
    wt = w.transpose(0, 2, 1)
    s_pad = c.shape[1]
    sk = min(_DSA_SK, s_pad)
    cf = c.astype(F32)
    cmax = jnp.broadcast_to(jnp.sqrt(jnp.max(jnp.sum(cf * cf, axis=2), axis=1))[:, None, None], (b, 1, LANES))
    qspec = lambda a: pl.BlockSpec((1, tq, a.shape[2]), lambda i, j: (i, j, 0))
    kspec = lambda a: pl.BlockSpec((1,) + a.shape[1:], lambda i, j: (i, 0, 0))
    dh = wuv.shape[0] * wuv.shape[2]
    return pl.pallas_call(
        functools.partial(_dsa_kernel, tq, min(tq, l_true), sk, pos0, s_valid, n_sel),
        grid=(b, l // tq),
        in_specs=[qspec(qlat), qspec(qidx), pl.BlockSpec((1, wt.shape[1], tq), lambda i, j: (i, 0, j)), kspec(ka), kspec(c),
                  kspec(cmax), _const_spec(wuv.shape)],
        out_specs=pl.BlockSpec((1, tq, dh), lambda i, j: (i, j, 0)),
        out_shape=jax.ShapeDtypeStruct((b, l, dh), BF16),
        scratch_shapes=[pltpu.VMEM((s_pad, tq), I32), pltpu.VMEM((min(tq, l_true), s_pad), F32), pltpu.VMEM((8, tq), I32)],
        compiler_params=_params(("parallel", "arbitrary")),
        name="dsa",
    )(qlat, qidx, wt, ka, c, cmax, wuv)[:, :l_true]


_PEER_WIDTH = [PEER_TOPK // (r + 1) for r in range(PEER_TOPK)]


def _top_rows(s, k):
    row = lax.broadcasted_iota(I32, s.shape, 0)
    vals, idxs = [], []
    for _ in range(k):
        m = jnp.max(s, axis=0, keepdims=True)
        first = jnp.min(jnp.where(s == m, row, s.shape[0]), axis=0, keepdims=True)
        s = jnp.where(row == first, -jnp.inf, s)
        vals.append(m)
        idxs.append(first)
    return jnp.concatenate(vals, axis=0), jnp.concatenate(idxs, axis=0)


def _peer_route_kernel(h_ref, g_ref, wq_ref, k1_ref, k2_ref, xn_ref, i1_ref, i2_ref, gate_ref):
    tn = h_ref.shape[0]
    xn = _rms(h_ref[...], g_ref[...]).astype(BF16)
    xn_ref[...] = xn
    q = _dot(xn, wq_ref[...]).astype(BF16)
    oute, outg = [], []
    for hd in range(PEER_HEADS):
        qa = q[:, (2 * hd) * PEER_HALF:(2 * hd + 1) * PEER_HALF]
        qb = q[:, (2 * hd + 1) * PEER_HALF:(2 * hd + 2) * PEER_HALF]
        v1, i1 = _top_rows(_dot_nt(k1_ref[hd], qa), PEER_TOPK)
        v2, i2 = _top_rows(_dot_nt(k2_ref[hd], qb), PEER_TOPK)
        cand = jnp.concatenate([v1[r:r + 1] + v2[0:w] for r, w in enumerate(_PEER_WIDTH)], axis=0)
        cexp = jnp.concatenate([i1[r:r + 1] * PEER_NKEYS + i2[0:w] for r, w in enumerate(_PEER_WIDTH)], axis=0)
        row = lax.broadcasted_iota(I32, cand.shape, 0)
        tops, experts = [], []
        for _ in range(PEER_TOPK):
            m = jnp.max(cand, axis=0, keepdims=True)
            first = jnp.min(jnp.where(cand == m, row, cand.shape[0]), axis=0, keepdims=True)
            hit = row == first
            experts.append(jnp.sum(jnp.where(hit, cexp, 0), axis=0, keepdims=True))
            cand = jnp.where(hit, -jnp.inf, cand)
            tops.append(m)
        top = jnp.concatenate(tops, axis=0)
        ex = jnp.exp(top - top[0:1])
        outg.append(ex / jnp.sum(ex, axis=0, keepdims=True))
        oute.append(jnp.concatenate(experts, axis=0))
    expert = jnp.concatenate(oute, axis=0)
    as_rows = lambda a: pltpu.bitcast(pltpu.bitcast(a, F32).T, I32)
    i1_ref[...] = as_rows(expert >> 7)
    i2_ref[...] = as_rows(expert & (PEER_NKEYS - 1))
    gate_ref[...] = jnp.concatenate(outg, axis=0).T


def _peer_route(h, g, wq, k1, k2):
    n, d = h.shape
    tn = LANES
    slots = PEER_HEADS * PEER_TOPK
    row = lambda w: pl.BlockSpec((tn, w), lambda i: (i, 0))
    return pl.pallas_call(
        _peer_route_kernel,
        grid=(n // tn,),
        in_specs=[row(d), _const_spec(g.shape), _const_spec(wq.shape), _const_spec(k1.shape), _const_spec(k2.shape)],
        out_specs=[row(d), row(slots), row(slots), row(slots)],
        out_shape=[jax.ShapeDtypeStruct((n, d), BF16), jax.ShapeDtypeStruct((n, slots), I32),
                   jax.ShapeDtypeStruct((n, slots), I32), jax.ShapeDtypeStruct((n, slots), F32)],
        compiler_params=_params(("parallel",)),
        name="peer_route",
    )(h, g, wq, k1, k2)


_PLANE_PAD = 4


def _peer_gates_kernel(i1_ref, i2_ref, gate_ref, a_ref, planes_ref):
    tn, slots = i1_ref.shape
    plane = tn + _PLANE_PAD
    sub = lax.broadcasted_iota(I32, (PEER_NKEYS, slots), 0)

    def token(n, carry):
        r = pl.ds(n, 1)
        pt = jnp.where(sub == i1_ref[r, :], gate_ref[r, :], 0.0).astype(BF16)
        qt = jnp.where(sub == i2_ref[r, :], 1.0, 0.0).astype(BF16)
        planes_ref[pl.ds(n, PEER_NKEYS, stride=plane), :] = _dot_nt(pt, qt)
        return carry

    lax.fori_loop(0, tn, token, 0, unroll=64)
    for k in range(PEER_NKEYS):
        a_ref[:, k * PEER_NKEYS:(k + 1) * PEER_NKEYS] = planes_ref[pl.ds(k * plane, tn), :].astype(BF16)


def _peer_gates(i1, i2, gate):
    n, slots = i1.shape
    tn = _pick_tile(n, 128)
    ne = PEER_NKEYS * PEER_NKEYS
    row = lambda w: pl.BlockSpec((tn, w), lambda i: (i, 0))
    return pl.pallas_call(
        _peer_gates_kernel,
        grid=(n // tn,),
        in_specs=[row(slots)] * 3,
        out_specs=row(ne),
        out_shape=jax.ShapeDtypeStruct((n, ne), BF16),
        scratch_shapes=[pltpu.VMEM((PEER_NKEYS * (tn + _PLANE_PAD), PEER_NKEYS), F32)],
        compiler_params=_params(("parallel",)),
        name="peer_gates",
    )(i1, i2, gate)


_PEER_SUB = 256
def _peer_apply_kernel(final_norm, xn_ref, a_ref, u_ref, v_ref, h_ref, g_ref, o_ref, acc_ref):
    j = pl.program_id(1)

    @pl.when(j == 0)
    def _():
        acc_ref[...] = jnp.zeros_like(acc_ref)

    xn = xn_ref[...]
    ws = []
    for s in range(u_ref.shape[0] // _PEER_SUB):
        sub = slice(s * _PEER_SUB, (s + 1) * _PEER_SUB)
        act = _gelu(_dot_nt(xn, u_ref[sub, :]))
        ws.append((act * a_ref[:, sub].astype(F32)).astype(BF16))
    acc_ref[...] += _dot(jnp.concatenate(ws, axis=1), v_ref[...])

    @pl.when(j == pl.num_programs(1) - 1)
    def _():
        y = h_ref[...] + acc_ref[...]
        o_ref[...] = _rms(y, g_ref[...]) if final_norm else y


def _peer_apply(xn, a, u, v, h, g, final_norm):
    n, d = xn.shape
    ne = u.shape[0]
    tn = _pick_tile(n, 512)
    te = 8 * _PEER_SUB
    return pl.pallas_call(
        functools.partial(_peer_apply_kernel, final_norm),
        grid=(n // tn, ne // te),
        in_specs=[pl.BlockSpec((tn, d), lambda i, j: (i, 0)), pl.BlockSpec((tn, te), lambda i, j: (i, j)),
                  pl.BlockSpec((te, d), lambda i, j: (j, 0)), pl.BlockSpec((te, d), lambda i, j: (j, 0)),
                  pl.BlockSpec((tn, d), lambda i, j: (i, 0)), _const_spec(g.shape)],
        out_specs=pl.BlockSpec((tn, d), lambda i, j: (i, 0)),
        out_shape=jax.ShapeDtypeStruct((n, d), F32),
        scratch_shapes=[pltpu.VMEM((tn, d), F32)],
        compiler_params=_params(("parallel", "arbitrary")),
        name="peer_apply",
    )(xn, a, u, v, h, g)


def _split_w_in(w_in, d):
    widths = (d, DSA_HEADS * DSA_LATENT, DSA_LATENT, IDX_HEADS * IDX_DIM, IDX_DIM, IDX_HEADS,
              MEM_HEADS * MEM_HEAD_DIM, N_BRANCH * d)
    offs = np.cumsum((0,) + widths)
    wu, wq, wc, wqi, wki, wwi, wmq, wg = [w_in[:, offs[i]:offs[i + 1]] for i in range(8)]
    wkw = jnp.pad(jnp.concatenate([wki, wwi], axis=1), ((0, 0), (0, LANES - IDX_DIM - IDX_HEADS)))
    kw_scale = jnp.concatenate([jnp.ones((IDX_DIM,), F32), jnp.full((IDX_HEADS,), IDX_HEADS ** -0.5, F32),
                                jnp.zeros((LANES - IDX_DIM - IDX_HEADS,), F32)])[None, :]
    wqi = jnp.pad(wqi.reshape(-1, IDX_HEADS, IDX_DIM), ((0, 0), (0, 0), (0, LANES - IDX_DIM))).reshape(-1, IDX_HEADS * LANES)
    ws = [wu, wq * (DSA_LATENT ** -0.5 * LOG2E), wc, wqi, wkw, wmq * MEM_HEAD_DIM ** -0.5, wg]
    return [w.astype(BF16) for w in ws], kw_scale


def _pad_rows(a, n):
    return a if a.shape[0] == n else jnp.pad(a, ((0, n - a.shape[0]),) + ((0, 0),) * (a.ndim - 1))


def _pad_keys(a, s_pad):
    return jnp.pad(a, ((0, 0), (0, s_pad - a.shape[1]), (0, 0)))


def _token_mix(h, seq_shape, lw, mats, state, cache, mem_kv, pos0):
    b, l = seq_shape
    bf = lambda a: a.astype(BF16)
    seq = lambda a: a.reshape(b, l, -1)
    u, q, c, c16, qi, kw, k16, mq, g = _inproj(h, lw["g_norm1"], lw["g_kv"], lw["kw_scale"], lw["w_in"])
    ki, wi = kw[:, :IDX_DIM], kw[:, IDX_DIM:IDX_DIM + IDX_HEADS]

    ys, s_re, s_im = _ssm_branch(seq(u), mats, state[0], state[1])

    c_all, k_all = seq(c16), seq(k16)
    if cache is not None:
        lane_pad = ((0, 0), (0, 0), (0, LANES - IDX_DIM))
        c_all = jnp.concatenate([bf(cache[0]), c_all], axis=1)
        k_all = jnp.concatenate([jnp.pad(bf(cache[1]), lane_pad), k_all], axis=1)
    s_all = c_all.shape[1]
    s_pad = -(-s_all // _DSA_SK) * _DSA_SK if cache is not None else s_all
    yd = _dsa(seq(q), seq(qi), seq(wi), _pad_keys(k_all, s_pad), _pad_keys(c_all, s_pad), lw["w_uv"], pos0, s_all,
              min(DSA_TOPK, s_all // 4), _pick_tile(max(l, LANES), _DSA_TQ))

    ym = _memattn(seq(mq), mem_kv[0], mem_kv[1])

    flat = lambda a: a.reshape(b * l, -1)
    h2 = _merge(h, flat(ys), flat(yd), flat(ym), g, lw["w_glu"], lw["b_glu"], lw["w_br_ssm"], lw["w_br_dsa"],
                lw["w_br_mem"], lw["w_out"])
    return h2, seq(c), seq(ki), s_re, s_im


def _channel_mix(h2, lw, g_final, final_norm):
    n = h2.shape[0]
    h2 = _pad_rows(h2, -(-n // LANES) * LANES)
    xn, i1, i2, gate = _peer_route(h2, lw["g_norm2"], lw["w_peer_q"], lw["peer_sub_k1"], lw["peer_sub_k2"])
    a = _peer_gates(i1, i2, gate)
    return _peer_apply(xn, a, lw["peer_u"], lw["peer_v"], h2, g_final, final_norm)[:n]


def kernel(x_prompt, x_sample, mem_prompt, cache_dsa_latent, cache_dsa_idx_k, state_ssm_re, state_ssm_im, cache_mem_k, cache_mem_v, g_norm1, w_in, g_kv, w_uv, ssm_lam_re, ssm_lam_im, ssm_log_dt, ssm_b_re, ssm_b_im, ssm_c_re, ssm_c_im, ssm_d, w_glu, b_glu, g_mem, w_mem_kv, w_br_ssm, w_br_dsa, w_br_mem, w_out, g_norm2, w_peer_q, peer_sub_k1, peer_sub_k2, peer_u, peer_v, g_final):
    depth = w_in.shape[0]
    bp, lp, d = x_prompt.shape
    bs, ls, _ = x_sample.shape
    past = cache_dsa_latent.shape[2]
    mem_w = MEM_HEADS * MEM_HEAD_DIM
    n_mem = mem_prompt.shape[1]
    heads = (n_mem, MEM_HEADS, MEM_HEAD_DIM)
    bf = lambda a: a.astype(BF16)
    row = lambda a: a[None, :]

    hp, hs = x_prompt.reshape(bp * lp, d), x_sample.reshape(bs * ls, d)
    outs = [[] for _ in range(10)]
    for l in range(depth):
        ws, kw_scale = _split_w_in(w_in[l], d)
        lw = dict(g_norm1=row(g_norm1[l]), g_kv=row(g_kv[l]), kw_scale=kw_scale, w_in=ws, w_uv=bf(w_uv[l]),
                  w_glu=bf(w_glu[l]), b_glu=row(b_glu[l]), w_br_ssm=bf(w_br_ssm[l]), w_br_dsa=bf(w_br_dsa[l]),
                  w_br_mem=bf(w_br_mem[l]), w_out=bf(w_out[l]), g_norm2=row(g_norm2[l]), w_peer_q=bf(w_peer_q[l]),
                  peer_sub_k1=bf(peer_sub_k1[l]), peer_sub_k2=bf(peer_sub_k2[l]), peer_u=bf(peer_u[l]),
                  peer_v=bf(peer_v[l]))
        mats = _ssm_matrices(ssm_lam_re[l], ssm_lam_im[l], ssm_log_dt[l], ssm_b_re[l], ssm_b_im[l],
                             ssm_c_re[l], ssm_c_im[l], ssm_d[l], SSM_T)
        kv = _memkv(mem_prompt.reshape(bp * n_mem, d), row(g_mem[l]), bf(w_mem_kv[l]))
        mk_p, mv_p = kv[:, :mem_w].reshape(bp, n_mem, mem_w), kv[:, mem_w:].reshape(bp, n_mem, mem_w)
        zeros = jnp.zeros((bp,) + state_ssm_re.shape[2:], F32)

        h2p, c_p, ki_p, sre_p, sim_p = _token_mix(hp, (bp, lp), lw, mats, (zeros, zeros), None, (bf(mk_p), bf(mv_p)), 0)
        mem_s = (bf(cache_mem_k[l].reshape(bs, n_mem, mem_w)), bf(cache_mem_v[l].reshape(bs, n_mem, mem_w)))
        h2s, c_s, ki_s, sre_s, sim_s = _token_mix(hs, (bs, ls), lw, mats, (state_ssm_re[l], state_ssm_im[l]),
                                                  (cache_dsa_latent[l], cache_dsa_idx_k[l]), mem_s, past)
        final = l == depth - 1
        hp = _channel_mix(h2p, lw, row(g_final), final)
        hs = _channel_mix(h2s, lw, row(g_final), final)
        for lst, val in zip(outs, (c_p, ki_p, sre_p, sim_p, mk_p.reshape((bp,) + heads), mv_p.reshape((bp,) + heads),
                                   c_s, ki_s, sre_s, sim_s)):
            lst.append(val)
    return (hp.reshape(bp, lp, d), hs.reshape(bs, ls, d)) + tuple(jnp.stack(o) for o in outs)
```

```python
import functools
import math

import jax
import jax.numpy as jnp
import numpy as np
from jax import lax
from jax.experimental import pallas as pl
from jax.experimental.pallas import tpu as pltpu

F32 = jnp.float32
BF16 = jnp.bfloat16
I32 = jnp.int32

EPS = 1e-6
CHUNK = 64
SSM_GROUP = 16
SSM_STATE = 64
SSM_T = 32
DSA_HEADS = 8
DSA_LATENT = 256
DSA_HEAD_DIM = 128
IDX_HEADS = 8
IDX_DIM = 64
DSA_TOPK = 256
MEM_HEADS = 4
MEM_HEAD_DIM = 256
PEER_HEADS = 8
PEER_NKEYS = 128
PEER_HALF = 128
PEER_TOPK = 16
N_BRANCH = 3

LANES = 128
VMEM_LIMIT = 56 * 1024 * 1024
ROWS_WIDE = 256
ROWS_NARROW = 512
INT_MIN = -2 ** 31
INT_MAX = 2 ** 31 - 1
NEG_BIG = -1e30
LOG2E = 1.4426950408889634


def _pick_tile(n, target):
    if n <= target:
        return n
    for t in range(target, 7, -1):
        if n % t == 0 and t % 8 == 0:
            return t
    return n


def _params(sem):
    return pltpu.CompilerParams(dimension_semantics=sem, vmem_limit_bytes=VMEM_LIMIT)


def _const_spec(shape):
    nd = len(shape)
    return pl.BlockSpec(shape, lambda *_: (0,) * nd, pipeline_mode=pl.Buffered(1))


def _rms(x, g):
    return x * lax.rsqrt(jnp.mean(x * x, axis=-1, keepdims=True) + EPS) * g


def _gelu(x):
    return 0.5 * x * (1.0 + jnp.tanh(0.7978845608028654 * (x + 0.044715 * x * x * x)))


def _sigmoid(x):
    return 1.0 / (1.0 + jnp.exp(-x))


def _dot_nt(a, b):
    return lax.dot_general(a, b, (((1,), (1,)), ((), ())), preferred_element_type=F32)


def _dot(a, b):
    return jnp.dot(a, b, preferred_element_type=F32)


def _inproj_kernel(x_ref, g1_ref, gkv_ref, kws_ref, wu_ref, wq_ref, wc_ref, wqi_ref, wkw_ref, wmq_ref, wg_ref,
                   u_ref, q_ref, c_ref, c16_ref, qi_ref, kw_ref, k16_ref, mq_ref, g_ref):
    xn = _rms(x_ref[...], g1_ref[...]).astype(BF16)
    u_ref[...] = _dot(xn, wu_ref[...]).astype(BF16)
    q_ref[...] = _dot(xn, wq_ref[...]).astype(BF16)
    c = _rms(_dot(xn, wc_ref[...]), gkv_ref[...])
    c_ref[...] = c
    c16_ref[...] = c.astype(BF16)
    qi_ref[...] = _dot(xn, wqi_ref[...]).astype(BF16)
    kw = _dot(xn, wkw_ref[...]) * kws_ref[...]
    kw_ref[...] = kw
    is_key = lax.broadcasted_iota(I32, kw.shape, 1) < IDX_DIM
    k16_ref[...] = jnp.where(is_key, kw, 0.0).astype(BF16)
    mq_ref[...] = _dot(xn, wmq_ref[...]).astype(BF16)
    g_ref[...] = _sigmoid(_dot(xn, wg_ref[...])).astype(BF16)


def _inproj(h, g1, gkv, kw_scale, ws):
    n, d = h.shape
    tm = _pick_tile(n, ROWS_WIDE)
    wu, wq, wc, wqi, wkw, wmq, wg = (w.shape[1] for w in ws)
    outs = [(wu, BF16), (wq, BF16), (wc, F32), (wc, BF16), (wqi, BF16), (wkw, F32), (wkw, BF16), (wmq, BF16), (wg, BF16)]
    row = lambda w: pl.BlockSpec((tm, w), lambda i: (i, 0))
    return pl.pallas_call(
        _inproj_kernel,
        grid=(n // tm,),
        in_specs=[row(d), _const_spec(g1.shape), _const_spec(gkv.shape), _const_spec(kw_scale.shape)]
        + [_const_spec(w.shape) for w in ws],
        out_specs=[row(w) for w, _ in outs],
        out_shape=[jax.ShapeDtypeStruct((n, w), dt) for w, dt in outs],
        compiler_params=_params(("parallel",)),
        name="inproj",
    )(h, g1, gkv, kw_scale, *ws)


def _memkv_kernel(x_ref, g_ref, w_ref, o_ref):
    xn = _rms(x_ref[...], g_ref[...]).astype(BF16)
    o_ref[...] = _dot(xn, w_ref[...])


def _memkv(mem, g, w):
    n, d = mem.shape
    tm = _pick_tile(n, ROWS_WIDE)
    return pl.pallas_call(
        _memkv_kernel,
        grid=(n // tm,),
        in_specs=[pl.BlockSpec((tm, d), lambda i: (i, 0)), _const_spec(g.shape), _const_spec(w.shape)],
        out_specs=pl.BlockSpec((tm, w.shape[1]), lambda i: (i, 0)),
        out_shape=jax.ShapeDtypeStruct((n, w.shape[1]), F32),
        compiler_params=_params(("parallel",)),
        name="memkv",
    )(mem, g, w)


def _memattn_kernel(q_ref, k_ref, v_ref, o_ref):
    for hd in range(MEM_HEADS):
        sl = slice(hd * MEM_HEAD_DIM, (hd + 1) * MEM_HEAD_DIM)
        logits = _dot_nt(q_ref[0, :, sl], k_ref[0, :, sl])
        m = jnp.max(logits, axis=-1, keepdims=True)
        p = jnp.exp(logits - m)
        l = jnp.sum(p, axis=-1, keepdims=True)
        o = _dot(p.astype(BF16), v_ref[0, :, sl]) / l
        o_ref[0, :, sl] = o.astype(BF16)


def _memattn(q, k, v):
    b, l, w = q.shape
    tl = _pick_tile(l, ROWS_NARROW)
    nm = k.shape[1]
    return pl.pallas_call(
        _memattn_kernel,
        grid=(b, l // tl),
        in_specs=[pl.BlockSpec((1, tl, w), lambda i, j: (i, j, 0)),
                  pl.BlockSpec((1, nm, w), lambda i, j: (i, 0, 0)),
                  pl.BlockSpec((1, nm, w), lambda i, j: (i, 0, 0))],
        out_specs=pl.BlockSpec((1, tl, w), lambda i, j: (i, j, 0)),
        out_shape=jax.ShapeDtypeStruct((b, l, w), BF16),
        compiler_params=_params(("parallel", "parallel")),
        name="memattn",
    )(q, k, v)


def _merge_kernel(h_ref, ys_ref, yd_ref, ym_ref, g_ref, wglu_ref, bglu_ref, wbs_ref, wbd_ref, wbm_ref, wout_ref,
                  o_ref):
    d = h_ref.shape[1]
    z = _gelu(ys_ref[...].astype(F32))
    gate = _sigmoid(_dot(z.astype(BF16), wglu_ref[...]) + bglu_ref[...])
    a = _dot((z * gate).astype(BF16), wbs_ref[...])
    b = _dot(yd_ref[...], wbd_ref[...])
    c = _dot(ym_ref[...], wbm_ref[...])
    g = g_ref[...].astype(F32)
    merged = g[:, 0:d] * a + g[:, d:2 * d] * b + g[:, 2 * d:3 * d] * c
    o_ref[...] = h_ref[...] + _dot(merged.astype(BF16), wout_ref[...])


def _merge(h, ys, yd, ym, g, wglu, bglu, wbs, wbd, wbm, wout):
    n, d = h.shape
    tm = _pick_tile(n, ROWS_NARROW)
    row = lambda w: pl.BlockSpec((tm, w), lambda i: (i, 0))
    consts = [wglu, bglu, wbs, wbd, wbm, wout]
    return pl.pallas_call(
        _merge_kernel,
        grid=(n // tm,),
        in_specs=[row(d), row(d), row(d), row(d), row(3 * d)] + [_const_spec(c.shape) for c in consts],
        out_specs=row(d),
        out_shape=jax.ShapeDtypeStruct((n, d), F32),
        compiler_params=_params(("parallel",)),
        name="merge",
    )(h, ys, yd, ym, g, *consts)


def _ssm_matrices(lam_re, lam_im, log_dt, b_re, b_im, c_re, c_im, d, t_len):
    hi = lax.Precision.HIGHEST
    g_n, p_n = lam_re.shape
    dt = jnp.exp(log_dt)[:, None]
    mag = jnp.exp(lam_re * dt)
    ar, ai = mag * jnp.cos(lam_im * dt), mag * jnp.sin(lam_im * dt)
    den = lam_re * lam_re + lam_im * lam_im
    nr, ni = ar - 1.0, ai
    kr = ((nr * lam_re + ni * lam_im) / den)[..., None]
    ki = ((ni * lam_re - nr * lam_im) / den)[..., None]
    bbr, bbi = kr * b_re - ki * b_im, kr * b_im + ki * b_re
    j = jnp.arange(t_len + 1, dtype=F32)[:, None, None]
    pmag = jnp.exp(j * (lam_re * dt))
    pr, pi = pmag * jnp.cos(j * (lam_im * dt)), pmag * jnp.sin(j * (lam_im * dt))
    mr = pr[:t_len, ..., None] * bbr - pi[:t_len, ..., None] * bbi
    mi = pr[:t_len, ..., None] * bbi + pi[:t_len, ..., None] * bbr
    kern = (jnp.einsum('gdp,jgpc->jgdc', c_re, mr, precision=hi)
            - jnp.einsum('gdp,jgpc->jgdc', c_im, mi, precision=hi))
    s_i = jnp.arange(t_len)[:, None]
    t_i = jnp.arange(t_len)[None, :]
    lag = t_i - s_i
    kg = jnp.where((lag >= 0)[:, :, None, None, None], kern[jnp.clip(lag, 0)], 0.0)
    tz = kg.transpose(2, 0, 4, 1, 3).reshape(g_n, t_len * SSM_GROUP, t_len * SSM_GROUP)
    pad = ((0, 0), (0, 0), (0, LANES - p_n))
    vr = jnp.pad(mr[::-1].transpose(1, 0, 3, 2).reshape(g_n, t_len * SSM_GROUP, p_n), pad)
    vi = jnp.pad(mi[::-1].transpose(1, 0, 3, 2).reshape(g_n, t_len * SSM_GROUP, p_n), pad)
    tzv = jnp.concatenate([tz, vr, vi], axis=-1).astype(BF16)
    pr1, pi1 = pr[1:].transpose(1, 2, 0), pi[1:].transpose(1, 2, 0)
    crt, cit = c_re.transpose(0, 2, 1), c_im.transpose(0, 2, 1)
    wre = crt[:, :, None, :] * pr1[..., None] - cit[:, :, None, :] * pi1[..., None]
    wim = -(crt[:, :, None, :] * pi1[..., None] + cit[:, :, None, :] * pr1[..., None])
    rpad = ((0, 0), (0, LANES - p_n), (0, 0))
    wre = jnp.pad(wre.reshape(g_n, p_n, -1), rpad).astype(BF16)
    wim = jnp.pad(wim.reshape(g_n, p_n, -1), rpad).astype(BF16)
    atr = jnp.pad(pr[t_len], ((0, 0), (0, LANES - p_n)))[:, None, :]
    ati = jnp.pad(pi[t_len], ((0, 0), (0, LANES - p_n)))[:, None, :]
    drow = jnp.tile(d.reshape(g_n, 1, SSM_GROUP), (1, t_len, 1)).reshape(g_n, 1, t_len * SSM_GROUP)
    return tzv, wre, wim, atr, ati, drow


def _ssm_kernel(nk, nb, u_ref, tzv_ref, wre_ref, wim_ref, atr_ref, ati_ref, d_ref, ire_ref, iim_ref,
                y_ref, fre_ref, fim_ref, yi_ref, sr_ref, si_ref, xr_ref, xi_ref):
    tc = u_ref.shape[2]
    u = u_ref[0]
    full = _dot(u, tzv_ref[0])
    yi_ref[...] = full[:, :tc]
    sr_ref[...] = full[:, tc:tc + LANES]
    si_ref[...] = full[:, tc + LANES:]
    atr, ati = atr_ref[0], ati_ref[0]

    def step(k, carry):
        xr, xi = carry
        rows = pl.ds(pl.multiple_of(k * nb, nb), nb)
        xr_ref[rows, :] = xr
        xi_ref[rows, :] = xi
        return (atr * xr - ati * xi + sr_ref[rows, :], atr * xi + ati * xr + si_ref[rows, :])

    xr, xi = lax.fori_loop(0, nk, step, (ire_ref[0], iim_ref[0]))
    fre_ref[0] = xr
    fim_ref[0] = xi
    y = (yi_ref[...] + _dot(xr_ref[...].astype(BF16), wre_ref[0]) + _dot(xi_ref[...].astype(BF16), wim_ref[0])
         + d_ref[0] * u.astype(F32))
    y_ref[0] = y.astype(BF16)


def _ssm(u, mats, init_re, init_im, nk, nb):
    tzv, wre, wim, atr, ati, drow = mats
    g_n, r, tc = u.shape
    per_g = lambda a: pl.BlockSpec((1,) + a.shape[1:], lambda g: (g, 0, 0))
    ins = [u, tzv, wre, wim, atr, ati, drow, init_re, init_im]
    st = jax.ShapeDtypeStruct((g_n, nb, LANES), F32)
    return pl.pallas_call(
        functools.partial(_ssm_kernel, nk, nb),
        grid=(g_n,),
        in_specs=[per_g(a) for a in ins],
        out_specs=[per_g(u), per_g(init_re), per_g(init_re)],
        out_shape=[jax.ShapeDtypeStruct(u.shape, BF16), st, st],
        scratch_shapes=[pltpu.VMEM((r, tc), F32)] + [pltpu.VMEM((r, LANES), F32)] * 4,
        compiler_params=_params(("parallel",)),
        name="ssm",
    )(*ins)


def _ssm_branch(zu, mats, st_re, st_im):
    b, l, w = zu.shape
    g_n, p_n = st_re.shape[1], st_re.shape[2]
    nk = l // SSM_T
    ug = zu.reshape(b, nk, SSM_T, g_n, SSM_GROUP).transpose(3, 1, 0, 2, 4).reshape(g_n, nk * b, SSM_T * SSM_GROUP)
    pad = ((0, 0), (0, 0), (0, LANES - p_n))
    ire = jnp.pad(st_re.transpose(1, 0, 2), pad)
    iim = jnp.pad(st_im.transpose(1, 0, 2), pad)
    y, fre, fim = _ssm(ug, mats, ire, iim, nk, b)
    y = y.reshape(g_n, nk, b, SSM_T, SSM_GROUP).transpose(2, 1, 3, 0, 4).reshape(b, l, w)
    return y, fre[:, :, :p_n].transpose(1, 0, 2), fim[:, :, :p_n].transpose(1, 0, 2)


_DSA_TQ = 256
_DSA_VALUE_STEPS = 24
_DSA_SK = 1024
_DSA_MIN_SUM = 2.0 ** -80


def _dsa_kernel(tq, tv, sk, pos0, s_valid, n_sel, qlat_ref, qidx_ref, w_ref, ka_ref, c_ref, cmax_ref, wuv_ref,
                o_ref, key_ref, bias_ref, j_ref):
    s_pad = key_ref.shape[0]
    q0 = pos0 + pl.program_id(1) * tq
    qpos = q0 + lax.broadcasted_iota(I32, (1, tq), 1)
    vis = jnp.minimum((qpos // CHUNK + 1) * CHUNK, s_valid)
    vis_max = jnp.minimum(((q0 + tq - 1) // CHUNK + 1) * CHUNK, s_valid)
    nch = (vis_max + sk - 1) // sk
    kpos = lax.broadcasted_iota(I32, (sk, tq), 0)
    kslice = lambda j: pl.ds(pl.multiple_of(j * sk, sk), sk)
    fold = lambda a: a.reshape(sk // 8, 8, tq)

    qs = jnp.concatenate([qidx_ref[0, :, hd * LANES:(hd + 1) * LANES] for hd in range(IDX_HEADS)], axis=0)

    def score_chunk(j, carry):
        kmin, kmax = carry
        rel = jnp.maximum(_dot_nt(ka_ref[0, kslice(j), :], qs), 0.0)
        acc = jnp.zeros((sk, tq), F32)
        for hd in range(IDX_HEADS):
            acc = acc + w_ref[0, hd:hd + 1, :] * rel[:, hd * tq:(hd + 1) * tq]
        bits = pltpu.bitcast(acc, I32)
        key = bits ^ ((bits >> 31) & 0x7FFFFFFF)
        visible = j * sk + kpos < vis
        key_ref[kslice(j), :] = jnp.where(visible, key, INT_MIN)
        kmin = jnp.minimum(kmin, jnp.min(fold(jnp.where(visible, key, INT_MAX)), axis=0))
        kmax = jnp.maximum(kmax, jnp.max(fold(jnp.where(visible, key, INT_MIN)), axis=0))
        return kmin, kmax

    kmin, kmax = lax.fori_loop(0, nch, score_chunk,
                               (jnp.full((8, tq), INT_MAX, I32), jnp.full((8, tq), INT_MIN, I32)))
    kmin = jnp.min(kmin, axis=0, keepdims=True)
    kmax = jnp.max(kmax, axis=0, keepdims=True)

    def count(pred):
        def body(j, c):
            hit = jnp.where(pred(key_ref[kslice(j), :], j * sk + kpos), 1.0, 0.0)
            return c + jnp.sum(fold(hit), axis=0)
        return jnp.sum(lax.fori_loop(0, nch, body, jnp.zeros((8, tq), F32)), axis=0, keepdims=True)

    key_of = lambda v: (lambda b: b ^ ((b >> 31) & 0x7FFFFFFF))(pltpu.bitcast(v, I32))
    val_of = lambda k: pltpu.bitcast(k ^ ((k >> 31) & 0x7FFFFFFF), F32)

    def settled(lo, hi, cnt):
        return (cnt <= n_sel) | (hi - 1 <= lo)

    def halve(it, lo, hi, cnt):
        mid_v = key_of(0.5 * val_of(lo) + 0.5 * val_of(hi))
        mid_k = (lo >> 1) + (hi >> 1) + (lo & hi & 1)
        mid = jnp.where((mid_v > lo) & (mid_v < hi) & (it < _DSA_VALUE_STEPS), mid_v, mid_k)
        c = count(lambda k, col: k >= mid)
        live = jnp.logical_not(settled(lo, hi, cnt))
        up = live & (c >= n_sel)
        return jnp.where(up, mid, lo), jnp.where(live & (c < n_sel), mid, hi), jnp.where(up, c, cnt)

    def halve_twice(state):
        it, lo, hi, cnt, _ = state
        lo, hi, cnt = halve(it, lo, hi, cnt)
        lo, hi, cnt = halve(it + 1, lo, hi, cnt)
        return it + 2, lo, hi, cnt, jnp.max(jnp.where(settled(lo, hi, cnt), 0, 1))

    c_pos, c_nn = count(lambda k, col: k >= 1), count(lambda k, col: k >= 0)
    pos, zero = c_pos >= n_sel, c_nn >= n_sel
    lo0 = jnp.where(pos, 1, jnp.where(zero, 0, kmin))
    hi0 = jnp.where(pos, jnp.where(kmax == INT_MAX, INT_MAX, kmax + 1), jnp.where(zero, 1, 0))
    cnt0 = jnp.where(pos, c_pos, jnp.where(zero, c_nn, vis.astype(F32)))
    state = (jnp.int32(0), lo0, hi0, cnt0, jnp.max(jnp.where(settled(lo0, hi0, cnt0), 0, 1)))
    _, thr, _, cnt, _ = lax.while_loop(lambda st: (st[0] < _DSA_VALUE_STEPS + 34) & (st[4] > 0), halve_twice, state)
    thr = jnp.maximum(thr, INT_MIN + 1)
    excess = cnt > n_sel

    j_ref[...] = jnp.full(j_ref.shape, s_pad, I32)

    @pl.when(jnp.max(jnp.where(excess, 1.0, 0.0)) > 0.0)
    def _():
        need = n_sel - count(lambda k, col: k > thr)

        def idx_step(_, lohi):
            lo, hi = lohi
            mid = (lo + hi) >> 1
            ok = count(lambda k, col: (k == thr) & (col < mid)) >= need
            return jnp.where(ok, lo, mid), jnp.where(ok, mid, hi)

        steps = int(math.ceil(math.log2(s_pad))) + 1
        _, hi = lax.fori_loop(0, steps, idx_step, (jnp.zeros((1, tq), I32), jnp.full((1, tq), s_pad, I32)))
        j_ref[...] = jnp.broadcast_to(jnp.where(excess, hi, s_pad), j_ref.shape)

    jlim = j_ref[0:1, :]

    rows = DSA_HEADS * tv
    q = jnp.concatenate([qlat_ref[0, :tv, hd * DSA_LATENT:(hd + 1) * DSA_LATENT] for hd in range(DSA_HEADS)], axis=0)

    wb = min(sk, 2 * LANES)

    kpos_wb = lax.broadcasted_iota(I32, (wb, tq), 0)

    def logits(j, i, first):
        blk = pl.ds(pl.multiple_of(j * sk + i * wb, wb), wb)
        if first:
            k = key_ref[blk, :]
            sel = (k > thr) | ((k == thr) & (j * sk + i * wb + kpos_wb < jlim))
            bias = jnp.where(sel, 0.0, NEG_BIG).T[:tv]
            bias_ref[:, blk] = bias
        else:
            bias = bias_ref[:, blk]
        s = _dot_nt(q, c_ref[0, blk, :])
        return (s.reshape(DSA_HEADS, tv, wb) + bias[None]).reshape(rows, wb)

    def attend(m, first):
        def acc_chunk(j, carry):
            l, acc = carry
            ps = []
            for i in range(sk // wb):
                s = logits(j, i, first)
                for t in range(wb // LANES):
                    p = jnp.exp2(s[:, t * LANES:(t + 1) * LANES] - m)
                    l = l + p
                    ps.append(p.astype(BF16))
            return l, acc + _dot(jnp.concatenate(ps, axis=1), c_ref[0, kslice(j), :])

        l, acc = lax.fori_loop(0, nch, acc_chunk, (jnp.zeros((rows, LANES), F32), jnp.zeros((rows, DSA_LATENT), F32)))
        return jnp.sum(l, axis=1, keepdims=True), acc

    def emit(l, acc):
        o = (acc / l).astype(BF16)
        for hd in range(DSA_HEADS):
            o_ref[0, :tv, hd * DSA_HEAD_DIM:(hd + 1) * DSA_HEAD_DIM] = _dot(o[hd * tv:(hd + 1) * tv], wuv_ref[hd]).astype(BF16)
        if tv < tq:
            o_ref[0, tv:, :] = jnp.zeros((tq - tv, o_ref.shape[2]), BF16)

    qf = q.astype(F32)
    bound = jnp.sqrt(jnp.sum(qf * qf, axis=1, keepdims=True)) * cmax_ref[0, 0:1, 0:1]
    l, acc = attend(jnp.broadcast_to(bound, (rows, LANES)), True)
    healthy = jnp.min(l) > _DSA_MIN_SUM

    @pl.when(healthy)
    def _():
        emit(l, acc)

    @pl.when(jnp.logical_not(healthy))
    def _():
        def max_chunk(j, mx):
            for i in range(sk // wb):
                s = logits(j, i, False)
                for t in range(wb // LANES):
                    mx = jnp.maximum(mx, s[:, t * LANES:(t + 1) * LANES])
            return mx

        mx = lax.fori_loop(0, nch, max_chunk, jnp.full((rows, LANES), NEG_BIG, F32))
        emit(*attend(jnp.broadcast_to(jnp.max(mx, axis=1, keepdims=True), (rows, LANES)), False))


def _dsa(qlat, qidx, w, ka, c, wuv, pos0, s_valid, n_sel, tq):
    b, l_true, _ = qlat.shape
    l = -(-l_true // tq) * tq
    qlat, qidx, w = (jnp.pad(a, ((0, 0), (0, l - l_true), (0, 0))) for a in (qlat, qidx, w))
    wt = w.transpose(0, 2, 1)
    s_pad = c.shape[1]
    sk = min(_DSA_SK, s_pad)
    cf = c.astype(F32)
    cmax = jnp.broadcast_to(jnp.sqrt(jnp.max(jnp.sum(cf * cf, axis=2), axis=1))[:, None, None], (b, 1, LANES))
    qspec = lambda a: pl.BlockSpec((1, tq, a.shape[2]), lambda i, j: (i, j, 0))
    kspec = lambda a: pl.BlockSpec((1,) + a.shape[1:], lambda i, j: (i, 0, 0))
    dh = wuv.shape[0] * wuv.shape[2]
    return pl.pallas_call(
        functools.partial(_dsa_kernel, tq, min(tq, l_true), sk, pos0, s_valid, n_sel),
        grid=(b, l // tq),
        in_specs=[qspec(qlat), qspec(qidx), pl.BlockSpec((1, wt.shape[1], tq), lambda i, j: (i, 0, j)), kspec(ka), kspec(c),
                  kspec(cmax), _const_spec(wuv.shape)],
        out_specs=pl.BlockSpec((1, tq, dh), lambda i, j: (i, j, 0)),
        out_shape=jax.ShapeDtypeStruct((b, l, dh), BF16),
        scratch_shapes=[pltpu.VMEM((s_pad, tq), I32), pltpu.VMEM((min(tq, l_true), s_pad), F32), pltpu.VMEM((8, tq), I32)],
        compiler_params=_params(("parallel", "arbitrary")),
        name="dsa",
    )(qlat, qidx, wt, ka, c, cmax, wuv)[:, :l_true]


_PEER_WIDTH = [PEER_TOPK // (r + 1) for r in range(PEER_TOPK)]


def _top_rows(s, k):
    row = lax.broadcasted_iota(I32, s.shape, 0)
    vals, idxs = [], []
    for _ in range(k):
        m = jnp.max(s, axis=0, keepdims=True)
        first = jnp.min(jnp.where(s == m, row, s.shape[0]), axis=0, keepdims=True)
        s = jnp.where(row == first, -jnp.inf, s)
        vals.append(m)
        idxs.append(first)
    return jnp.concatenate(vals, axis=0), jnp.concatenate(idxs, axis=0)


def _peer_route_kernel(h_ref, g_ref, wq_ref, k1_ref, k2_ref, xn_ref, i1_ref, i2_ref, gate_ref):
    tn = h_ref.shape[0]
    xn = _rms(h_ref[...], g_ref[...]).astype(BF16)
    xn_ref[...] = xn
    q = _dot(xn, wq_ref[...]).astype(BF16)
    oute, outg = [], []
    for hd in range(PEER_HEADS):
        qa = q[:, (2 * hd) * PEER_HALF:(2 * hd + 1) * PEER_HALF]
        qb = q[:, (2 * hd + 1) * PEER_HALF:(2 * hd + 2) * PEER_HALF]
        v1, i1 = _top_rows(_dot_nt(k1_ref[hd], qa), PEER_TOPK)
        v2, i2 = _top_rows(_dot_nt(k2_ref[hd], qb), PEER_TOPK)
        cand = jnp.concatenate([v1[r:r + 1] + v2[0:w] for r, w in enumerate(_PEER_WIDTH)], axis=0)
        cexp = jnp.concatenate([i1[r:r + 1] * PEER_NKEYS + i2[0:w] for r, w in enumerate(_PEER_WIDTH)], axis=0)
        row = lax.broadcasted_iota(I32, cand.shape, 0)
        tops, experts = [], []
        for _ in range(PEER_TOPK):
            m = jnp.max(cand, axis=0, keepdims=True)
            first = jnp.min(jnp.where(cand == m, row, cand.shape[0]), axis=0, keepdims=True)
            hit = row == first
            experts.append(jnp.sum(jnp.where(hit, cexp, 0), axis=0, keepdims=True))
            cand = jnp.where(hit, -jnp.inf, cand)
            tops.append(m)
        top = jnp.concatenate(tops, axis=0)
        ex = jnp.exp(top - top[0:1])
        outg.append(ex / jnp.sum(ex, axis=0, keepdims=True))
        oute.append(jnp.concatenate(experts, axis=0))
    expert = jnp.concatenate(oute, axis=0)
    as_rows = lambda a: pltpu.bitcast(pltpu.bitcast(a, F32).T, I32)
    i1_ref[...] = as_rows(expert >> 7)
    i2_ref[...] = as_rows(expert & (PEER_NKEYS - 1))
    gate_ref[...] = jnp.concatenate(outg, axis=0).T


def _peer_route(h, g, wq, k1, k2):
    n, d = h.shape
    tn = LANES
    slots = PEER_HEADS * PEER_TOPK
    row = lambda w: pl.BlockSpec((tn, w), lambda i: (i, 0))
    return pl.pallas_call(
        _peer_route_kernel,
        grid=(n // tn,),
        in_specs=[row(d), _const_spec(g.shape), _const_spec(wq.shape), _const_spec(k1.shape), _const_spec(k2.shape)],
        out_specs=[row(d), row(slots), row(slots), row(slots)],
        out_shape=[jax.ShapeDtypeStruct((n, d), BF16), jax.ShapeDtypeStruct((n, slots), I32),
                   jax.ShapeDtypeStruct((n, slots), I32), jax.ShapeDtypeStruct((n, slots), F32)],
        compiler_params=_params(("parallel",)),
        name="peer_route",
    )(h, g, wq, k1, k2)


_PLANE_PAD = 4


def _peer_gates_kernel(i1_ref, i2_ref, gate_ref, a_ref, planes_ref):
    tn, slots = i1_ref.shape
    plane = tn + _PLANE_PAD
    sub = lax.broadcasted_iota(I32, (PEER_NKEYS, slots), 0)

    def token(n, carry):
        r = pl.ds(n, 1)
        pt = jnp.where(sub == i1_ref[r, :], gate_ref[r, :], 0.0).astype(BF16)
        qt = jnp.where(sub == i2_ref[r, :], 1.0, 0.0).astype(BF16)
        planes_ref[pl.ds(n, PEER_NKEYS, stride=plane), :] = _dot_nt(pt, qt)
        return carry

    lax.fori_loop(0, tn, token, 0, unroll=True)
    for k in range(PEER_NKEYS):
        a_ref[:, k * PEER_NKEYS:(k + 1) * PEER_NKEYS] = planes_ref[pl.ds(k * plane, tn), :].astype(BF16)


def _peer_gates(i1, i2, gate):
    n, slots = i1.shape
    tn = _pick_tile(n, LANES)
    ne = PEER_NKEYS * PEER_NKEYS
    row = lambda w: pl.BlockSpec((tn, w), lambda i: (i, 0))
    return pl.pallas_call(
        _peer_gates_kernel,
        grid=(n // tn,),
        in_specs=[row(slots)] * 3,
        out_specs=row(ne),
        out_shape=jax.ShapeDtypeStruct((n, ne), BF16),
        scratch_shapes=[pltpu.VMEM((PEER_NKEYS * (tn + _PLANE_PAD), PEER_NKEYS), F32)],
        compiler_params=_params(("parallel",)),
        name="peer_gates",
    )(i1, i2, gate)


_PEER_SUB = 256
def _peer_apply_kernel(final_norm, xn_ref, a_ref, u_ref, v_ref, h_ref, g_ref, o_ref, acc_ref):
    j = pl.program_id(1)

    @pl.when(j == 0)
    def _():
        acc_ref[...] = jnp.zeros_like(acc_ref)

    xn = xn_ref[...]
    ws = []
    for s in range(u_ref.shape[0] // _PEER_SUB):
        sub = slice(s * _PEER_SUB, (s + 1) * _PEER_SUB)
        act = _gelu(_dot_nt(xn, u_ref[sub, :]))
        ws.append((act * a_ref[:, sub].astype(F32)).astype(BF16))
    acc_ref[...] += _dot(jnp.concatenate(ws, axis=1), v_ref[...])

    @pl.when(j == pl.num_programs(1) - 1)
    def _():
        y = h_ref[...] + acc_ref[...]
        o_ref[...] = _rms(y, g_ref[...]) if final_norm else y


def _peer_apply(xn, a, u, v, h, g, final_norm):
    n, d = xn.shape
    ne = u.shape[0]
    tn = _pick_tile(n, ROWS_NARROW)
    te = 8 * _PEER_SUB
    return pl.pallas_call(
        functools.partial(_peer_apply_kernel, final_norm),
        grid=(n // tn, ne // te),
        in_specs=[pl.BlockSpec((tn, d), lambda i, j: (i, 0)), pl.BlockSpec((tn, te), lambda i, j: (i, j)),
                  pl.BlockSpec((te, d), lambda i, j: (j, 0)), pl.BlockSpec((te, d), lambda i, j: (j, 0)),
                  pl.BlockSpec((tn, d), lambda i, j: (i, 0)), _const_spec(g.shape)],
        out_specs=pl.BlockSpec((tn, d), lambda i, j: (i, 0)),
        out_shape=jax.ShapeDtypeStruct((n, d), F32),
        scratch_shapes=[pltpu.VMEM((tn, d), F32)],
        compiler_params=_params(("parallel", "arbitrary")),
        name="peer_apply",
    )(xn, a, u, v, h, g)


def _split_w_in(w_in, d):
    widths = (d, DSA_HEADS * DSA_LATENT, DSA_LATENT, IDX_HEADS * IDX_DIM, IDX_DIM, IDX_HEADS,
              MEM_HEADS * MEM_HEAD_DIM, N_BRANCH * d)
    offs = np.cumsum((0,) + widths)
    wu, wq, wc, wqi, wki, wwi, wmq, wg = [w_in[:, offs[i]:offs[i + 1]] for i in range(8)]
    wkw = jnp.pad(jnp.concatenate([wki, wwi], axis=1), ((0, 0), (0, LANES - IDX_DIM - IDX_HEADS)))
    kw_scale = jnp.concatenate([jnp.ones((IDX_DIM,), F32), jnp.full((IDX_HEADS,), IDX_HEADS ** -0.5, F32),
                                jnp.zeros((LANES - IDX_DIM - IDX_HEADS,), F32)])[None, :]
    wqi = jnp.pad(wqi.reshape(-1, IDX_HEADS, IDX_DIM), ((0, 0), (0, 0), (0, LANES - IDX_DIM))).reshape(-1, IDX_HEADS * LANES)
    ws = [wu, wq * (DSA_LATENT ** -0.5 * LOG2E), wc, wqi, wkw, wmq * MEM_HEAD_DIM ** -0.5, wg]
    return [w.astype(BF16) for w in ws], kw_scale


def _pad_rows(a, n):
    return a if a.shape[0] == n else jnp.pad(a, ((0, n - a.shape[0]),) + ((0, 0),) * (a.ndim - 1))


def _pad_keys(a, s_pad):
    return jnp.pad(a, ((0, 0), (0, s_pad - a.shape[1]), (0, 0)))


def _token_mix(h, seq_shape, lw, mats, state, cache, mem_kv, pos0):
    b, l = seq_shape
    bf = lambda a: a.astype(BF16)
    seq = lambda a: a.reshape(b, l, -1)
    u, q, c, c16, qi, kw, k16, mq, g = _inproj(h, lw["g_norm1"], lw["g_kv"], lw["kw_scale"], lw["w_in"])
    ki, wi = kw[:, :IDX_DIM], kw[:, IDX_DIM:IDX_DIM + IDX_HEADS]

    ys, s_re, s_im = _ssm_branch(seq(u), mats, state[0], state[1])

    c_all, k_all = seq(c16), seq(k16)
    if cache is not None:
        lane_pad = ((0, 0), (0, 0), (0, LANES - IDX_DIM))
        c_all = jnp.concatenate([bf(cache[0]), c_all], axis=1)
        k_all = jnp.concatenate([jnp.pad(bf(cache[1]), lane_pad), k_all], axis=1)
    s_all = c_all.shape[1]
    s_pad = -(-s_all // _DSA_SK) * _DSA_SK if cache is not None else s_all
    yd = _dsa(seq(q), seq(qi), seq(wi), _pad_keys(k_all, s_pad), _pad_keys(c_all, s_pad), lw["w_uv"], pos0, s_all,
              min(DSA_TOPK, s_all // 4), _pick_tile(max(l, LANES), _DSA_TQ))

    ym = _memattn(seq(mq), mem_kv[0], mem_kv[1])

    flat = lambda a: a.reshape(b * l, -1)
    h2 = _merge(h, flat(ys), flat(yd), flat(ym), g, lw["w_glu"], lw["b_glu"], lw["w_br_ssm"], lw["w_br_dsa"],
                lw["w_br_mem"], lw["w_out"])
    return h2, seq(c), seq(ki), s_re, s_im


def _channel_mix(h2, lw, g_final, final_norm):
    n = h2.shape[0]
    h2 = _pad_rows(h2, -(-n // LANES) * LANES)
    xn, i1, i2, gate = _peer_route(h2, lw["g_norm2"], lw["w_peer_q"], lw["peer_sub_k1"], lw["peer_sub_k2"])
    a = _peer_gates(i1, i2, gate)
    return _peer_apply(xn, a, lw["peer_u"], lw["peer_v"], h2, g_final, final_norm)[:n]


def kernel(x_prompt, x_sample, mem_prompt, cache_dsa_latent, cache_dsa_idx_k, state_ssm_re, state_ssm_im, cache_mem_k, cache_mem_v, g_norm1, w_in, g_kv, w_uv, ssm_lam_re, ssm_lam_im, ssm_log_dt, ssm_b_re, ssm_b_im, ssm_c_re, ssm_c_im, ssm_d, w_glu, b_glu, g_mem, w_mem_kv, w_br_ssm, w_br_dsa, w_br_mem, w_out, g_norm2, w_peer_q, peer_sub_k1, peer_sub_k2, peer_u, peer_v, g_final):
    depth = w_in.shape[0]
    bp, lp, d = x_prompt.shape
    bs, ls, _ = x_sample.shape
    past = cache_dsa_latent.shape[2]
    mem_w = MEM_HEADS * MEM_HEAD_DIM
    n_mem = mem_prompt.shape[1]
    heads = (n_mem, MEM_HEADS, MEM_HEAD_DIM)
    bf = lambda a: a.astype(BF16)
    row = lambda a: a[None, :]

    hp, hs = x_prompt.reshape(bp * lp, d), x_sample.reshape(bs * ls, d)
    outs = [[] for _ in range(10)]
    for l in range(depth):
        ws, kw_scale = _split_w_in(w_in[l], d)
        lw = dict(g_norm1=row(g_norm1[l]), g_kv=row(g_kv[l]), kw_scale=kw_scale, w_in=ws, w_uv=bf(w_uv[l]),
                  w_glu=bf(w_glu[l]), b_glu=row(b_glu[l]), w_br_ssm=bf(w_br_ssm[l]), w_br_dsa=bf(w_br_dsa[l]),
                  w_br_mem=bf(w_br_mem[l]), w_out=bf(w_out[l]), g_norm2=row(g_norm2[l]), w_peer_q=bf(w_peer_q[l]),
                  peer_sub_k1=bf(peer_sub_k1[l]), peer_sub_k2=bf(peer_sub_k2[l]), peer_u=bf(peer_u[l]),
                  peer_v=bf(peer_v[l]))
        mats = _ssm_matrices(ssm_lam_re[l], ssm_lam_im[l], ssm_log_dt[l], ssm_b_re[l], ssm_b_im[l],
                             ssm_c_re[l], ssm_c_im[l], ssm_d[l], SSM_T)
        kv = _memkv(mem_prompt.reshape(bp * n_mem, d), row(g_mem[l]), bf(w_mem_kv[l]))
        mk_p, mv_p = kv[:, :mem_w].reshape(bp, n_mem, mem_w), kv[:, mem_w:].reshape(bp, n_mem, mem_w)
        zeros = jnp.zeros((bp,) + state_ssm_re.shape[2:], F32)

        h2p, c_p, ki_p, sre_p, sim_p = _token_mix(hp, (bp, lp), lw, mats, (zeros, zeros), None, (bf(mk_p), bf(mv_p)), 0)
        mem_s = (bf(cache_mem_k[l].reshape(bs, n_mem, mem_w)), bf(cache_mem_v[l].reshape(bs, n_mem, mem_w)))
        h2s, c_s, ki_s, sre_s, sim_s = _token_mix(hs, (bs, ls), lw, mats, (state_ssm_re[l], state_ssm_im[l]),
                                                  (cache_dsa_latent[l], cache_dsa_idx_k[l]), mem_s, past)
        final = l == depth - 1
        hp = _channel_mix(h2p, lw, row(g_final), final)
        hs = _channel_mix(h2s, lw, row(g_final), final)
        for lst, val in zip(outs, (c_p, ki_p, sre_p, sim_p, mk_p.reshape((bp,) + heads), mv_p.reshape((bp,) + heads),
                                   c_s, ki_s, sre_s, sim_s)):
            lst.append(val)
    return (hp.reshape(bp, lp, d), hs.reshape(bs, ls, d)) + tuple(jnp.stack(o) for o in outs)
```

```python
import functools
import math

import jax
import jax.numpy as jnp
import numpy as np
from jax import lax
from jax.experimental import pallas as pl
from jax.experimental.pallas import tpu as pltpu

F32 = jnp.float32
BF16 = jnp.bfloat16
I32 = jnp.int32

EPS = 1e-6
CHUNK = 64
SSM_GROUP = 16
SSM_STATE = 64
SSM_T = 32
DSA_HEADS = 8
DSA_LATENT = 256
DSA_HEAD_DIM = 128
IDX_HEADS = 8
IDX_DIM = 64
DSA_TOPK = 256
MEM_HEADS = 4
MEM_HEAD_DIM = 256
PEER_HEADS = 8
PEER_NKEYS = 128
PEER_HALF = 128
PEER_TOPK = 16
N_BRANCH = 3

LANES = 128
VMEM_LIMIT = 56 * 1024 * 1024
ROWS_WIDE = 256
ROWS_NARROW = 512
INT_MIN = -2 ** 31
INT_MAX = 2 ** 31 - 1
NEG_BIG = -1e30
LOG2E = 1.4426950408889634


def _pick_tile(n, target):
    if n <= target:
        return n
    for t in range(target, 7, -1):
        if n % t == 0 and t % 8 == 0:
            return t
    return n


def _params(sem):
    return pltpu.CompilerParams(dimension_semantics=sem, vmem_limit_bytes=VMEM_LIMIT)


def _const_spec(shape):
    nd = len(shape)
    return pl.BlockSpec(shape, lambda *_: (0,) * nd, pipeline_mode=pl.Buffered(1))


def _rms(x, g):
    return x * lax.rsqrt(jnp.mean(x * x, axis=-1, keepdims=True) + EPS) * g


def _gelu(x):
    return 0.5 * x * (1.0 + jnp.tanh(0.7978845608028654 * (x + 0.044715 * x * x * x)))


def _sigmoid(x):
    return 1.0 / (1.0 + jnp.exp(-x))


def _dot_nt(a, b):
    return lax.dot_general(a, b, (((1,), (1,)), ((), ())), preferred_element_type=F32)


def _dot(a, b):
    return jnp.dot(a, b, preferred_element_type=F32)


def _inproj_kernel(x_ref, g1_ref, gkv_ref, kws_ref, wu_ref, wq_ref, wc_ref, wqi_ref, wkw_ref, wmq_ref, wg_ref,
                   u_ref, q_ref, c_ref, c16_ref, qi_ref, kw_ref, k16_ref, mq_ref, g_ref):
    xn = _rms(x_ref[...], g1_ref[...]).astype(BF16)
    u_ref[...] = _dot(xn, wu_ref[...]).astype(BF16)
    q_ref[...] = _dot(xn, wq_ref[...]).astype(BF16)
    c = _rms(_dot(xn, wc_ref[...]), gkv_ref[...])
    c_ref[...] = c
    c16_ref[...] = c.astype(BF16)
    qi_ref[...] = _dot(xn, wqi_ref[...]).astype(BF16)
    kw = _dot(xn, wkw_ref[...]) * kws_ref[...]
    kw_ref[...] = kw
    is_key = lax.broadcasted_iota(I32, kw.shape, 1) < IDX_DIM
    k16_ref[...] = jnp.where(is_key, kw, 0.0).astype(BF16)
    mq_ref[...] = _dot(xn, wmq_ref[...]).astype(BF16)
    g_ref[...] = _sigmoid(_dot(xn, wg_ref[...])).astype(BF16)


def _inproj(h, g1, gkv, kw_scale, ws):
    n, d = h.shape
    tm = _pick_tile(n, ROWS_WIDE)
    wu, wq, wc, wqi, wkw, wmq, wg = (w.shape[1] for w in ws)
    outs = [(wu, BF16), (wq, BF16), (wc, F32), (wc, BF16), (wqi, BF16), (wkw, F32), (wkw, BF16), (wmq, BF16), (wg, BF16)]
    row = lambda w: pl.BlockSpec((tm, w), lambda i: (i, 0))
    return pl.pallas_call(
        _inproj_kernel,
        grid=(n // tm,),
        in_specs=[row(d), _const_spec(g1.shape), _const_spec(gkv.shape), _const_spec(kw_scale.shape)]
        + [_const_spec(w.shape) for w in ws],
        out_specs=[row(w) for w, _ in outs],
        out_shape=[jax.ShapeDtypeStruct((n, w), dt) for w, dt in outs],
        compiler_params=_params(("parallel",)),
        name="inproj",
    )(h, g1, gkv, kw_scale, *ws)


def _memkv_kernel(x_ref, g_ref, w_ref, o_ref):
    xn = _rms(x_ref[...], g_ref[...]).astype(BF16)
    o_ref[...] = _dot(xn, w_ref[...])


def _memkv(mem, g, w):
    n, d = mem.shape
    tm = _pick_tile(n, ROWS_WIDE)
    return pl.pallas_call(
        _memkv_kernel,
        grid=(n // tm,),
        in_specs=[pl.BlockSpec((tm, d), lambda i: (i, 0)), _const_spec(g.shape), _const_spec(w.shape)],
        out_specs=pl.BlockSpec((tm, w.shape[1]), lambda i: (i, 0)),
        out_shape=jax.ShapeDtypeStruct((n, w.shape[1]), F32),
        compiler_params=_params(("parallel",)),
        name="memkv",
    )(mem, g, w)


def _memattn_kernel(q_ref, k_ref, v_ref, o_ref):
    for hd in range(MEM_HEADS):
        sl = slice(hd * MEM_HEAD_DIM, (hd + 1) * MEM_HEAD_DIM)
        logits = _dot_nt(q_ref[0, :, sl], k_ref[0, :, sl])
        m = jnp.max(logits, axis=-1, keepdims=True)
        p = jnp.exp(logits - m)
        l = jnp.sum(p, axis=-1, keepdims=True)
        o = _dot(p.astype(BF16), v_ref[0, :, sl]) / l
        o_ref[0, :, sl] = o.astype(BF16)


def _memattn(q, k, v):
    b, l, w = q.shape
    tl = _pick_tile(l, ROWS_NARROW)
    nm = k.shape[1]
    return pl.pallas_call(
        _memattn_kernel,
        grid=(b, l // tl),
        in_specs=[pl.BlockSpec((1, tl, w), lambda i, j: (i, j, 0)),
                  pl.BlockSpec((1, nm, w), lambda i, j: (i, 0, 0)),
                  pl.BlockSpec((1, nm, w), lambda i, j: (i, 0, 0))],
        out_specs=pl.BlockSpec((1, tl, w), lambda i, j: (i, j, 0)),
        out_shape=jax.ShapeDtypeStruct((b, l, w), BF16),
        compiler_params=_params(("parallel", "parallel")),
        name="memattn",
    )(q, k, v)


def _merge_kernel(h_ref, ys_ref, yd_ref, ym_ref, g_ref, wglu_ref, bglu_ref, wbs_ref, wbd_ref, wbm_ref, wout_ref,
                  o_ref):
    d = h_ref.shape[1]
    z = _gelu(ys_ref[...].astype(F32))
    gate = _sigmoid(_dot(z.astype(BF16), wglu_ref[...]) + bglu_ref[...])
    a = _dot((z * gate).astype(BF16), wbs_ref[...])
    b = _dot(yd_ref[...], wbd_ref[...])
    c = _dot(ym_ref[...], wbm_ref[...])
    g = g_ref[...].astype(F32)
    merged = g[:, 0:d] * a + g[:, d:2 * d] * b + g[:, 2 * d:3 * d] * c
    o_ref[...] = h_ref[...] + _dot(merged.astype(BF16), wout_ref[...])


def _merge(h, ys, yd, ym, g, wglu, bglu, wbs, wbd, wbm, wout):
    n, d = h.shape
    tm = _pick_tile(n, ROWS_NARROW)
    row = lambda w: pl.BlockSpec((tm, w), lambda i: (i, 0))
    consts = [wglu, bglu, wbs, wbd, wbm, wout]
    return pl.pallas_call(
        _merge_kernel,
        grid=(n // tm,),
        in_specs=[row(d), row(d), row(d), row(d), row(3 * d)] + [_const_spec(c.shape) for c in consts],
        out_specs=row(d),
        out_shape=jax.ShapeDtypeStruct((n, d), F32),
        compiler_params=_params(("parallel",)),
        name="merge",
    )(h, ys, yd, ym, g, *consts)


def _ssm_matrices(lam_re, lam_im, log_dt, b_re, b_im, c_re, c_im, d, t_len):
    hi = lax.Precision.HIGHEST
    g_n, p_n = lam_re.shape
    dt = jnp.exp(log_dt)[:, None]
    mag = jnp.exp(lam_re * dt)
    ar, ai = mag * jnp.cos(lam_im * dt), mag * jnp.sin(lam_im * dt)
    den = lam_re * lam_re + lam_im * lam_im
    nr, ni = ar - 1.0, ai
    kr = ((nr * lam_re + ni * lam_im) / den)[..., None]
    ki = ((ni * lam_re - nr * lam_im) / den)[..., None]
    bbr, bbi = kr * b_re - ki * b_im, kr * b_im + ki * b_re
    j = jnp.arange(t_len + 1, dtype=F32)[:, None, None]
    pmag = jnp.exp(j * (lam_re * dt))
    pr, pi = pmag * jnp.cos(j * (lam_im * dt)), pmag * jnp.sin(j * (lam_im * dt))
    mr = pr[:t_len, ..., None] * bbr - pi[:t_len, ..., None] * bbi
    mi = pr[:t_len, ..., None] * bbi + pi[:t_len, ..., None] * bbr
    kern = (jnp.einsum('gdp,jgpc->jgdc', c_re, mr, precision=hi)
            - jnp.einsum('gdp,jgpc->jgdc', c_im, mi, precision=hi))
    s_i = jnp.arange(t_len)[:, None]
    t_i = jnp.arange(t_len)[None, :]
    lag = t_i - s_i
    kg = jnp.where((lag >= 0)[:, :, None, None, None], kern[jnp.clip(lag, 0)], 0.0)
    tz = kg.transpose(2, 0, 4, 1, 3).reshape(g_n, t_len * SSM_GROUP, t_len * SSM_GROUP)
    pad = ((0, 0), (0, 0), (0, LANES - p_n))
    vr = jnp.pad(mr[::-1].transpose(1, 0, 3, 2).reshape(g_n, t_len * SSM_GROUP, p_n), pad)
    vi = jnp.pad(mi[::-1].transpose(1, 0, 3, 2).reshape(g_n, t_len * SSM_GROUP, p_n), pad)
    tzv = jnp.concatenate([tz, vr, vi], axis=-1).astype(BF16)
    pr1, pi1 = pr[1:].transpose(1, 2, 0), pi[1:].transpose(1, 2, 0)
    crt, cit = c_re.transpose(0, 2, 1), c_im.transpose(0, 2, 1)
    wre = crt[:, :, None, :] * pr1[..., None] - cit[:, :, None, :] * pi1[..., None]
    wim = -(crt[:, :, None, :] * pi1[..., None] + cit[:, :, None, :] * pr1[..., None])
    rpad = ((0, 0), (0, LANES - p_n), (0, 0))
    wre = jnp.pad(wre.reshape(g_n, p_n, -1), rpad).astype(BF16)
    wim = jnp.pad(wim.reshape(g_n, p_n, -1), rpad).astype(BF16)
    atr = jnp.pad(pr[t_len], ((0, 0), (0, LANES - p_n)))[:, None, :]
    ati = jnp.pad(pi[t_len], ((0, 0), (0, LANES - p_n)))[:, None, :]
    drow = jnp.tile(d.reshape(g_n, 1, SSM_GROUP), (1, t_len, 1)).reshape(g_n, 1, t_len * SSM_GROUP)
    return tzv, wre, wim, atr, ati, drow


def _ssm_kernel(nk, nb, u_ref, tzv_ref, wre_ref, wim_ref, atr_ref, ati_ref, d_ref, ire_ref, iim_ref,
                y_ref, fre_ref, fim_ref, yi_ref, sr_ref, si_ref, xr_ref, xi_ref):
    tc = u_ref.shape[2]
    u = u_ref[0]
    full = _dot(u, tzv_ref[0])
    yi_ref[...] = full[:, :tc]
    sr_ref[...] = full[:, tc:tc + LANES]
    si_ref[...] = full[:, tc + LANES:]
    atr, ati = atr_ref[0], ati_ref[0]

    def step(k, carry):
        xr, xi = carry
        rows = pl.ds(pl.multiple_of(k * nb, nb), nb)
        xr_ref[rows, :] = xr
        xi_ref[rows, :] = xi
        return (atr * xr - ati * xi + sr_ref[rows, :], atr * xi + ati * xr + si_ref[rows, :])

    xr, xi = lax.fori_loop(0, nk, step, (ire_ref[0], iim_ref[0]))
    fre_ref[0] = xr
    fim_ref[0] = xi
    y = (yi_ref[...] + _dot(xr_ref[...].astype(BF16), wre_ref[0]) + _dot(xi_ref[...].astype(BF16), wim_ref[0])
         + d_ref[0] * u.astype(F32))
    y_ref[0] = y.astype(BF16)


def _ssm(u, mats, init_re, init_im, nk, nb):
    tzv, wre, wim, atr, ati, drow = mats
    g_n, r, tc = u.shape
    per_g = lambda a: pl.BlockSpec((1,) + a.shape[1:], lambda g: (g, 0, 0))
    ins = [u, tzv, wre, wim, atr, ati, drow, init_re, init_im]
    st = jax.ShapeDtypeStruct((g_n, nb, LANES), F32)
    return pl.pallas_call(
        functools.partial(_ssm_kernel, nk, nb),
        grid=(g_n,),
        in_specs=[per_g(a) for a in ins],
        out_specs=[per_g(u), per_g(init_re), per_g(init_re)],
        out_shape=[jax.ShapeDtypeStruct(u.shape, BF16), st, st],
        scratch_shapes=[pltpu.VMEM((r, tc), F32)] + [pltpu.VMEM((r, LANES), F32)] * 4,
        compiler_params=_params(("parallel",)),
        name="ssm",
    )(*ins)


def _ssm_branch(zu, mats, st_re, st_im):
    b, l, w = zu.shape
    g_n, p_n = st_re.shape[1], st_re.shape[2]
    nk = l // SSM_T
    ug = zu.reshape(b, nk, SSM_T, g_n, SSM_GROUP).transpose(3, 1, 0, 2, 4).reshape(g_n, nk * b, SSM_T * SSM_GROUP)
    pad = ((0, 0), (0, 0), (0, LANES - p_n))
    ire = jnp.pad(st_re.transpose(1, 0, 2), pad)
    iim = jnp.pad(st_im.transpose(1, 0, 2), pad)
    y, fre, fim = _ssm(ug, mats, ire, iim, nk, b)
    y = y.reshape(g_n, nk, b, SSM_T, SSM_GROUP).transpose(2, 1, 3, 0, 4).reshape(b, l, w)
    return y, fre[:, :, :p_n].transpose(1, 0, 2), fim[:, :, :p_n].transpose(1, 0, 2)


SSM_BLOCK = 128
_SSM_SUB = SSM_BLOCK // SSM_T
_SSM_PITCH = 8


def _ssm_matrices_ct(mats):
    tzv, wre, wim, atr, ati, drow = mats
    tc = SSM_T * SSM_GROUP
    perm = np.arange(tc).reshape(SSM_T, SSM_GROUP).T.reshape(-1)
    cols = np.concatenate([perm, np.arange(tc, tzv.shape[2])])
    return tzv[:, perm][:, :, cols], wre[:, :, perm], wim[:, :, perm], atr, ati, drow[:, :, perm]


def _ssm_long_kernel(nb, nk, u_ref, tzv_ref, wre_ref, wim_ref, atr_ref, ati_ref, d_ref, ire_ref, iim_ref,
                     y_ref, fre_ref, fim_ref, yi_ref, sr_ref, si_ref, xr_ref, xi_ref):
    tc = SSM_T * SSM_GROUP
    rows = nb * nk
    pitch = nk + _SSM_PITCH
    x3 = u_ref[...].reshape(SSM_GROUP, rows, SSM_BLOCK)
    tzv = tzv_ref[0]
    lhs = []
    for s in range(_SSM_SUB):
        lhs_s = jnp.concatenate([x3[c][:, s * SSM_T:(s + 1) * SSM_T] for c in range(SSM_GROUP)], axis=1)
        lhs.append(lhs_s)
        full = _dot(lhs_s.astype(BF16), tzv)
        yi_ref[s] = full[:, :tc]
        for b in range(nb):
            sr_ref[s, b * pitch:b * pitch + nk, :] = full[b * nk:(b + 1) * nk, tc:tc + LANES]
            si_ref[s, b * pitch:b * pitch + nk, :] = full[b * nk:(b + 1) * nk, tc + LANES:]
    atr, ati = atr_ref[0], ati_ref[0]

    def step(k, carry):
        xr, xi = carry
        across = pl.ds(k, nb, stride=pitch)
        for s in range(_SSM_SUB):
            xr_ref[s, across, :] = xr
            xi_ref[s, across, :] = xi
            xr, xi = (atr * xr - ati * xi + sr_ref[s, across, :], atr * xi + ati * xr + si_ref[s, across, :])
        return xr, xi

    xr, xi = lax.fori_loop(0, nk, step, (ire_ref[0], iim_ref[0]))
    fre_ref[0] = xr
    fim_ref[0] = xi
    ys = []
    for s in range(_SSM_SUB):
        unpitch = lambda ref: jnp.concatenate([ref[s, b * pitch:b * pitch + nk, :] for b in range(nb)], axis=0)
        ys.append(yi_ref[s] + _dot(unpitch(xr_ref).astype(BF16), wre_ref[0]) + _dot(unpitch(xi_ref).astype(BF16), wim_ref[0])
                  + d_ref[0] * lhs[s])
    y3 = jnp.stack([jnp.concatenate([y[:, c * SSM_T:(c + 1) * SSM_T] for y in ys], axis=1) for c in range(SSM_GROUP)])
    y_ref[...] = y3.reshape(SSM_GROUP, rows * SSM_BLOCK)


def _ssm_long(ut, mats, init_re, init_im, nb, nk):
    tzv, wre, wim, atr, ati, drow = mats
    g_n = tzv.shape[0]
    n = ut.shape[1]
    rows = nb * nk
    per_g = lambda a: pl.BlockSpec((1,) + a.shape[1:], lambda g: (g, 0, 0))
    chan = pl.BlockSpec((SSM_GROUP, n), lambda g: (g, 0))
    st = jax.ShapeDtypeStruct((g_n, nb, LANES), F32)
    prows = nb * (nk + _SSM_PITCH)
    return pl.pallas_call(
        functools.partial(_ssm_long_kernel, nb, nk),
        grid=(g_n,),
        in_specs=[chan] + [per_g(a) for a in (tzv, wre, wim, atr, ati, drow, init_re, init_im)],
        out_specs=[chan, per_g(init_re), per_g(init_re)],
        out_shape=[jax.ShapeDtypeStruct(ut.shape, F32), st, st],
        scratch_shapes=[pltpu.VMEM((_SSM_SUB, rows, SSM_T * SSM_GROUP), F32)]
        + [pltpu.VMEM((_SSM_SUB, prows, LANES), F32)] * 4,
        compiler_params=_params(("parallel",)),
        name="ssm_long",
    )(ut, tzv, wre, wim, atr, ati, drow, init_re, init_im)


def _ssm_branch_long(zu, mats_ct, st_re, st_im):
    b, l, w = zu.shape
    p_n = st_re.shape[2]
    pad = ((0, 0), (0, 0), (0, LANES - p_n))
    ire = jnp.pad(st_re.transpose(1, 0, 2), pad)
    iim = jnp.pad(st_im.transpose(1, 0, 2), pad)
    ut = zu.reshape(b * l, w).astype(F32).T
    yt, fre, fim = _ssm_long(ut, mats_ct, ire, iim, b, l // SSM_BLOCK)
    y = yt.T.astype(BF16).reshape(b, l, w)
    return y, fre[:, :, :p_n].transpose(1, 0, 2), fim[:, :, :p_n].transpose(1, 0, 2)


_DSA_TQ = 256
_DSA_VALUE_STEPS = 24
_DSA_SK = 1024
_DSA_MIN_SUM = 2.0 ** -80


def _dsa_kernel(tq, tv, sk, pos0, s_valid, n_sel, qlat_ref, qidx_ref, w_ref, ka_ref, c_ref, cmax_ref, wuv_ref,
                o_ref, key_ref, bias_ref, j_ref):
    s_pad = key_ref.shape[0]
    q0 = pos0 + pl.program_id(1) * tq
    qpos = q0 + lax.broadcasted_iota(I32, (1, tq), 1)
    vis = jnp.minimum((qpos // CHUNK + 1) * CHUNK, s_valid)
    vis_max = jnp.minimum(((q0 + tq - 1) // CHUNK + 1) * CHUNK, s_valid)
    nch = (vis_max + sk - 1) // sk
    kpos = lax.broadcasted_iota(I32, (sk, tq), 0)
    kslice = lambda j: pl.ds(pl.multiple_of(j * sk, sk), sk)
    fold = lambda a: a.reshape(sk // 8, 8, tq)

    qs = jnp.concatenate([qidx_ref[0, :, hd * LANES:(hd + 1) * LANES] for hd in range(IDX_HEADS)], axis=0)

    def score_chunk(j, carry):
        kmin, kmax = carry
        rel = jnp.maximum(_dot_nt(ka_ref[0, kslice(j), :], qs), 0.0)
        acc = jnp.zeros((sk, tq), F32)
        for hd in range(IDX_HEADS):
            acc = acc + w_ref[0, hd:hd + 1, :] * rel[:, hd * tq:(hd + 1) * tq]
        bits = pltpu.bitcast(acc, I32)
        key = bits ^ ((bits >> 31) & 0x7FFFFFFF)
        visible = j * sk + kpos < vis
        key_ref[kslice(j), :] = jnp.where(visible, key, INT_MIN)
        kmin = jnp.minimum(kmin, jnp.min(fold(jnp.where(visible, key, INT_MAX)), axis=0))
        kmax = jnp.maximum(kmax, jnp.max(fold(jnp.where(visible, key, INT_MIN)), axis=0))
        return kmin, kmax

    kmin, kmax = lax.fori_loop(0, nch, score_chunk,
                               (jnp.full((8, tq), INT_MAX, I32), jnp.full((8, tq), INT_MIN, I32)))
    kmin = jnp.min(kmin, axis=0, keepdims=True)
    kmax = jnp.max(kmax, axis=0, keepdims=True)

    def count(pred):
        def body(j, c):
            hit = jnp.where(pred(key_ref[kslice(j), :], j * sk + kpos), 1.0, 0.0)
            return c + jnp.sum(fold(hit), axis=0)
        return jnp.sum(lax.fori_loop(0, nch, body, jnp.zeros((8, tq), F32)), axis=0, keepdims=True)

    key_of = lambda v: (lambda b: b ^ ((b >> 31) & 0x7FFFFFFF))(pltpu.bitcast(v, I32))
    val_of = lambda k: pltpu.bitcast(k ^ ((k >> 31) & 0x7FFFFFFF), F32)

    def settled(lo, hi, cnt):
        return (cnt <= n_sel) | (hi - 1 <= lo)

    def halve(it, lo, hi, cnt):
        mid_v = key_of(0.5 * val_of(lo) + 0.5 * val_of(hi))
        mid_k = (lo >> 1) + (hi >> 1) + (lo & hi & 1)
        mid = jnp.where((mid_v > lo) & (mid_v < hi) & (it < _DSA_VALUE_STEPS), mid_v, mid_k)
        c = count(lambda k, col: k >= mid)
        live = jnp.logical_not(settled(lo, hi, cnt))
        up = live & (c >= n_sel)
        return jnp.where(up, mid, lo), jnp.where(live & (c < n_sel), mid, hi), jnp.where(up, c, cnt)

    def halve_twice(state):
        it, lo, hi, cnt, _ = state
        lo, hi, cnt = halve(it, lo, hi, cnt)
        lo, hi, cnt = halve(it + 1, lo, hi, cnt)
        return it + 2, lo, hi, cnt, jnp.max(jnp.where(settled(lo, hi, cnt), 0, 1))

    c_pos, c_nn = count(lambda k, col: k >= 1), count(lambda k, col: k >= 0)
    pos, zero = c_pos >= n_sel, c_nn >= n_sel
    lo0 = jnp.where(pos, 1, jnp.where(zero, 0, kmin))
    hi0 = jnp.where(pos, jnp.where(kmax == INT_MAX, INT_MAX, kmax + 1), jnp.where(zero, 1, 0))
    cnt0 = jnp.where(pos, c_pos, jnp.where(zero, c_nn, vis.astype(F32)))
    state = (jnp.int32(0), lo0, hi0, cnt0, jnp.max(jnp.where(settled(lo0, hi0, cnt0), 0, 1)))
    _, thr, _, cnt, _ = lax.while_loop(lambda st: (st[0] < _DSA_VALUE_STEPS + 34) & (st[4] > 0), halve_twice, state)
    thr = jnp.maximum(thr, INT_MIN + 1)
    excess = cnt > n_sel

    j_ref[...] = jnp.full(j_ref.shape, s_pad, I32)

    @pl.when(jnp.max(jnp.where(excess, 1.0, 0.0)) > 0.0)
    def _():
        need = n_sel - count(lambda k, col: k > thr)

        def idx_step(_, lohi):
            lo, hi = lohi
            mid = (lo + hi) >> 1
            ok = count(lambda k, col: (k == thr) & (col < mid)) >= need
            return jnp.where(ok, lo, mid), jnp.where(ok, mid, hi)

        steps = int(math.ceil(math.log2(s_pad))) + 1
        _, hi = lax.fori_loop(0, steps, idx_step, (jnp.zeros((1, tq), I32), jnp.full((1, tq), s_pad, I32)))
        j_ref[...] = jnp.broadcast_to(jnp.where(excess, hi, s_pad), j_ref.shape)

    jlim = j_ref[0:1, :]

    rows = DSA_HEADS * tv
    q = jnp.concatenate([qlat_ref[0, :tv, hd * DSA_LATENT:(hd + 1) * DSA_LATENT] for hd in range(DSA_HEADS)], axis=0)

    wb = min(sk, 2 * LANES)

    kpos_wb = lax.broadcasted_iota(I32, (wb, tq), 0)

    def logits(j, i, first):
        blk = pl.ds(pl.multiple_of(j * sk + i * wb, wb), wb)
        if first:
            k = key_ref[blk, :]
            sel = (k > thr) | ((k == thr) & (j * sk + i * wb + kpos_wb < jlim))
            bias = jnp.where(sel, 0.0, NEG_BIG).T[:tv]
            bias_ref[:, blk] = bias
        else:
            bias = bias_ref[:, blk]
        s = _dot_nt(q, c_ref[0, blk, :])
        return (s.reshape(DSA_HEADS, tv, wb) + bias[None]).reshape(rows, wb)

    def attend(m, first):
        def acc_chunk(j, carry):
            l, acc = carry
            ps = []
            for i in range(sk // wb):
                s = logits(j, i, first)
                for t in range(wb // LANES):
                    p = jnp.exp2(s[:, t * LANES:(t + 1) * LANES] - m)
                    l = l + p
                    ps.append(p.astype(BF16))
            return l, acc + _dot(jnp.concatenate(ps, axis=1), c_ref[0, kslice(j), :])

        l, acc = lax.fori_loop(0, nch, acc_chunk, (jnp.zeros((rows, LANES), F32), jnp.zeros((rows, DSA_LATENT), F32)))
        return jnp.sum(l, axis=1, keepdims=True), acc

    def emit(l, acc):
        o = (acc / l).astype(BF16)
        for hd in range(DSA_HEADS):
            o_ref[0, :tv, hd * DSA_HEAD_DIM:(hd + 1) * DSA_HEAD_DIM] = _dot(o[hd * tv:(hd + 1) * tv], wuv_ref[hd]).astype(BF16)
        if tv < tq:
            o_ref[0, tv:, :] = jnp.zeros((tq - tv, o_ref.shape[2]), BF16)

    qf = q.astype(F32)
    bound = jnp.sqrt(jnp.sum(qf * qf, axis=1, keepdims=True)) * cmax_ref[0, 0:1, 0:1]
    l, acc = attend(jnp.broadcast_to(bound, (rows, LANES)), True)
    healthy = jnp.min(l) > _DSA_MIN_SUM

    @pl.when(healthy)
    def _():
        emit(l, acc)

    @pl.when(jnp.logical_not(healthy))
    def _():
        def max_chunk(j, mx):
            for i in range(sk // wb):
                s = logits(j, i, False)
                for t in range(wb // LANES):
                    mx = jnp.maximum(mx, s[:, t * LANES:(t + 1) * LANES])
            return mx

        mx = lax.fori_loop(0, nch, max_chunk, jnp.full((rows, LANES), NEG_BIG, F32))
        emit(*attend(jnp.broadcast_to(jnp.max(mx, axis=1, keepdims=True), (rows, LANES)), False))


def _dsa(qlat, qidx, w, ka, c, wuv, pos0, s_valid, n_sel, tq):
    b, l_true, _ = qlat.shape
    l = -(-l_true // tq) * tq
    qlat, qidx, w = (jnp.pad(a, ((0, 0), (0, l - l_true), (0, 0))) for a in (qlat, qidx, w))
    wt = w.transpose(0, 2, 1)
    s_pad = c.shape[1]
    sk = min(_DSA_SK, s_pad)
    cf = c.astype(F32)
    cmax = jnp.broadcast_to(jnp.sqrt(jnp.max(jnp.sum(cf * cf, axis=2), axis=1))[:, None, None], (b, 1, LANES))
    qspec = lambda a: pl.BlockSpec((1, tq, a.shape[2]), lambda i, j: (i, j, 0))
    kspec = lambda a: pl.BlockSpec((1,) + a.shape[1:], lambda i, j: (i, 0, 0))
    dh = wuv.shape[0] * wuv.shape[2]
    return pl.pallas_call(
        functools.partial(_dsa_kernel, tq, min(tq, l_true), sk, pos0, s_valid, n_sel),
        grid=(b, l // tq),
        in_specs=[qspec(qlat), qspec(qidx), pl.BlockSpec((1, wt.shape[1], tq), lambda i, j: (i, 0, j)), kspec(ka), kspec(c),
                  kspec(cmax), _const_spec(wuv.shape)],
        out_specs=pl.BlockSpec((1, tq, dh), lambda i, j: (i, j, 0)),
        out_shape=jax.ShapeDtypeStruct((b, l, dh), BF16),
        scratch_shapes=[pltpu.VMEM((s_pad, tq), I32), pltpu.VMEM((min(tq, l_true), s_pad), F32), pltpu.VMEM((8, tq), I32)],
        compiler_params=_params(("parallel", "arbitrary")),
        name="dsa",
    )(qlat, qidx, wt, ka, c, cmax, wuv)[:, :l_true]


_PEER_WIDTH = [PEER_TOPK // (r + 1) for r in range(PEER_TOPK)]


def _top_rows(s, k):
    row = lax.broadcasted_iota(I32, s.shape, 0)
    vals, idxs = [], []
    for _ in range(k):
        m = jnp.max(s, axis=0, keepdims=True)
        first = jnp.min(jnp.where(s == m, row, s.shape[0]), axis=0, keepdims=True)
        s = jnp.where(row == first, -jnp.inf, s)
        vals.append(m)
        idxs.append(first)
    return jnp.concatenate(vals, axis=0), jnp.concatenate(idxs, axis=0)


def _peer_route_kernel(h_ref, g_ref, wq_ref, k1_ref, k2_ref, xn_ref, i1_ref, i2_ref, gate_ref):
    tn = h_ref.shape[0]
    xn = _rms(h_ref[...], g_ref[...]).astype(BF16)
    xn_ref[...] = xn
    q = _dot(xn, wq_ref[...]).astype(BF16)
    oute, outg = [], []
    for hd in range(PEER_HEADS):
        qa = q[:, (2 * hd) * PEER_HALF:(2 * hd + 1) * PEER_HALF]
        qb = q[:, (2 * hd + 1) * PEER_HALF:(2 * hd + 2) * PEER_HALF]
        v1, i1 = _top_rows(_dot_nt(k1_ref[hd], qa), PEER_TOPK)
        v2, i2 = _top_rows(_dot_nt(k2_ref[hd], qb), PEER_TOPK)
        cand = jnp.concatenate([v1[r:r + 1] + v2[0:w] for r, w in enumerate(_PEER_WIDTH)], axis=0)
        cexp = jnp.concatenate([i1[r:r + 1] * PEER_NKEYS + i2[0:w] for r, w in enumerate(_PEER_WIDTH)], axis=0)
        row = lax.broadcasted_iota(I32, cand.shape, 0)
        tops, experts = [], []
        for _ in range(PEER_TOPK):
            m = jnp.max(cand, axis=0, keepdims=True)
            first = jnp.min(jnp.where(cand == m, row, cand.shape[0]), axis=0, keepdims=True)
            hit = row == first
            experts.append(jnp.sum(jnp.where(hit, cexp, 0), axis=0, keepdims=True))
            cand = jnp.where(hit, -jnp.inf, cand)
            tops.append(m)
        top = jnp.concatenate(tops, axis=0)
        ex = jnp.exp(top - top[0:1])
        outg.append(ex / jnp.sum(ex, axis=0, keepdims=True))
        oute.append(jnp.concatenate(experts, axis=0))
    expert = jnp.concatenate(oute, axis=0)
    as_rows = lambda a: pltpu.bitcast(pltpu.bitcast(a, F32).T, I32)
    i1_ref[...] = as_rows(expert >> 7)
    i2_ref[...] = as_rows(expert & (PEER_NKEYS - 1))
    gate_ref[...] = jnp.concatenate(outg, axis=0).T


def _peer_route(h, g, wq, k1, k2):
    n, d = h.shape
    tn = LANES
    slots = PEER_HEADS * PEER_TOPK
    row = lambda w: pl.BlockSpec((tn, w), lambda i: (i, 0))
    return pl.pallas_call(
        _peer_route_kernel,
        grid=(n // tn,),
        in_specs=[row(d), _const_spec(g.shape), _const_spec(wq.shape), _const_spec(k1.shape), _const_spec(k2.shape)],
        out_specs=[row(d), row(slots), row(slots), row(slots)],
        out_shape=[jax.ShapeDtypeStruct((n, d), BF16), jax.ShapeDtypeStruct((n, slots), I32),
                   jax.ShapeDtypeStruct((n, slots), I32), jax.ShapeDtypeStruct((n, slots), F32)],
        compiler_params=_params(("parallel",)),
        name="peer_route",
    )(h, g, wq, k1, k2)


_PLANE_PAD = 4


def _peer_gates_kernel(i1_ref, i2_ref, gate_ref, a_ref, planes_ref):
    tn, slots = i1_ref.shape
    plane = tn + _PLANE_PAD
    sub = lax.broadcasted_iota(I32, (PEER_NKEYS, slots), 0)

    def token(n, carry):
        r = pl.ds(n, 1)
        pt = jnp.where(sub == i1_ref[r, :], gate_ref[r, :], 0.0).astype(BF16)
        qt = jnp.where(sub == i2_ref[r, :], 1.0, 0.0).astype(BF16)
        planes_ref[pl.ds(n, PEER_NKEYS, stride=plane), :] = _dot_nt(pt, qt)
        return carry

    lax.fori_loop(0, tn, token, 0, unroll=True)
    for k in range(PEER_NKEYS):
        a_ref[:, k * PEER_NKEYS:(k + 1) * PEER_NKEYS] = planes_ref[pl.ds(k * plane, tn), :].astype(BF16)


def _peer_gates(i1, i2, gate):
    n, slots = i1.shape
    tn = _pick_tile(n, LANES)
    ne = PEER_NKEYS * PEER_NKEYS
    row = lambda w: pl.BlockSpec((tn, w), lambda i: (i, 0))
    return pl.pallas_call(
        _peer_gates_kernel,
        grid=(n // tn,),
        in_specs=[row(slots)] * 3,
        out_specs=row(ne),
        out_shape=jax.ShapeDtypeStruct((n, ne), BF16),
        scratch_shapes=[pltpu.VMEM((PEER_NKEYS * (tn + _PLANE_PAD), PEER_NKEYS), F32)],
        compiler_params=_params(("parallel",)),
        name="peer_gates",
    )(i1, i2, gate)


_PEER_SUB = 256
def _peer_apply_kernel(final_norm, xn_ref, a_ref, u_ref, v_ref, h_ref, g_ref, o_ref, acc_ref):
    j = pl.program_id(1)

    @pl.when(j == 0)
    def _():
        acc_ref[...] = jnp.zeros_like(acc_ref)

    xn = xn_ref[...]
    ws = []
    for s in range(u_ref.shape[0] // _PEER_SUB):
        sub = slice(s * _PEER_SUB, (s + 1) * _PEER_SUB)
        act = _gelu(_dot_nt(xn, u_ref[sub, :]))
        ws.append((act * a_ref[:, sub].astype(F32)).astype(BF16))
    acc_ref[...] += _dot(jnp.concatenate(ws, axis=1), v_ref[...])

    @pl.when(j == pl.num_programs(1) - 1)
    def _():
        y = h_ref[...] + acc_ref[...]
        o_ref[...] = _rms(y, g_ref[...]) if final_norm else y


def _peer_apply(xn, a, u, v, h, g, final_norm):
    n, d = xn.shape
    ne = u.shape[0]
    tn = _pick_tile(n, ROWS_NARROW)
    te = 8 * _PEER_SUB
    return pl.pallas_call(
        functools.partial(_peer_apply_kernel, final_norm),
        grid=(n // tn, ne // te),
        in_specs=[pl.BlockSpec((tn, d), lambda i, j: (i, 0)), pl.BlockSpec((tn, te), lambda i, j: (i, j)),
                  pl.BlockSpec((te, d), lambda i, j: (j, 0)), pl.BlockSpec((te, d), lambda i, j: (j, 0)),
                  pl.BlockSpec((tn, d), lambda i, j: (i, 0)), _const_spec(g.shape)],
        out_specs=pl.BlockSpec((tn, d), lambda i, j: (i, 0)),
        out_shape=jax.ShapeDtypeStruct((n, d), F32),
        scratch_shapes=[pltpu.VMEM((tn, d), F32)],
        compiler_params=_params(("parallel", "arbitrary")),
        name="peer_apply",
    )(xn, a, u, v, h, g)


def _split_w_in(w_in, d):
    widths = (d, DSA_HEADS * DSA_LATENT, DSA_LATENT, IDX_HEADS * IDX_DIM, IDX_DIM, IDX_HEADS,
              MEM_HEADS * MEM_HEAD_DIM, N_BRANCH * d)
    offs = np.cumsum((0,) + widths)
    wu, wq, wc, wqi, wki, wwi, wmq, wg = [w_in[:, offs[i]:offs[i + 1]] for i in range(8)]
    wkw = jnp.pad(jnp.concatenate([wki, wwi], axis=1), ((0, 0), (0, LANES - IDX_DIM - IDX_HEADS)))
    kw_scale = jnp.concatenate([jnp.ones((IDX_DIM,), F32), jnp.full((IDX_HEADS,), IDX_HEADS ** -0.5, F32),
                                jnp.zeros((LANES - IDX_DIM - IDX_HEADS,), F32)])[None, :]
    wqi = jnp.pad(wqi.reshape(-1, IDX_HEADS, IDX_DIM), ((0, 0), (0, 0), (0, LANES - IDX_DIM))).reshape(-1, IDX_HEADS * LANES)
    ws = [wu, wq * (DSA_LATENT ** -0.5 * LOG2E), wc, wqi, wkw, wmq * MEM_HEAD_DIM ** -0.5, wg]
    return [w.astype(BF16) for w in ws], kw_scale


def _pad_rows(a, n):
    return a if a.shape[0] == n else jnp.pad(a, ((0, n - a.shape[0]),) + ((0, 0),) * (a.ndim - 1))


def _pad_keys(a, s_pad):
    return jnp.pad(a, ((0, 0), (0, s_pad - a.shape[1]), (0, 0)))


def _token_mix(h, seq_shape, lw, mats, state, cache, mem_kv, pos0):
    b, l = seq_shape
    bf = lambda a: a.astype(BF16)
    seq = lambda a: a.reshape(b, l, -1)
    u, q, c, c16, qi, kw, k16, mq, g = _inproj(h, lw["g_norm1"], lw["g_kv"], lw["kw_scale"], lw["w_in"])
    ki, wi = kw[:, :IDX_DIM], kw[:, IDX_DIM:IDX_DIM + IDX_HEADS]

    if l % SSM_BLOCK == 0:
        ys, s_re, s_im = _ssm_branch_long(seq(u), _ssm_matrices_ct(mats), state[0], state[1])
    else:
        ys, s_re, s_im = _ssm_branch(seq(u), mats, state[0], state[1])

    c_all, k_all = seq(c16), seq(k16)
    if cache is not None:
        lane_pad = ((0, 0), (0, 0), (0, LANES - IDX_DIM))
        c_all = jnp.concatenate([bf(cache[0]), c_all], axis=1)
        k_all = jnp.concatenate([jnp.pad(bf(cache[1]), lane_pad), k_all], axis=1)
    s_all = c_all.shape[1]
    s_pad = -(-s_all // _DSA_SK) * _DSA_SK if cache is not None else s_all
    yd = _dsa(seq(q), seq(qi), seq(wi), _pad_keys(k_all, s_pad), _pad_keys(c_all, s_pad), lw["w_uv"], pos0, s_all,
              min(DSA_TOPK, s_all // 4), _pick_tile(max(l, LANES), _DSA_TQ))

    ym = _memattn(seq(mq), mem_kv[0], mem_kv[1])

    flat = lambda a: a.reshape(b * l, -1)
    h2 = _merge(h, flat(ys), flat(yd), flat(ym), g, lw["w_glu"], lw["b_glu"], lw["w_br_ssm"], lw["w_br_dsa"],
                lw["w_br_mem"], lw["w_out"])
    return h2, seq(c), seq(ki), s_re, s_im


def _channel_mix(h2, lw, g_final, final_norm):
    n = h2.shape[0]
    h2 = _pad_rows(h2, -(-n // LANES) * LANES)
    xn, i1, i2, gate = _peer_route(h2, lw["g_norm2"], lw["w_peer_q"], lw["peer_sub_k1"], lw["peer_sub_k2"])
    a = _peer_gates(i1, i2, gate)
    return _peer_apply(xn, a, lw["peer_u"], lw["peer_v"], h2, g_final, final_norm)[:n]


def kernel(x_prompt, x_sample, mem_prompt, cache_dsa_latent, cache_dsa_idx_k, state_ssm_re, state_ssm_im, cache_mem_k, cache_mem_v, g_norm1, w_in, g_kv, w_uv, ssm_lam_re, ssm_lam_im, ssm_log_dt, ssm_b_re, ssm_b_im, ssm_c_re, ssm_c_im, ssm_d, w_glu, b_glu, g_mem, w_mem_kv, w_br_ssm, w_br_dsa, w_br_mem, w_out, g_norm2, w_peer_q, peer_sub_k1, peer_sub_k2, peer_u, peer_v, g_final):
    depth = w_in.shape[0]
    bp, lp, d = x_prompt.shape
    bs, ls, _ = x_sample.shape
    past = cache_dsa_latent.shape[2]
    mem_w = MEM_HEADS * MEM_HEAD_DIM
    n_mem = mem_prompt.shape[1]
    heads = (n_mem, MEM_HEADS, MEM_HEAD_DIM)
    bf = lambda a: a.astype(BF16)
    row = lambda a: a[None, :]

    hp, hs = x_prompt.reshape(bp * lp, d), x_sample.reshape(bs * ls, d)
    outs = [[] for _ in range(10)]
    for l in range(depth):
        ws, kw_scale = _split_w_in(w_in[l], d)
        lw = dict(g_norm1=row(g_norm1[l]), g_kv=row(g_kv[l]), kw_scale=kw_scale, w_in=ws, w_uv=bf(w_uv[l]),
                  w_glu=bf(w_glu[l]), b_glu=row(b_glu[l]), w_br_ssm=bf(w_br_ssm[l]), w_br_dsa=bf(w_br_dsa[l]),
                  w_br_mem=bf(w_br_mem[l]), w_out=bf(w_out[l]), g_norm2=row(g_norm2[l]), w_peer_q=bf(w_peer_q[l]),
                  peer_sub_k1=bf(peer_sub_k1[l]), peer_sub_k2=bf(peer_sub_k2[l]), peer_u=bf(peer_u[l]),
                  peer_v=bf(peer_v[l]))
        mats = _ssm_matrices(ssm_lam_re[l], ssm_lam_im[l], ssm_log_dt[l], ssm_b_re[l], ssm_b_im[l],
                             ssm_c_re[l], ssm_c_im[l], ssm_d[l], SSM_T)
        kv = _memkv(mem_prompt.reshape(bp * n_mem, d), row(g_mem[l]), bf(w_mem_kv[l]))
        mk_p, mv_p = kv[:, :mem_w].reshape(bp, n_mem, mem_w), kv[:, mem_w:].reshape(bp, n_mem, mem_w)
        zeros = jnp.zeros((bp,) + state_ssm_re.shape[2:], F32)

        h2p, c_p, ki_p, sre_p, sim_p = _token_mix(hp, (bp, lp), lw, mats, (zeros, zeros), None, (bf(mk_p), bf(mv_p)), 0)
        mem_s = (bf(cache_mem_k[l].reshape(bs, n_mem, mem_w)), bf(cache_mem_v[l].reshape(bs, n_mem, mem_w)))
        h2s, c_s, ki_s, sre_s, sim_s = _token_mix(hs, (bs, ls), lw, mats, (state_ssm_re[l], state_ssm_im[l]),
                                                  (cache_dsa_latent[l], cache_dsa_idx_k[l]), mem_s, past)
        final = l == depth - 1
        hp = _channel_mix(h2p, lw, row(g_final), final)
        hs = _channel_mix(h2s, lw, row(g_final), final)
        for lst, val in zip(outs, (c_p, ki_p, sre_p, sim_p, mk_p.reshape((bp,) + heads), mv_p.reshape((bp,) + heads),
                                   c_s, ki_s, sre_s, sim_s)):
            lst.append(val)
    return (hp.reshape(bp, lp, d), hs.reshape(bs, ls, d)) + tuple(jnp.stack(o) for o in outs)
```

```python
import functools
import math

import jax
import jax.numpy as jnp
import numpy as np
from jax import lax
from jax.experimental import pallas as pl
from jax.experimental.pallas import tpu as pltpu

F32 = jnp.float32
BF16 = jnp.bfloat16
I32 = jnp.int32

EPS = 1e-6
CHUNK = 64
SSM_GROUP = 16
SSM_STATE = 64
SSM_T = 32
DSA_HEADS = 8
DSA_LATENT = 256
DSA_HEAD_DIM = 128
IDX_HEADS = 8
IDX_DIM = 64
DSA_TOPK = 256
MEM_HEADS = 4
MEM_HEAD_DIM = 256
PEER_HEADS = 8
PEER_NKEYS = 128
PEER_HALF = 128
PEER_TOPK = 16
N_BRANCH = 3

LANES = 128
VMEM_LIMIT = 56 * 1024 * 1024
ROWS_WIDE = 256
ROWS_NARROW = 512
INT_MIN = -2 ** 31
INT_MAX = 2 ** 31 - 1
NEG_BIG = -1e30
LOG2E = 1.4426950408889634


def _pick_tile(n, target):
    if n <= target:
        return n
    for t in range(target, 7, -1):
        if n % t == 0 and t % 8 == 0:
            return t
    return n


def _params(sem):
    return pltpu.CompilerParams(dimension_semantics=sem, vmem_limit_bytes=VMEM_LIMIT)


def _const_spec(shape):
    nd = len(shape)
    return pl.BlockSpec(shape, lambda *_: (0,) * nd, pipeline_mode=pl.Buffered(1))


def _rms(x, g):
    return x * lax.rsqrt(jnp.mean(x * x, axis=-1, keepdims=True) + EPS) * g


def _gelu(x):
    return 0.5 * x * (1.0 + jnp.tanh(0.7978845608028654 * (x + 0.044715 * x * x * x)))


def _sigmoid(x):
    return 1.0 / (1.0 + jnp.exp(-x))


def _dot_nt(a, b):
    return lax.dot_general(a, b, (((1,), (1,)), ((), ())), preferred_element_type=F32)


def _dot(a, b):
    return jnp.dot(a, b, preferred_element_type=F32)


def _inproj_kernel(x_ref, g1_ref, gkv_ref, kws_ref, wu_ref, wq_ref, wc_ref, wqi_ref, wkw_ref, wmq_ref, wg_ref,
                   u_ref, q_ref, c_ref, c16_ref, qi_ref, kw_ref, k16_ref, mq_ref, g_ref):
    xn = _rms(x_ref[...], g1_ref[...]).astype(BF16)
    u_ref[...] = _dot_nt(wu_ref[...], xn)
    q_ref[...] = _dot(xn, wq_ref[...]).astype(BF16)
    c = _rms(_dot(xn, wc_ref[...]), gkv_ref[...])
    c_ref[...] = c
    c16_ref[...] = c.astype(BF16)
    qi_ref[...] = _dot(xn, wqi_ref[...]).astype(BF16)
    kw = _dot(xn, wkw_ref[...]) * kws_ref[...]
    kw_ref[...] = kw
    is_key = lax.broadcasted_iota(I32, kw.shape, 1) < IDX_DIM
    k16_ref[...] = jnp.where(is_key, kw, 0.0).astype(BF16)
    mq_ref[...] = _dot(xn, wmq_ref[...]).astype(BF16)
    g_ref[...] = _sigmoid(_dot(xn, wg_ref[...])).astype(BF16)


def _inproj(h, g1, gkv, kw_scale, ws):
    n, d = h.shape
    tm = _pick_tile(n, ROWS_WIDE)
    wu, wq, wc, wqi, wkw, wmq, wg = (w.shape[1] for w in ws)
    wu = ws[0].shape[0]
    outs = [(wq, BF16), (wc, F32), (wc, BF16), (wqi, BF16), (wkw, F32), (wkw, BF16), (wmq, BF16), (wg, BF16)]
    row = lambda w: pl.BlockSpec((tm, w), lambda i: (i, 0))
    return pl.pallas_call(
        _inproj_kernel,
        grid=(n // tm,),
        in_specs=[row(d), _const_spec(g1.shape), _const_spec(gkv.shape), _const_spec(kw_scale.shape)]
        + [_const_spec(w.shape) for w in ws],
        out_specs=[pl.BlockSpec((wu, tm), lambda i: (0, i))] + [row(w) for w, _ in outs],
        out_shape=[jax.ShapeDtypeStruct((wu, n), F32)] + [jax.ShapeDtypeStruct((n, w), dt) for w, dt in outs],
        compiler_params=_params(("parallel",)),
        name="inproj",
    )(h, g1, gkv, kw_scale, *ws)


def _memkv_kernel(x_ref, g_ref, w_ref, o_ref):
    xn = _rms(x_ref[...], g_ref[...]).astype(BF16)
    o_ref[...] = _dot(xn, w_ref[...])


def _memkv(mem, g, w):
    n, d = mem.shape
    tm = _pick_tile(n, ROWS_WIDE)
    return pl.pallas_call(
        _memkv_kernel,
        grid=(n // tm,),
        in_specs=[pl.BlockSpec((tm, d), lambda i: (i, 0)), _const_spec(g.shape), _const_spec(w.shape)],
        out_specs=pl.BlockSpec((tm, w.shape[1]), lambda i: (i, 0)),
        out_shape=jax.ShapeDtypeStruct((n, w.shape[1]), F32),
        compiler_params=_params(("parallel",)),
        name="memkv",
    )(mem, g, w)


def _memattn_kernel(q_ref, k_ref, v_ref, o_ref):
    for hd in range(MEM_HEADS):
        sl = slice(hd * MEM_HEAD_DIM, (hd + 1) * MEM_HEAD_DIM)
        logits = _dot_nt(q_ref[0, :, sl], k_ref[0, :, sl])
        m = jnp.max(logits, axis=-1, keepdims=True)
        p = jnp.exp(logits - m)
        l = jnp.sum(p, axis=-1, keepdims=True)
        o = _dot(p.astype(BF16), v_ref[0, :, sl]) / l
        o_ref[0, :, sl] = o.astype(BF16)


def _memattn(q, k, v):
    b, l, w = q.shape
    tl = _pick_tile(l, ROWS_NARROW)
    nm = k.shape[1]
    return pl.pallas_call(
        _memattn_kernel,
        grid=(b, l // tl),
        in_specs=[pl.BlockSpec((1, tl, w), lambda i, j: (i, j, 0)),
                  pl.BlockSpec((1, nm, w), lambda i, j: (i, 0, 0)),
                  pl.BlockSpec((1, nm, w), lambda i, j: (i, 0, 0))],
        out_specs=pl.BlockSpec((1, tl, w), lambda i, j: (i, j, 0)),
        out_shape=jax.ShapeDtypeStruct((b, l, w), BF16),
        compiler_params=_params(("parallel", "parallel")),
        name="memattn",
    )(q, k, v)


def _merge_kernel(h_ref, ys_ref, yd_ref, ym_ref, g_ref, wglu_ref, bglu_ref, wbs_ref, wbd_ref, wbm_ref, wout_ref,
                  o_ref):
    d = h_ref.shape[1]
    z = _gelu(ys_ref[...].T)
    gate = _sigmoid(_dot(z.astype(BF16), wglu_ref[...]) + bglu_ref[...])
    a = _dot((z * gate).astype(BF16), wbs_ref[...])
    b = _dot(yd_ref[...], wbd_ref[...])
    c = _dot(ym_ref[...], wbm_ref[...])
    g = g_ref[...].astype(F32)
    merged = g[:, 0:d] * a + g[:, d:2 * d] * b + g[:, 2 * d:3 * d] * c
    o_ref[...] = h_ref[...] + _dot(merged.astype(BF16), wout_ref[...])


def _merge(h, ys, yd, ym, g, wglu, bglu, wbs, wbd, wbm, wout):
    n, d = h.shape
    tm = _pick_tile(n, ROWS_NARROW)
    row = lambda w: pl.BlockSpec((tm, w), lambda i: (i, 0))
    consts = [wglu, bglu, wbs, wbd, wbm, wout]
    return pl.pallas_call(
        _merge_kernel,
        grid=(n // tm,),
        in_specs=[row(d), pl.BlockSpec((d, tm), lambda i: (0, i)), row(d), row(d), row(3 * d)]
        + [_const_spec(c.shape) for c in consts],
        out_specs=row(d),
        out_shape=jax.ShapeDtypeStruct((n, d), F32),
        compiler_params=_params(("parallel",)),
        name="merge",
    )(h, ys, yd, ym, g, *consts)


def _ssm_matrices(lam_re, lam_im, log_dt, b_re, b_im, c_re, c_im, d, t_len):
    hi = lax.Precision.HIGHEST
    g_n, p_n = lam_re.shape
    dt = jnp.exp(log_dt)[:, None]
    mag = jnp.exp(lam_re * dt)
    ar, ai = mag * jnp.cos(lam_im * dt), mag * jnp.sin(lam_im * dt)
    den = lam_re * lam_re + lam_im * lam_im
    nr, ni = ar - 1.0, ai
    kr = ((nr * lam_re + ni * lam_im) / den)[..., None]
    ki = ((ni * lam_re - nr * lam_im) / den)[..., None]
    bbr, bbi = kr * b_re - ki * b_im, kr * b_im + ki * b_re
    j = jnp.arange(t_len + 1, dtype=F32)[:, None, None]
    pmag = jnp.exp(j * (lam_re * dt))
    pr, pi = pmag * jnp.cos(j * (lam_im * dt)), pmag * jnp.sin(j * (lam_im * dt))
    mr = pr[:t_len, ..., None] * bbr - pi[:t_len, ..., None] * bbi
    mi = pr[:t_len, ..., None] * bbi + pi[:t_len, ..., None] * bbr
    kern = (jnp.einsum('gdp,jgpc->jgdc', c_re, mr, precision=hi)
            - jnp.einsum('gdp,jgpc->jgdc', c_im, mi, precision=hi))
    s_i = jnp.arange(t_len)[:, None]
    t_i = jnp.arange(t_len)[None, :]
    lag = t_i - s_i
    kg = jnp.where((lag >= 0)[:, :, None, None, None], kern[jnp.clip(lag, 0)], 0.0)
    tz = kg.transpose(2, 0, 4, 1, 3).reshape(g_n, t_len * SSM_GROUP, t_len * SSM_GROUP)
    pad = ((0, 0), (0, 0), (0, LANES - p_n))
    vr = jnp.pad(mr[::-1].transpose(1, 0, 3, 2).reshape(g_n, t_len * SSM_GROUP, p_n), pad)
    vi = jnp.pad(mi[::-1].transpose(1, 0, 3, 2).reshape(g_n, t_len * SSM_GROUP, p_n), pad)
    tzv = jnp.concatenate([tz, vr, vi], axis=-1).astype(BF16)
    pr1, pi1 = pr[1:].transpose(1, 2, 0), pi[1:].transpose(1, 2, 0)
    crt, cit = c_re.transpose(0, 2, 1), c_im.transpose(0, 2, 1)
    wre = crt[:, :, None, :] * pr1[..., None] - cit[:, :, None, :] * pi1[..., None]
    wim = -(crt[:, :, None, :] * pi1[..., None] + cit[:, :, None, :] * pr1[..., None])
    rpad = ((0, 0), (0, LANES - p_n), (0, 0))
    wre = jnp.pad(wre.reshape(g_n, p_n, -1), rpad).astype(BF16)
    wim = jnp.pad(wim.reshape(g_n, p_n, -1), rpad).astype(BF16)
    atr = jnp.pad(pr[t_len], ((0, 0), (0, LANES - p_n)))[:, None, :]
    ati = jnp.pad(pi[t_len], ((0, 0), (0, LANES - p_n)))[:, None, :]
    drow = jnp.tile(d.reshape(g_n, 1, SSM_GROUP), (1, t_len, 1)).reshape(g_n, 1, t_len * SSM_GROUP)
    return tzv, wre, wim, atr, ati, drow


def _ssm_kernel(nk, nb, u_ref, tzv_ref, wre_ref, wim_ref, atr_ref, ati_ref, d_ref, ire_ref, iim_ref,
                y_ref, fre_ref, fim_ref, yi_ref, sr_ref, si_ref, xr_ref, xi_ref):
    tc = u_ref.shape[2]
    u = u_ref[0]
    full = _dot(u, tzv_ref[0])
    yi_ref[...] = full[:, :tc]
    sr_ref[...] = full[:, tc:tc + LANES]
    si_ref[...] = full[:, tc + LANES:]
    atr, ati = atr_ref[0], ati_ref[0]

    def step(k, carry):
        xr, xi = carry
        rows = pl.ds(pl.multiple_of(k * nb, nb), nb)
        xr_ref[rows, :] = xr
        xi_ref[rows, :] = xi
        return (atr * xr - ati * xi + sr_ref[rows, :], atr * xi + ati * xr + si_ref[rows, :])

    xr, xi = lax.fori_loop(0, nk, step, (ire_ref[0], iim_ref[0]))
    fre_ref[0] = xr
    fim_ref[0] = xi
    y = (yi_ref[...] + _dot(xr_ref[...].astype(BF16), wre_ref[0]) + _dot(xi_ref[...].astype(BF16), wim_ref[0])
         + d_ref[0] * u.astype(F32))
    y_ref[0] = y.astype(BF16)


def _ssm(u, mats, init_re, init_im, nk, nb):
    tzv, wre, wim, atr, ati, drow = mats
    g_n, r, tc = u.shape
    per_g = lambda a: pl.BlockSpec((1,) + a.shape[1:], lambda g: (g, 0, 0))
    ins = [u, tzv, wre, wim, atr, ati, drow, init_re, init_im]
    st = jax.ShapeDtypeStruct((g_n, nb, LANES), F32)
    return pl.pallas_call(
        functools.partial(_ssm_kernel, nk, nb),
        grid=(g_n,),
        in_specs=[per_g(a) for a in ins],
        out_specs=[per_g(u), per_g(init_re), per_g(init_re)],
        out_shape=[jax.ShapeDtypeStruct(u.shape, BF16), st, st],
        scratch_shapes=[pltpu.VMEM((r, tc), F32)] + [pltpu.VMEM((r, LANES), F32)] * 4,
        compiler_params=_params(("parallel",)),
        name="ssm",
    )(*ins)


def _ssm_branch(zu, mats, st_re, st_im):
    b, l, w = zu.shape
    g_n, p_n = st_re.shape[1], st_re.shape[2]
    nk = l // SSM_T
    ug = zu.reshape(b, nk, SSM_T, g_n, SSM_GROUP).transpose(3, 1, 0, 2, 4).reshape(g_n, nk * b, SSM_T * SSM_GROUP)
    pad = ((0, 0), (0, 0), (0, LANES - p_n))
    ire = jnp.pad(st_re.transpose(1, 0, 2), pad)
    iim = jnp.pad(st_im.transpose(1, 0, 2), pad)
    y, fre, fim = _ssm(ug, mats, ire, iim, nk, b)
    y = y.reshape(g_n, nk, b, SSM_T, SSM_GROUP).transpose(2, 1, 3, 0, 4).reshape(b, l, w)
    return y, fre[:, :, :p_n].transpose(1, 0, 2), fim[:, :, :p_n].transpose(1, 0, 2)


SSM_BLOCK = 128
_SSM_SUB = SSM_BLOCK // SSM_T
_SSM_PITCH = 8


def _ssm_matrices_ct(mats):
    tzv, wre, wim, atr, ati, drow = mats
    tc = SSM_T * SSM_GROUP
    perm = np.arange(tc).reshape(SSM_T, SSM_GROUP).T.reshape(-1)
    cols = np.concatenate([perm, np.arange(tc, tzv.shape[2])])
    return tzv[:, perm][:, :, cols], wre[:, :, perm], wim[:, :, perm], atr, ati, drow[:, :, perm]


def _ssm_long_kernel(nb, nk, u_ref, tzv_ref, wre_ref, wim_ref, atr_ref, ati_ref, d_ref, ire_ref, iim_ref,
                     y_ref, fre_ref, fim_ref, yi_ref, sr_ref, si_ref, xr_ref, xi_ref):
    tc = SSM_T * SSM_GROUP
    rows = nb * nk
    pitch = nk + _SSM_PITCH
    x3 = u_ref[...].reshape(SSM_GROUP, rows, SSM_BLOCK)
    tzv = tzv_ref[0]
    lhs = []
    for s in range(_SSM_SUB):
        lhs_s = jnp.concatenate([x3[c][:, s * SSM_T:(s + 1) * SSM_T] for c in range(SSM_GROUP)], axis=1)
        lhs.append(lhs_s)
        full = _dot(lhs_s.astype(BF16), tzv)
        yi_ref[s] = full[:, :tc]
        for b in range(nb):
            sr_ref[s, b * pitch:b * pitch + nk, :] = full[b * nk:(b + 1) * nk, tc:tc + LANES]
            si_ref[s, b * pitch:b * pitch + nk, :] = full[b * nk:(b + 1) * nk, tc + LANES:]
    atr, ati = atr_ref[0], ati_ref[0]

    def step(k, carry):
        xr, xi = carry
        across = pl.ds(k, nb, stride=pitch)
        for s in range(_SSM_SUB):
            xr_ref[s, across, :] = xr
            xi_ref[s, across, :] = xi
            xr, xi = (atr * xr - ati * xi + sr_ref[s, across, :], atr * xi + ati * xr + si_ref[s, across, :])
        return xr, xi

    xr, xi = lax.fori_loop(0, nk, step, (ire_ref[0], iim_ref[0]))
    fre_ref[0] = xr
    fim_ref[0] = xi
    ys = []
    for s in range(_SSM_SUB):
        unpitch = lambda ref: jnp.concatenate([ref[s, b * pitch:b * pitch + nk, :] for b in range(nb)], axis=0)
        ys.append(yi_ref[s] + _dot(unpitch(xr_ref).astype(BF16), wre_ref[0]) + _dot(unpitch(xi_ref).astype(BF16), wim_ref[0])
                  + d_ref[0] * lhs[s])
    y3 = jnp.stack([jnp.concatenate([y[:, c * SSM_T:(c + 1) * SSM_T] for y in ys], axis=1) for c in range(SSM_GROUP)])
    y_ref[...] = y3.reshape(SSM_GROUP, rows * SSM_BLOCK)


def _ssm_long(ut, mats, init_re, init_im, nb, nk):
    tzv, wre, wim, atr, ati, drow = mats
    g_n = tzv.shape[0]
    n = ut.shape[1]
    rows = nb * nk
    per_g = lambda a: pl.BlockSpec((1,) + a.shape[1:], lambda g: (g, 0, 0))
    chan = pl.BlockSpec((SSM_GROUP, n), lambda g: (g, 0))
    st = jax.ShapeDtypeStruct((g_n, nb, LANES), F32)
    prows = nb * (nk + _SSM_PITCH)
    return pl.pallas_call(
        functools.partial(_ssm_long_kernel, nb, nk),
        grid=(g_n,),
        in_specs=[chan] + [per_g(a) for a in (tzv, wre, wim, atr, ati, drow, init_re, init_im)],
        out_specs=[chan, per_g(init_re), per_g(init_re)],
        out_shape=[jax.ShapeDtypeStruct(ut.shape, F32), st, st],
        scratch_shapes=[pltpu.VMEM((_SSM_SUB, rows, SSM_T * SSM_GROUP), F32)]
        + [pltpu.VMEM((_SSM_SUB, prows, LANES), F32)] * 4,
        compiler_params=_params(("parallel",)),
        name="ssm_long",
    )(ut, tzv, wre, wim, atr, ati, drow, init_re, init_im)


def _ssm_branch_long(ut, nb, mats_ct, st_re, st_im):
    p_n = st_re.shape[2]
    pad = ((0, 0), (0, 0), (0, LANES - p_n))
    ire = jnp.pad(st_re.transpose(1, 0, 2), pad)
    iim = jnp.pad(st_im.transpose(1, 0, 2), pad)
    yt, fre, fim = _ssm_long(ut, mats_ct, ire, iim, nb, ut.shape[1] // (nb * SSM_BLOCK))
    return yt, fre[:, :, :p_n].transpose(1, 0, 2), fim[:, :, :p_n].transpose(1, 0, 2)


_DSA_TQ = 256
_DSA_VALUE_STEPS = 24
_DSA_SK = 1024
_DSA_MIN_SUM = 2.0 ** -80


def _dsa_kernel(tq, tv, sk, pos0, s_valid, n_sel, qlat_ref, qidx_ref, w_ref, ka_ref, c_ref, cmax_ref, wuv_ref,
                o_ref, key_ref, bias_ref, j_ref):
    s_pad = key_ref.shape[0]
    q0 = pos0 + pl.program_id(1) * tq
    qpos = q0 + lax.broadcasted_iota(I32, (1, tq), 1)
    vis = jnp.minimum((qpos // CHUNK + 1) * CHUNK, s_valid)
    vis_max = jnp.minimum(((q0 + tq - 1) // CHUNK + 1) * CHUNK, s_valid)
    nch = (vis_max + sk - 1) // sk
    kpos = lax.broadcasted_iota(I32, (sk, tq), 0)
    kslice = lambda j: pl.ds(pl.multiple_of(j * sk, sk), sk)
    fold = lambda a: a.reshape(sk // 8, 8, tq)

    qs = jnp.concatenate([qidx_ref[0, :, hd * LANES:(hd + 1) * LANES] for hd in range(IDX_HEADS)], axis=0)

    def score_chunk(j, carry):
        kmin, kmax = carry
        rel = jnp.maximum(_dot_nt(ka_ref[0, kslice(j), :], qs), 0.0)
        acc = jnp.zeros((sk, tq), F32)
        for hd in range(IDX_HEADS):
            acc = acc + w_ref[0, hd:hd + 1, :] * rel[:, hd * tq:(hd + 1) * tq]
        bits = pltpu.bitcast(acc, I32)
        key = bits ^ ((bits >> 31) & 0x7FFFFFFF)
        visible = j * sk + kpos < vis
        key_ref[kslice(j), :] = jnp.where(visible, key, INT_MIN)
        kmin = jnp.minimum(kmin, jnp.min(fold(jnp.where(visible, key, INT_MAX)), axis=0))
        kmax = jnp.maximum(kmax, jnp.max(fold(jnp.where(visible, key, INT_MIN)), axis=0))
        return kmin, kmax

    kmin, kmax = lax.fori_loop(0, nch, score_chunk,
                               (jnp.full((8, tq), INT_MAX, I32), jnp.full((8, tq), INT_MIN, I32)))
    kmin = jnp.min(kmin, axis=0, keepdims=True)
    kmax = jnp.max(kmax, axis=0, keepdims=True)

    def count(pred):
        def body(j, c):
            hit = jnp.where(pred(key_ref[kslice(j), :], j * sk + kpos), 1.0, 0.0)
            return c + jnp.sum(fold(hit), axis=0)
        return jnp.sum(lax.fori_loop(0, nch, body, jnp.zeros((8, tq), F32)), axis=0, keepdims=True)

    key_of = lambda v: (lambda b: b ^ ((b >> 31) & 0x7FFFFFFF))(pltpu.bitcast(v, I32))
    val_of = lambda k: pltpu.bitcast(k ^ ((k >> 31) & 0x7FFFFFFF), F32)

    def settled(lo, hi, cnt):
        return (cnt <= n_sel) | (hi - 1 <= lo)

    def halve(it, lo, hi, cnt):
        mid_v = key_of(0.5 * val_of(lo) + 0.5 * val_of(hi))
        mid_k = (lo >> 1) + (hi >> 1) + (lo & hi & 1)
        mid = jnp.where((mid_v > lo) & (mid_v < hi) & (it < _DSA_VALUE_STEPS), mid_v, mid_k)
        c = count(lambda k, col: k >= mid)
        live = jnp.logical_not(settled(lo, hi, cnt))
        up = live & (c >= n_sel)
        return jnp.where(up, mid, lo), jnp.where(live & (c < n_sel), mid, hi), jnp.where(up, c, cnt)

    def halve_twice(state):
        it, lo, hi, cnt, _ = state
        lo, hi, cnt = halve(it, lo, hi, cnt)
        lo, hi, cnt = halve(it + 1, lo, hi, cnt)
        return it + 2, lo, hi, cnt, jnp.max(jnp.where(settled(lo, hi, cnt), 0, 1))

    c_pos, c_nn = count(lambda k, col: k >= 1), count(lambda k, col: k >= 0)
    pos, zero = c_pos >= n_sel, c_nn >= n_sel
    lo0 = jnp.where(pos, 1, jnp.where(zero, 0, kmin))
    hi0 = jnp.where(pos, jnp.where(kmax == INT_MAX, INT_MAX, kmax + 1), jnp.where(zero, 1, 0))
    cnt0 = jnp.where(pos, c_pos, jnp.where(zero, c_nn, vis.astype(F32)))
    state = (jnp.int32(0), lo0, hi0, cnt0, jnp.max(jnp.where(settled(lo0, hi0, cnt0), 0, 1)))
    _, thr, _, cnt, _ = lax.while_loop(lambda st: (st[0] < _DSA_VALUE_STEPS + 34) & (st[4] > 0), halve_twice, state)
    thr = jnp.maximum(thr, INT_MIN + 1)
    excess = cnt > n_sel

    j_ref[...] = jnp.full(j_ref.shape, s_pad, I32)

    @pl.when(jnp.max(jnp.where(excess, 1.0, 0.0)) > 0.0)
    def _():
        need = n_sel - count(lambda k, col: k > thr)

        def idx_step(_, lohi):
            lo, hi = lohi
            mid = (lo + hi) >> 1
            ok = count(lambda k, col: (k == thr) & (col < mid)) >= need
            return jnp.where(ok, lo, mid), jnp.where(ok, mid, hi)

        steps = int(math.ceil(math.log2(s_pad))) + 1
        _, hi = lax.fori_loop(0, steps, idx_step, (jnp.zeros((1, tq), I32), jnp.full((1, tq), s_pad, I32)))
        j_ref[...] = jnp.broadcast_to(jnp.where(excess, hi, s_pad), j_ref.shape)

    jlim = j_ref[0:1, :]

    rows = DSA_HEADS * tv
    q = jnp.concatenate([qlat_ref[0, :tv, hd * DSA_LATENT:(hd + 1) * DSA_LATENT] for hd in range(DSA_HEADS)], axis=0)

    wb = min(sk, 2 * LANES)

    kpos_wb = lax.broadcasted_iota(I32, (wb, tq), 0)

    def logits(j, i, first):
        blk = pl.ds(pl.multiple_of(j * sk + i * wb, wb), wb)
        if first:
            k = key_ref[blk, :]
            sel = (k > thr) | ((k == thr) & (j * sk + i * wb + kpos_wb < jlim))
            bias = jnp.where(sel, 0.0, NEG_BIG).T[:tv]
            bias_ref[:, blk] = bias
        else:
            bias = bias_ref[:, blk]
        s = _dot_nt(q, c_ref[0, blk, :])
        return (s.reshape(DSA_HEADS, tv, wb) + bias[None]).reshape(rows, wb)

    def attend(m, first):
        def acc_chunk(j, carry):
            l, acc = carry
            ps = []
            for i in range(sk // wb):
                s = logits(j, i, first)
                for t in range(wb // LANES):
                    p = jnp.exp2(s[:, t * LANES:(t + 1) * LANES] - m)
                    l = l + p
                    ps.append(p.astype(BF16))
            return l, acc + _dot(jnp.concatenate(ps, axis=1), c_ref[0, kslice(j), :])

        l, acc = lax.fori_loop(0, nch, acc_chunk, (jnp.zeros((rows, LANES), F32), jnp.zeros((rows, DSA_LATENT), F32)))
        return jnp.sum(l, axis=1, keepdims=True), acc

    def emit(l, acc):
        o = (acc / l).astype(BF16)
        for hd in range(DSA_HEADS):
            o_ref[0, :tv, hd * DSA_HEAD_DIM:(hd + 1) * DSA_HEAD_DIM] = _dot(o[hd * tv:(hd + 1) * tv], wuv_ref[hd]).astype(BF16)
        if tv < tq:
            o_ref[0, tv:, :] = jnp.zeros((tq - tv, o_ref.shape[2]), BF16)

    qf = q.astype(F32)
    bound = jnp.sqrt(jnp.sum(qf * qf, axis=1, keepdims=True)) * cmax_ref[0, 0:1, 0:1]
    l, acc = attend(jnp.broadcast_to(bound, (rows, LANES)), True)
    healthy = jnp.min(l) > _DSA_MIN_SUM

    @pl.when(healthy)
    def _():
        emit(l, acc)

    @pl.when(jnp.logical_not(healthy))
    def _():
        def max_chunk(j, mx):
            for i in range(sk // wb):
                s = logits(j, i, False)
                for t in range(wb // LANES):
                    mx = jnp.maximum(mx, s[:, t * LANES:(t + 1) * LANES])
            return mx

        mx = lax.fori_loop(0, nch, max_chunk, jnp.full((rows, LANES), NEG_BIG, F32))
        emit(*attend(jnp.broadcast_to(jnp.max(mx, axis=1, keepdims=True), (rows, LANES)), False))


def _dsa(qlat, qidx, w, ka, c, wuv, pos0, s_valid, n_sel, tq):
    b, l_true, _ = qlat.shape
    l = -(-l_true // tq) * tq
    qlat, qidx, w = (jnp.pad(a, ((0, 0), (0, l - l_true), (0, 0))) for a in (qlat, qidx, w))
    wt = w.transpose(0, 2, 1)
    s_pad = c.shape[1]
    sk = min(_DSA_SK, s_pad)
    cf = c.astype(F32)
    cmax = jnp.broadcast_to(jnp.sqrt(jnp.max(jnp.sum(cf * cf, axis=2), axis=1))[:, None, None], (b, 1, LANES))
    qspec = lambda a: pl.BlockSpec((1, tq, a.shape[2]), lambda i, j: (i, j, 0))
    kspec = lambda a: pl.BlockSpec((1,) + a.shape[1:], lambda i, j: (i, 0, 0))
    dh = wuv.shape[0] * wuv.shape[2]
    return pl.pallas_call(
        functools.partial(_dsa_kernel, tq, min(tq, l_true), sk, pos0, s_valid, n_sel),
        grid=(b, l // tq),
        in_specs=[qspec(qlat), qspec(qidx), pl.BlockSpec((1, wt.shape[1], tq), lambda i, j: (i, 0, j)), kspec(ka), kspec(c),
                  kspec(cmax), _const_spec(wuv.shape)],
        out_specs=pl.BlockSpec((1, tq, dh), lambda i, j: (i, j, 0)),
        out_shape=jax.ShapeDtypeStruct((b, l, dh), BF16),
        scratch_shapes=[pltpu.VMEM((s_pad, tq), I32), pltpu.VMEM((min(tq, l_true), s_pad), F32), pltpu.VMEM((8, tq), I32)],
        compiler_params=_params(("parallel", "arbitrary")),
        name="dsa",
    )(qlat, qidx, wt, ka, c, cmax, wuv)[:, :l_true]


_PEER_WIDTH = [PEER_TOPK // (r + 1) for r in range(PEER_TOPK)]


def _top_rows(s, k):
    row = lax.broadcasted_iota(I32, s.shape, 0)
    vals, idxs = [], []
    for _ in range(k):
        m = jnp.max(s, axis=0, keepdims=True)
        first = jnp.min(jnp.where(s == m, row, s.shape[0]), axis=0, keepdims=True)
        s = jnp.where(row == first, -jnp.inf, s)
        vals.append(m)
        idxs.append(first)
    return jnp.concatenate(vals, axis=0), jnp.concatenate(idxs, axis=0)


def _peer_route_kernel(h_ref, g_ref, wq_ref, k1_ref, k2_ref, xn_ref, i1_ref, i2_ref, gate_ref):
    tn = h_ref.shape[0]
    xn = _rms(h_ref[...], g_ref[...]).astype(BF16)
    xn_ref[...] = xn
    q = _dot(xn, wq_ref[...]).astype(BF16)
    oute, outg = [], []
    for hd in range(PEER_HEADS):
        qa = q[:, (2 * hd) * PEER_HALF:(2 * hd + 1) * PEER_HALF]
        qb = q[:, (2 * hd + 1) * PEER_HALF:(2 * hd + 2) * PEER_HALF]
        v1, i1 = _top_rows(_dot_nt(k1_ref[hd], qa), PEER_TOPK)
        v2, i2 = _top_rows(_dot_nt(k2_ref[hd], qb), PEER_TOPK)
        cand = jnp.concatenate([v1[r:r + 1] + v2[0:w] for r, w in enumerate(_PEER_WIDTH)], axis=0)
        cexp = jnp.concatenate([i1[r:r + 1] * PEER_NKEYS + i2[0:w] for r, w in enumerate(_PEER_WIDTH)], axis=0)
        row = lax.broadcasted_iota(I32, cand.shape, 0)
        tops, experts = [], []
        for _ in range(PEER_TOPK):
            m = jnp.max(cand, axis=0, keepdims=True)
            first = jnp.min(jnp.where(cand == m, row, cand.shape[0]), axis=0, keepdims=True)
            hit = row == first
            experts.append(jnp.sum(jnp.where(hit, cexp, 0), axis=0, keepdims=True))
            cand = jnp.where(hit, -jnp.inf, cand)
            tops.append(m)
        top = jnp.concatenate(tops, axis=0)
        ex = jnp.exp(top - top[0:1])
        outg.append(ex / jnp.sum(ex, axis=0, keepdims=True))
        oute.append(jnp.concatenate(experts, axis=0))
    expert = jnp.concatenate(oute, axis=0)
    as_rows = lambda a: pltpu.bitcast(pltpu.bitcast(a, F32).T, I32)
    i1_ref[...] = as_rows(expert >> 7)
    i2_ref[...] = as_rows(expert & (PEER_NKEYS - 1))
    gate_ref[...] = jnp.concatenate(outg, axis=0).T


def _peer_route(h, g, wq, k1, k2):
    n, d = h.shape
    tn = LANES
    slots = PEER_HEADS * PEER_TOPK
    row = lambda w: pl.BlockSpec((tn, w), lambda i: (i, 0))
    return pl.pallas_call(
        _peer_route_kernel,
        grid=(n // tn,),
        in_specs=[row(d), _const_spec(g.shape), _const_spec(wq.shape), _const_spec(k1.shape), _const_spec(k2.shape)],
        out_specs=[row(d), row(slots), row(slots), row(slots)],
        out_shape=[jax.ShapeDtypeStruct((n, d), BF16), jax.ShapeDtypeStruct((n, slots), I32),
                   jax.ShapeDtypeStruct((n, slots), I32), jax.ShapeDtypeStruct((n, slots), F32)],
        compiler_params=_params(("parallel",)),
        name="peer_route",
    )(h, g, wq, k1, k2)


_PLANE_PAD = 4


def _peer_gates_kernel(i1_ref, i2_ref, gate_ref, a_ref, planes_ref):
    tn, slots = i1_ref.shape
    plane = tn + _PLANE_PAD
    sub = lax.broadcasted_iota(I32, (PEER_NKEYS, slots), 0)

    def token(n, carry):
        r = pl.ds(n, 1)
        pt = jnp.where(sub == i1_ref[r, :], gate_ref[r, :], 0.0).astype(BF16)
        qt = jnp.where(sub == i2_ref[r, :], 1.0, 0.0).astype(BF16)
        planes_ref[pl.ds(n, PEER_NKEYS, stride=plane), :] = _dot_nt(pt, qt)
        return carry

    lax.fori_loop(0, tn, token, 0, unroll=True)
    for k in range(PEER_NKEYS):
        a_ref[:, k * PEER_NKEYS:(k + 1) * PEER_NKEYS] = planes_ref[pl.ds(k * plane, tn), :].astype(BF16)


def _peer_gates(i1, i2, gate):
    n, slots = i1.shape
    tn = _pick_tile(n, LANES)
    ne = PEER_NKEYS * PEER_NKEYS
    row = lambda w: pl.BlockSpec((tn, w), lambda i: (i, 0))
    return pl.pallas_call(
        _peer_gates_kernel,
        grid=(n // tn,),
        in_specs=[row(slots)] * 3,
        out_specs=row(ne),
        out_shape=jax.ShapeDtypeStruct((n, ne), BF16),
        scratch_shapes=[pltpu.VMEM((PEER_NKEYS * (tn + _PLANE_PAD), PEER_NKEYS), F32)],
        compiler_params=_params(("parallel",)),
        name="peer_gates",
    )(i1, i2, gate)


_PEER_SUB = 256
def _peer_apply_kernel(final_norm, xn_ref, a_ref, u_ref, v_ref, h_ref, g_ref, o_ref, acc_ref):
    j = pl.program_id(1)

    @pl.when(j == 0)
    def _():
        acc_ref[...] = jnp.zeros_like(acc_ref)

    xn = xn_ref[...]
    ws = []
    for s in range(u_ref.shape[0] // _PEER_SUB):
        sub = slice(s * _PEER_SUB, (s + 1) * _PEER_SUB)
        act = _gelu(_dot_nt(xn, u_ref[sub, :]))
        ws.append((act * a_ref[:, sub].astype(F32)).astype(BF16))
    acc_ref[...] += _dot(jnp.concatenate(ws, axis=1), v_ref[...])

    @pl.when(j == pl.num_programs(1) - 1)
    def _():
        y = h_ref[...] + acc_ref[...]
        o_ref[...] = _rms(y, g_ref[...]) if final_norm else y


def _peer_apply(xn, a, u, v, h, g, final_norm):
    n, d = xn.shape
    ne = u.shape[0]
    tn = _pick_tile(n, ROWS_NARROW)
    te = 8 * _PEER_SUB
    return pl.pallas_call(
        functools.partial(_peer_apply_kernel, final_norm),
        grid=(n // tn, ne // te),
        in_specs=[pl.BlockSpec((tn, d), lambda i, j: (i, 0)), pl.BlockSpec((tn, te), lambda i, j: (i, j)),
                  pl.BlockSpec((te, d), lambda i, j: (j, 0)), pl.BlockSpec((te, d), lambda i, j: (j, 0)),
                  pl.BlockSpec((tn, d), lambda i, j: (i, 0)), _const_spec(g.shape)],
        out_specs=pl.BlockSpec((tn, d), lambda i, j: (i, 0)),
        out_shape=jax.ShapeDtypeStruct((n, d), F32),
        scratch_shapes=[pltpu.VMEM((tn, d), F32)],
        compiler_params=_params(("parallel", "arbitrary")),
        name="peer_apply",
    )(xn, a, u, v, h, g)


def _split_w_in(w_in, d):
    widths = (d, DSA_HEADS * DSA_LATENT, DSA_LATENT, IDX_HEADS * IDX_DIM, IDX_DIM, IDX_HEADS,
              MEM_HEADS * MEM_HEAD_DIM, N_BRANCH * d)
    offs = np.cumsum((0,) + widths)
    wu, wq, wc, wqi, wki, wwi, wmq, wg = [w_in[:, offs[i]:offs[i + 1]] for i in range(8)]
    wkw = jnp.pad(jnp.concatenate([wki, wwi], axis=1), ((0, 0), (0, LANES - IDX_DIM - IDX_HEADS)))
    kw_scale = jnp.concatenate([jnp.ones((IDX_DIM,), F32), jnp.full((IDX_HEADS,), IDX_HEADS ** -0.5, F32),
                                jnp.zeros((LANES - IDX_DIM - IDX_HEADS,), F32)])[None, :]
    wqi = jnp.pad(wqi.reshape(-1, IDX_HEADS, IDX_DIM), ((0, 0), (0, 0), (0, LANES - IDX_DIM))).reshape(-1, IDX_HEADS * LANES)
    ws = [wu.T, wq * (DSA_LATENT ** -0.5 * LOG2E), wc, wqi, wkw, wmq * MEM_HEAD_DIM ** -0.5, wg]
    return [w.astype(BF16) for w in ws], kw_scale


def _pad_rows(a, n):
    return a if a.shape[0] == n else jnp.pad(a, ((0, n - a.shape[0]),) + ((0, 0),) * (a.ndim - 1))


def _pad_keys(a, s_pad):
    return jnp.pad(a, ((0, 0), (0, s_pad - a.shape[1]), (0, 0)))


def _token_mix(h, seq_shape, lw, mats, state, cache, mem_kv, pos0):
    b, l = seq_shape
    bf = lambda a: a.astype(BF16)
    seq = lambda a: a.reshape(b, l, -1)
    ut, q, c, c16, qi, kw, k16, mq, g = _inproj(h, lw["g_norm1"], lw["g_kv"], lw["kw_scale"], lw["w_in"])
    ki, wi = kw[:, :IDX_DIM], kw[:, IDX_DIM:IDX_DIM + IDX_HEADS]

    if l % SSM_BLOCK == 0:
        yst, s_re, s_im = _ssm_branch_long(ut, b, _ssm_matrices_ct(mats), state[0], state[1])
    else:
        ys, s_re, s_im = _ssm_branch(seq(bf(ut.T)), mats, state[0], state[1])
        yst = ys.reshape(b * l, -1).astype(F32).T

    c_all, k_all = seq(c16), seq(k16)
    if cache is not None:
        lane_pad = ((0, 0), (0, 0), (0, LANES - IDX_DIM))
        c_all = jnp.concatenate([bf(cache[0]), c_all], axis=1)
        k_all = jnp.concatenate([jnp.pad(bf(cache[1]), lane_pad), k_all], axis=1)
    s_all = c_all.shape[1]
    s_pad = -(-s_all // _DSA_SK) * _DSA_SK if cache is not None else s_all
    yd = _dsa(seq(q), seq(qi), seq(wi), _pad_keys(k_all, s_pad), _pad_keys(c_all, s_pad), lw["w_uv"], pos0, s_all,
              min(DSA_TOPK, s_all // 4), _pick_tile(max(l, LANES), _DSA_TQ))

    ym = _memattn(seq(mq), mem_kv[0], mem_kv[1])

    flat = lambda a: a.reshape(b * l, -1)
    h2 = _merge(h, yst, flat(yd), flat(ym), g, lw["w_glu"], lw["b_glu"], lw["w_br_ssm"], lw["w_br_dsa"],
                lw["w_br_mem"], lw["w_out"])
    return h2, seq(c), seq(ki), s_re, s_im


def _channel_mix(h2, lw, g_final, final_norm):
    n = h2.shape[0]
    h2 = _pad_rows(h2, -(-n // LANES) * LANES)
    xn, i1, i2, gate = _peer_route(h2, lw["g_norm2"], lw["w_peer_q"], lw["peer_sub_k1"], lw["peer_sub_k2"])
    a = _peer_gates(i1, i2, gate)
    return _peer_apply(xn, a, lw["peer_u"], lw["peer_v"], h2, g_final, final_norm)[:n]


def kernel(x_prompt, x_sample, mem_prompt, cache_dsa_latent, cache_dsa_idx_k, state_ssm_re, state_ssm_im, cache_mem_k, cache_mem_v, g_norm1, w_in, g_kv, w_uv, ssm_lam_re, ssm_lam_im, ssm_log_dt, ssm_b_re, ssm_b_im, ssm_c_re, ssm_c_im, ssm_d, w_glu, b_glu, g_mem, w_mem_kv, w_br_ssm, w_br_dsa, w_br_mem, w_out, g_norm2, w_peer_q, peer_sub_k1, peer_sub_k2, peer_u, peer_v, g_final):
    depth = w_in.shape[0]
    bp, lp, d = x_prompt.shape
    bs, ls, _ = x_sample.shape
    past = cache_dsa_latent.shape[2]
    mem_w = MEM_HEADS * MEM_HEAD_DIM
    n_mem = mem_prompt.shape[1]
    heads = (n_mem, MEM_HEADS, MEM_HEAD_DIM)
    bf = lambda a: a.astype(BF16)
    row = lambda a: a[None, :]

    hp, hs = x_prompt.reshape(bp * lp, d), x_sample.reshape(bs * ls, d)
    outs = [[] for _ in range(10)]
    for l in range(depth):
        ws, kw_scale = _split_w_in(w_in[l], d)
        lw = dict(g_norm1=row(g_norm1[l]), g_kv=row(g_kv[l]), kw_scale=kw_scale, w_in=ws, w_uv=bf(w_uv[l]),
                  w_glu=bf(w_glu[l]), b_glu=row(b_glu[l]), w_br_ssm=bf(w_br_ssm[l]), w_br_dsa=bf(w_br_dsa[l]),
                  w_br_mem=bf(w_br_mem[l]), w_out=bf(w_out[l]), g_norm2=row(g_norm2[l]), w_peer_q=bf(w_peer_q[l]),
                  peer_sub_k1=bf(peer_sub_k1[l]), peer_sub_k2=bf(peer_sub_k2[l]), peer_u=bf(peer_u[l]),
                  peer_v=bf(peer_v[l]))
        mats = _ssm_matrices(ssm_lam_re[l], ssm_lam_im[l], ssm_log_dt[l], ssm_b_re[l], ssm_b_im[l],
                             ssm_c_re[l], ssm_c_im[l], ssm_d[l], SSM_T)
        kv = _memkv(mem_prompt.reshape(bp * n_mem, d), row(g_mem[l]), bf(w_mem_kv[l]))
        mk_p, mv_p = kv[:, :mem_w].reshape(bp, n_mem, mem_w), kv[:, mem_w:].reshape(bp, n_mem, mem_w)
        zeros = jnp.zeros((bp,) + state_ssm_re.shape[2:], F32)

        h2p, c_p, ki_p, sre_p, sim_p = _token_mix(hp, (bp, lp), lw, mats, (zeros, zeros), None, (bf(mk_p), bf(mv_p)), 0)
        mem_s = (bf(cache_mem_k[l].reshape(bs, n_mem, mem_w)), bf(cache_mem_v[l].reshape(bs, n_mem, mem_w)))
        h2s, c_s, ki_s, sre_s, sim_s = _token_mix(hs, (bs, ls), lw, mats, (state_ssm_re[l], state_ssm_im[l]),
                                                  (cache_dsa_latent[l], cache_dsa_idx_k[l]), mem_s, past)
        final = l == depth - 1
        hp = _channel_mix(h2p, lw, row(g_final), final)
        hs = _channel_mix(h2s, lw, row(g_final), final)
        for lst, val in zip(outs, (c_p, ki_p, sre_p, sim_p, mk_p.reshape((bp,) + heads), mv_p.reshape((bp,) + heads),
                                   c_s, ki_s, sre_s, sim_s)):
            lst.append(val)
    return (hp.reshape(bp, lp, d), hs.reshape(bs, ls, d)) + tuple(jnp.stack(o) for o in outs)
```

```python
import functools
import math

import jax
import jax.numpy as jnp
import numpy as np
from jax import lax
from jax.experimental import pallas as pl
from jax.experimental.pallas import tpu as pltpu

F32 = jnp.float32
BF16 = jnp.bfloat16
I32 = jnp.int32

EPS = 1e-6
CHUNK = 64
SSM_GROUP = 16
SSM_STATE = 64
SSM_T = 32
DSA_HEADS = 8
DSA_LATENT = 256
DSA_HEAD_DIM = 128
IDX_HEADS = 8
IDX_DIM = 64
DSA_TOPK = 256
MEM_HEADS = 4
MEM_HEAD_DIM = 256
PEER_HEADS = 8
PEER_NKEYS = 128
PEER_HALF = 128
PEER_TOPK = 16
N_BRANCH = 3

LANES = 128
VMEM_LIMIT = 56 * 1024 * 1024
ROWS_WIDE = 256
ROWS_NARROW = 512
INT_MIN = -2 ** 31
INT_MAX = 2 ** 31 - 1
NEG_BIG = -1e30
LOG2E = 1.4426950408889634


def _pick_tile(n, target):
    if n <= target:
        return n
    for t in range(target, 7, -1):
        if n % t == 0 and t % 8 == 0:
            return t
    return n


def _params(sem):
    return pltpu.CompilerParams(dimension_semantics=sem, vmem_limit_bytes=VMEM_LIMIT)


def _const_spec(shape):
    nd = len(shape)
    return pl.BlockSpec(shape, lambda *_: (0,) * nd, pipeline_mode=pl.Buffered(1))


def _rms(x, g):
    return x * lax.rsqrt(jnp.mean(x * x, axis=-1, keepdims=True) + EPS) * g


def _gelu(x):
    return 0.5 * x * (1.0 + jnp.tanh(0.7978845608028654 * (x + 0.044715 * x * x * x)))


def _sigmoid(x):
    return 1.0 / (1.0 + jnp.exp(-x))


def _dot_nt(a, b):
    return lax.dot_general(a, b, (((1,), (1,)), ((), ())), preferred_element_type=F32)


def _dot(a, b):
    return jnp.dot(a, b, preferred_element_type=F32)


def _inproj_kernel(x_ref, g1_ref, gkv_ref, kws_ref, wu_ref, wq_ref, wc_ref, wqi_ref, wkw_ref, wmq_ref, wg_ref,
                   u_ref, q_ref, c_ref, c16_ref, qi_ref, kw_ref, k16_ref, mq_ref, g_ref):
    xn = _rms(x_ref[...], g1_ref[...]).astype(BF16)
    u_ref[...] = _dot_nt(wu_ref[...], xn)
    q_ref[...] = _dot(xn, wq_ref[...]).astype(BF16)
    c = _rms(_dot(xn, wc_ref[...]), gkv_ref[...])
    c_ref[...] = c
    c16_ref[...] = c.astype(BF16)
    qi_ref[...] = _dot(xn, wqi_ref[...]).astype(BF16)
    kw = _dot(xn, wkw_ref[...]) * kws_ref[...]
    kw_ref[...] = kw
    is_key = lax.broadcasted_iota(I32, kw.shape, 1) < IDX_DIM
    k16_ref[...] = jnp.where(is_key, kw, 0.0).astype(BF16)
    mq_ref[...] = _dot(xn, wmq_ref[...]).astype(BF16)
    g_ref[...] = _sigmoid(_dot(xn, wg_ref[...])).astype(BF16)


def _inproj(h, g1, gkv, kw_scale, ws):
    n, d = h.shape
    tm = _pick_tile(n, ROWS_WIDE)
    wu, wq, wc, wqi, wkw, wmq, wg = (w.shape[1] for w in ws)
    wu = ws[0].shape[0]
    outs = [(wq, BF16), (wc, F32), (wc, BF16), (wqi, BF16), (wkw, F32), (wkw, BF16), (wmq, BF16), (wg, BF16)]
    row = lambda w: pl.BlockSpec((tm, w), lambda i: (i, 0))
    return pl.pallas_call(
        _inproj_kernel,
        grid=(n // tm,),
        in_specs=[row(d), _const_spec(g1.shape), _const_spec(gkv.shape), _const_spec(kw_scale.shape)]
        + [_const_spec(w.shape) for w in ws],
        out_specs=[pl.BlockSpec((wu, tm), lambda i: (0, i))] + [row(w) for w, _ in outs],
        out_shape=[jax.ShapeDtypeStruct((wu, n), F32)] + [jax.ShapeDtypeStruct((n, w), dt) for w, dt in outs],
        compiler_params=_params(("parallel",)),
        name="inproj",
    )(h, g1, gkv, kw_scale, *ws)


def _memkv_kernel(x_ref, g_ref, w_ref, o_ref):
    xn = _rms(x_ref[...], g_ref[...]).astype(BF16)
    o_ref[...] = _dot(xn, w_ref[...])


def _memkv(mem, g, w):
    n, d = mem.shape
    tm = _pick_tile(n, ROWS_WIDE)
    return pl.pallas_call(
        _memkv_kernel,
        grid=(n // tm,),
        in_specs=[pl.BlockSpec((tm, d), lambda i: (i, 0)), _const_spec(g.shape), _const_spec(w.shape)],
        out_specs=pl.BlockSpec((tm, w.shape[1]), lambda i: (i, 0)),
        out_shape=jax.ShapeDtypeStruct((n, w.shape[1]), F32),
        compiler_params=_params(("parallel",)),
        name="memkv",
    )(mem, g, w)


def _memattn_kernel(q_ref, k_ref, v_ref, o_ref):
    for hd in range(MEM_HEADS):
        sl = slice(hd * MEM_HEAD_DIM, (hd + 1) * MEM_HEAD_DIM)
        logits = _dot_nt(q_ref[0, :, sl], k_ref[0, :, sl])
        m = jnp.max(logits, axis=-1, keepdims=True)
        p = jnp.exp(logits - m)
        l = jnp.sum(p, axis=-1, keepdims=True)
        o = _dot(p.astype(BF16), v_ref[0, :, sl]) / l
        o_ref[0, :, sl] = o.astype(BF16)


def _memattn(q, k, v):
    b, l, w = q.shape
    tl = _pick_tile(l, ROWS_NARROW)
    nm = k.shape[1]
    return pl.pallas_call(
        _memattn_kernel,
        grid=(b, l // tl),
        in_specs=[pl.BlockSpec((1, tl, w), lambda i, j: (i, j, 0)),
                  pl.BlockSpec((1, nm, w), lambda i, j: (i, 0, 0)),
                  pl.BlockSpec((1, nm, w), lambda i, j: (i, 0, 0))],
        out_specs=pl.BlockSpec((1, tl, w), lambda i, j: (i, j, 0)),
        out_shape=jax.ShapeDtypeStruct((b, l, w), BF16),
        compiler_params=_params(("parallel", "parallel")),
        name="memattn",
    )(q, k, v)


def _merge_kernel(h_ref, ys_ref, yd_ref, ym_ref, g_ref, wglu_ref, bglu_ref, wbs_ref, wbd_ref, wbm_ref, wout_ref,
                  o_ref):
    d = h_ref.shape[1]
    z = _gelu(ys_ref[...].T)
    gate = _sigmoid(_dot(z.astype(BF16), wglu_ref[...]) + bglu_ref[...])
    a = _dot((z * gate).astype(BF16), wbs_ref[...])
    b = _dot(yd_ref[...], wbd_ref[...])
    c = _dot(ym_ref[...], wbm_ref[...])
    g = g_ref[...].astype(F32)
    merged = g[:, 0:d] * a + g[:, d:2 * d] * b + g[:, 2 * d:3 * d] * c
    o_ref[...] = h_ref[...] + _dot(merged.astype(BF16), wout_ref[...])


def _merge(h, ys, yd, ym, g, wglu, bglu, wbs, wbd, wbm, wout):
    n, d = h.shape
    tm = _pick_tile(n, ROWS_NARROW)
    row = lambda w: pl.BlockSpec((tm, w), lambda i: (i, 0))
    consts = [wglu, bglu, wbs, wbd, wbm, wout]
    return pl.pallas_call(
        _merge_kernel,
        grid=(n // tm,),
        in_specs=[row(d), pl.BlockSpec((d, tm), lambda i: (0, i)), row(d), row(d), row(3 * d)]
        + [_const_spec(c.shape) for c in consts],
        out_specs=row(d),
        out_shape=jax.ShapeDtypeStruct((n, d), F32),
        compiler_params=_params(("parallel",)),
        name="merge",
    )(h, ys, yd, ym, g, *consts)


def _ssm_matrices(lam_re, lam_im, log_dt, b_re, b_im, c_re, c_im, d, t_len):
    hi = lax.Precision.HIGHEST
    g_n, p_n = lam_re.shape
    dt = jnp.exp(log_dt)[:, None]
    mag = jnp.exp(lam_re * dt)
    ar, ai = mag * jnp.cos(lam_im * dt), mag * jnp.sin(lam_im * dt)
    den = lam_re * lam_re + lam_im * lam_im
    nr, ni = ar - 1.0, ai
    kr = ((nr * lam_re + ni * lam_im) / den)[..., None]
    ki = ((ni * lam_re - nr * lam_im) / den)[..., None]
    bbr, bbi = kr * b_re - ki * b_im, kr * b_im + ki * b_re
    j = jnp.arange(t_len + 1, dtype=F32)[:, None, None]
    pmag = jnp.exp(j * (lam_re * dt))
    pr, pi = pmag * jnp.cos(j * (lam_im * dt)), pmag * jnp.sin(j * (lam_im * dt))
    mr = pr[:t_len, ..., None] * bbr - pi[:t_len, ..., None] * bbi
    mi = pr[:t_len, ..., None] * bbi + pi[:t_len, ..., None] * bbr
    kern = (jnp.einsum('gdp,jgpc->jgdc', c_re, mr, precision=hi)
            - jnp.einsum('gdp,jgpc->jgdc', c_im, mi, precision=hi))
    s_i = jnp.arange(t_len)[:, None]
    t_i = jnp.arange(t_len)[None, :]
    lag = t_i - s_i
    kg = jnp.where((lag >= 0)[:, :, None, None, None], kern[jnp.clip(lag, 0)], 0.0)
    tz = kg.transpose(2, 4, 0, 3, 1).reshape(g_n, t_len * SSM_GROUP, t_len * SSM_GROUP)
    pad = ((0, 0), (0, 0), (0, LANES - p_n))
    vr = jnp.pad(mr[::-1].transpose(1, 3, 0, 2).reshape(g_n, t_len * SSM_GROUP, p_n), pad)
    vi = jnp.pad(mi[::-1].transpose(1, 3, 0, 2).reshape(g_n, t_len * SSM_GROUP, p_n), pad)
    tzv = jnp.concatenate([tz, vr, vi], axis=-1).astype(BF16)
    pr1, pi1 = pr[1:].transpose(1, 2, 0), pi[1:].transpose(1, 2, 0)
    crt, cit = c_re.transpose(0, 2, 1), c_im.transpose(0, 2, 1)
    wre = crt[..., None] * pr1[:, :, None, :] - cit[..., None] * pi1[:, :, None, :]
    wim = -(crt[..., None] * pi1[:, :, None, :] + cit[..., None] * pr1[:, :, None, :])
    rpad = ((0, 0), (0, LANES - p_n), (0, 0))
    wre = jnp.pad(wre.reshape(g_n, p_n, -1), rpad).astype(BF16)
    wim = jnp.pad(wim.reshape(g_n, p_n, -1), rpad).astype(BF16)
    atr = jnp.pad(pr[t_len], ((0, 0), (0, LANES - p_n)))[:, None, :]
    ati = jnp.pad(pi[t_len], ((0, 0), (0, LANES - p_n)))[:, None, :]
    drow = jnp.repeat(d.reshape(g_n, SSM_GROUP), t_len, axis=1)[:, None, :]
    return tzv, wre, wim, atr, ati, drow


def _ssm_kernel(nk, nb, u_ref, tzv_ref, wre_ref, wim_ref, atr_ref, ati_ref, d_ref, ire_ref, iim_ref,
                y_ref, fre_ref, fim_ref, yi_ref, sr_ref, si_ref, xr_ref, xi_ref):
    tc = u_ref.shape[2]
    u = u_ref[0]
    full = _dot(u, tzv_ref[0])
    yi_ref[...] = full[:, :tc]
    sr_ref[...] = full[:, tc:tc + LANES]
    si_ref[...] = full[:, tc + LANES:]
    atr, ati = atr_ref[0], ati_ref[0]

    def step(k, carry):
        xr, xi = carry
        rows = pl.ds(pl.multiple_of(k * nb, nb), nb)
        xr_ref[rows, :] = xr
        xi_ref[rows, :] = xi
        return (atr * xr - ati * xi + sr_ref[rows, :], atr * xi + ati * xr + si_ref[rows, :])

    xr, xi = lax.fori_loop(0, nk, step, (ire_ref[0], iim_ref[0]))
    fre_ref[0] = xr
    fim_ref[0] = xi
    y = (yi_ref[...] + _dot(xr_ref[...].astype(BF16), wre_ref[0]) + _dot(xi_ref[...].astype(BF16), wim_ref[0])
         + d_ref[0] * u.astype(F32))
    y_ref[0] = y.astype(BF16)


def _ssm(u, mats, init_re, init_im, nk, nb):
    tzv, wre, wim, atr, ati, drow = mats
    g_n, r, tc = u.shape
    per_g = lambda a: pl.BlockSpec((1,) + a.shape[1:], lambda g: (g, 0, 0))
    ins = [u, tzv, wre, wim, atr, ati, drow, init_re, init_im]
    st = jax.ShapeDtypeStruct((g_n, nb, LANES), F32)
    return pl.pallas_call(
        functools.partial(_ssm_kernel, nk, nb),
        grid=(g_n,),
        in_specs=[per_g(a) for a in ins],
        out_specs=[per_g(u), per_g(init_re), per_g(init_re)],
        out_shape=[jax.ShapeDtypeStruct(u.shape, BF16), st, st],
        scratch_shapes=[pltpu.VMEM((r, tc), F32)] + [pltpu.VMEM((r, LANES), F32)] * 4,
        compiler_params=_params(("parallel",)),
        name="ssm",
    )(*ins)


def _ssm_branch(zu, mats, st_re, st_im):
    b, l, w = zu.shape
    g_n, p_n = st_re.shape[1], st_re.shape[2]
    nk = l // SSM_T
    ug = zu.reshape(b, nk, SSM_T, g_n, SSM_GROUP).transpose(3, 1, 0, 4, 2).reshape(g_n, nk * b, SSM_T * SSM_GROUP)
    pad = ((0, 0), (0, 0), (0, LANES - p_n))
    ire = jnp.pad(st_re.transpose(1, 0, 2), pad)
    iim = jnp.pad(st_im.transpose(1, 0, 2), pad)
    y, fre, fim = _ssm(ug, mats, ire, iim, nk, b)
    y = y.reshape(g_n, nk, b, SSM_GROUP, SSM_T).transpose(2, 1, 4, 0, 3).reshape(b, l, w)
    return y, fre[:, :, :p_n].transpose(1, 0, 2), fim[:, :, :p_n].transpose(1, 0, 2)


SSM_BLOCK = 128
_SSM_SUB = SSM_BLOCK // SSM_T
_SSM_PITCH = 8


def _ssm_long_kernel(nb, nk, u_ref, tzv_ref, wre_ref, wim_ref, atr_ref, ati_ref, d_ref, ire_ref, iim_ref,
                     y_ref, fre_ref, fim_ref, yi_ref, sr_ref, si_ref, xr_ref, xi_ref):
    tc = SSM_T * SSM_GROUP
    rows = nb * nk
    pitch = nk + _SSM_PITCH
    x3 = u_ref[...].reshape(SSM_GROUP, rows, SSM_BLOCK)
    tzv = tzv_ref[0]
    lhs = []
    for s in range(_SSM_SUB):
        lhs_s = jnp.concatenate([x3[c][:, s * SSM_T:(s + 1) * SSM_T] for c in range(SSM_GROUP)], axis=1)
        lhs.append(lhs_s)
        full = _dot(lhs_s.astype(BF16), tzv)
        yi_ref[s] = full[:, :tc]
        for b in range(nb):
            sr_ref[s, b * pitch:b * pitch + nk, :] = full[b * nk:(b + 1) * nk, tc:tc + LANES]
            si_ref[s, b * pitch:b * pitch + nk, :] = full[b * nk:(b + 1) * nk, tc + LANES:]
    atr, ati = atr_ref[0], ati_ref[0]

    def step(k, carry):
        xr, xi = carry
        across = pl.ds(k, nb, stride=pitch)
        for s in range(_SSM_SUB):
            xr_ref[s, across, :] = xr
            xi_ref[s, across, :] = xi
            xr, xi = (atr * xr - ati * xi + sr_ref[s, across, :], atr * xi + ati * xr + si_ref[s, across, :])
        return xr, xi

    xr, xi = lax.fori_loop(0, nk, step, (ire_ref[0], iim_ref[0]))
    fre_ref[0] = xr
    fim_ref[0] = xi
    ys = []
    for s in range(_SSM_SUB):
        unpitch = lambda ref: jnp.concatenate([ref[s, b * pitch:b * pitch + nk, :] for b in range(nb)], axis=0)
        ys.append(yi_ref[s] + _dot(unpitch(xr_ref).astype(BF16), wre_ref[0]) + _dot(unpitch(xi_ref).astype(BF16), wim_ref[0])
                  + d_ref[0] * lhs[s])
    y3 = jnp.stack([jnp.concatenate([y[:, c * SSM_T:(c + 1) * SSM_T] for y in ys], axis=1) for c in range(SSM_GROUP)])
    y_ref[...] = y3.reshape(SSM_GROUP, rows * SSM_BLOCK)


def _ssm_long(ut, mats, init_re, init_im, nb, nk):
    tzv, wre, wim, atr, ati, drow = mats
    g_n = tzv.shape[0]
    n = ut.shape[1]
    rows = nb * nk
    per_g = lambda a: pl.BlockSpec((1,) + a.shape[1:], lambda g: (g, 0, 0))
    chan = pl.BlockSpec((SSM_GROUP, n), lambda g: (g, 0))
    st = jax.ShapeDtypeStruct((g_n, nb, LANES), F32)
    prows = nb * (nk + _SSM_PITCH)
    return pl.pallas_call(
        functools.partial(_ssm_long_kernel, nb, nk),
        grid=(g_n,),
        in_specs=[chan] + [per_g(a) for a in (tzv, wre, wim, atr, ati, drow, init_re, init_im)],
        out_specs=[chan, per_g(init_re), per_g(init_re)],
        out_shape=[jax.ShapeDtypeStruct(ut.shape, F32), st, st],
        scratch_shapes=[pltpu.VMEM((_SSM_SUB, rows, SSM_T * SSM_GROUP), F32)]
        + [pltpu.VMEM((_SSM_SUB, prows, LANES), F32)] * 4,
        compiler_params=_params(("parallel",)),
        name="ssm_long",
    )(ut, tzv, wre, wim, atr, ati, drow, init_re, init_im)


def _ssm_branch_long(ut, nb, mats, st_re, st_im):
    p_n = st_re.shape[2]
    pad = ((0, 0), (0, 0), (0, LANES - p_n))
    ire = jnp.pad(st_re.transpose(1, 0, 2), pad)
    iim = jnp.pad(st_im.transpose(1, 0, 2), pad)
    yt, fre, fim = _ssm_long(ut, mats, ire, iim, nb, ut.shape[1] // (nb * SSM_BLOCK))
    return yt, fre[:, :, :p_n].transpose(1, 0, 2), fim[:, :, :p_n].transpose(1, 0, 2)


_DSA_TQ = 256
_DSA_VALUE_STEPS = 24
_DSA_SK = 1024
_DSA_MIN_SUM = 2.0 ** -80


def _dsa_kernel(tq, tv, sk, pos0, s_valid, n_sel, qlat_ref, qidx_ref, w_ref, ka_ref, c_ref, cmax_ref, wuv_ref,
                o_ref, key_ref, bias_ref, j_ref):
    s_pad = key_ref.shape[0]
    q0 = pos0 + pl.program_id(1) * tq
    qpos = q0 + lax.broadcasted_iota(I32, (1, tq), 1)
    vis = jnp.minimum((qpos // CHUNK + 1) * CHUNK, s_valid)
    vis_max = jnp.minimum(((q0 + tq - 1) // CHUNK + 1) * CHUNK, s_valid)
    nch = (vis_max + sk - 1) // sk
    kpos = lax.broadcasted_iota(I32, (sk, tq), 0)
    kslice = lambda j: pl.ds(pl.multiple_of(j * sk, sk), sk)
    fold = lambda a: a.reshape(sk // 8, 8, tq)

    qs = jnp.concatenate([qidx_ref[0, :, hd * LANES:(hd + 1) * LANES] for hd in range(IDX_HEADS)], axis=0)

    def score_chunk(j, carry):
        kmin, kmax = carry
        rel = jnp.maximum(_dot_nt(ka_ref[0, kslice(j), :], qs), 0.0)
        acc = jnp.zeros((sk, tq), F32)
        for hd in range(IDX_HEADS):
            acc = acc + w_ref[0, hd:hd + 1, :] * rel[:, hd * tq:(hd + 1) * tq]
        bits = pltpu.bitcast(acc, I32)
        key = bits ^ ((bits >> 31) & 0x7FFFFFFF)
        visible = j * sk + kpos < vis
        key_ref[kslice(j), :] = jnp.where(visible, key, INT_MIN)
        kmin = jnp.minimum(kmin, jnp.min(fold(jnp.where(visible, key, INT_MAX)), axis=0))
        kmax = jnp.maximum(kmax, jnp.max(fold(jnp.where(visible, key, INT_MIN)), axis=0))
        return kmin, kmax

    kmin, kmax = lax.fori_loop(0, nch, score_chunk,
                               (jnp.full((8, tq), INT_MAX, I32), jnp.full((8, tq), INT_MIN, I32)))
    kmin = jnp.min(kmin, axis=0, keepdims=True)
    kmax = jnp.max(kmax, axis=0, keepdims=True)

    def count(pred):
        def body(j, c):
            hit = jnp.where(pred(key_ref[kslice(j), :], j * sk + kpos), 1.0, 0.0)
            return c + jnp.sum(fold(hit), axis=0)
        return jnp.sum(lax.fori_loop(0, nch, body, jnp.zeros((8, tq), F32)), axis=0, keepdims=True)

    key_of = lambda v: (lambda b: b ^ ((b >> 31) & 0x7FFFFFFF))(pltpu.bitcast(v, I32))
    val_of = lambda k: pltpu.bitcast(k ^ ((k >> 31) & 0x7FFFFFFF), F32)

    def settled(lo, hi, cnt):
        return (cnt <= n_sel) | (hi - 1 <= lo)

    def halve(it, lo, hi, cnt):
        mid_v = key_of(0.5 * val_of(lo) + 0.5 * val_of(hi))
        mid_k = (lo >> 1) + (hi >> 1) + (lo & hi & 1)
        mid = jnp.where((mid_v > lo) & (mid_v < hi) & (it < _DSA_VALUE_STEPS), mid_v, mid_k)
        c = count(lambda k, col: k >= mid)
        live = jnp.logical_not(settled(lo, hi, cnt))
        up = live & (c >= n_sel)
        return jnp.where(up, mid, lo), jnp.where(live & (c < n_sel), mid, hi), jnp.where(up, c, cnt)

    def halve_twice(state):
        it, lo, hi, cnt, _ = state
        lo, hi, cnt = halve(it, lo, hi, cnt)
        lo, hi, cnt = halve(it + 1, lo, hi, cnt)
        return it + 2, lo, hi, cnt, jnp.max(jnp.where(settled(lo, hi, cnt), 0, 1))

    c_pos, c_nn = count(lambda k, col: k >= 1), count(lambda k, col: k >= 0)
    pos, zero = c_pos >= n_sel, c_nn >= n_sel
    lo0 = jnp.where(pos, 1, jnp.where(zero, 0, kmin))
    hi0 = jnp.where(pos, jnp.where(kmax == INT_MAX, INT_MAX, kmax + 1), jnp.where(zero, 1, 0))
    cnt0 = jnp.where(pos, c_pos, jnp.where(zero, c_nn, vis.astype(F32)))
    state = (jnp.int32(0), lo0, hi0, cnt0, jnp.max(jnp.where(settled(lo0, hi0, cnt0), 0, 1)))
    _, thr, _, cnt, _ = lax.while_loop(lambda st: (st[0] < _DSA_VALUE_STEPS + 34) & (st[4] > 0), halve_twice, state)
    thr = jnp.maximum(thr, INT_MIN + 1)
    excess = cnt > n_sel

    j_ref[...] = jnp.full(j_ref.shape, s_pad, I32)

    @pl.when(jnp.max(jnp.where(excess, 1.0, 0.0)) > 0.0)
    def _():
        need = n_sel - count(lambda k, col: k > thr)

        def idx_step(_, lohi):
            lo, hi = lohi
            mid = (lo + hi) >> 1
            ok = count(lambda k, col: (k == thr) & (col < mid)) >= need
            return jnp.where(ok, lo, mid), jnp.where(ok, mid, hi)

        steps = int(math.ceil(math.log2(s_pad))) + 1
        _, hi = lax.fori_loop(0, steps, idx_step, (jnp.zeros((1, tq), I32), jnp.full((1, tq), s_pad, I32)))
        j_ref[...] = jnp.broadcast_to(jnp.where(excess, hi, s_pad), j_ref.shape)

    jlim = j_ref[0:1, :]

    rows = DSA_HEADS * tv
    q = jnp.concatenate([qlat_ref[0, :tv, hd * DSA_LATENT:(hd + 1) * DSA_LATENT] for hd in range(DSA_HEADS)], axis=0)

    wb = min(sk, 2 * LANES)

    kpos_wb = lax.broadcasted_iota(I32, (wb, tq), 0)

    def logits(j, i, first):
        blk = pl.ds(pl.multiple_of(j * sk + i * wb, wb), wb)
        if first:
            k = key_ref[blk, :]
            sel = (k > thr) | ((k == thr) & (j * sk + i * wb + kpos_wb < jlim))
            bias = jnp.where(sel, 0.0, NEG_BIG).T[:tv]
            bias_ref[:, blk] = bias
        else:
            bias = bias_ref[:, blk]
        s = _dot_nt(q, c_ref[0, blk, :])
        return (s.reshape(DSA_HEADS, tv, wb) + bias[None]).reshape(rows, wb)

    def attend(m, first):
        def acc_chunk(j, carry):
            l, acc = carry
            ps = []
            for i in range(sk // wb):
                s = logits(j, i, first)
                for t in range(wb // LANES):
                    p = jnp.exp2(s[:, t * LANES:(t + 1) * LANES] - m)
                    l = l + p
                    ps.append(p.astype(BF16))
            return l, acc + _dot(jnp.concatenate(ps, axis=1), c_ref[0, kslice(j), :])

        l, acc = lax.fori_loop(0, nch, acc_chunk, (jnp.zeros((rows, LANES), F32), jnp.zeros((rows, DSA_LATENT), F32)))
        return jnp.sum(l, axis=1, keepdims=True), acc

    def emit(l, acc):
        o = (acc / l).astype(BF16)
        for hd in range(DSA_HEADS):
            o_ref[0, :tv, hd * DSA_HEAD_DIM:(hd + 1) * DSA_HEAD_DIM] = _dot(o[hd * tv:(hd + 1) * tv], wuv_ref[hd]).astype(BF16)
        if tv < tq:
            o_ref[0, tv:, :] = jnp.zeros((tq - tv, o_ref.shape[2]), BF16)

    qf = q.astype(F32)
    bound = jnp.sqrt(jnp.sum(qf * qf, axis=1, keepdims=True)) * cmax_ref[0, 0:1, 0:1]
    l, acc = attend(jnp.broadcast_to(bound, (rows, LANES)), True)
    healthy = jnp.min(l) > _DSA_MIN_SUM

    @pl.when(healthy)
    def _():
        emit(l, acc)

    @pl.when(jnp.logical_not(healthy))
    def _():
        def max_chunk(j, mx):
            for i in range(sk // wb):
                s = logits(j, i, False)
                for t in range(wb // LANES):
                    mx = jnp.maximum(mx, s[:, t * LANES:(t + 1) * LANES])
            return mx

        mx = lax.fori_loop(0, nch, max_chunk, jnp.full((rows, LANES), NEG_BIG, F32))
        emit(*attend(jnp.broadcast_to(jnp.max(mx, axis=1, keepdims=True), (rows, LANES)), False))


def _dsa(qlat, qidx, w, ka, c, wuv, pos0, s_valid, n_sel, tq):
    b, l_true, _ = qlat.shape
    l = -(-l_true // tq) * tq
    qlat, qidx, w = (jnp.pad(a, ((0, 0), (0, l - l_true), (0, 0))) for a in (qlat, qidx, w))
    wt = w.transpose(0, 2, 1)
    s_pad = c.shape[1]
    sk = min(_DSA_SK, s_pad)
    cf = c.astype(F32)
    cmax = jnp.broadcast_to(jnp.sqrt(jnp.max(jnp.sum(cf * cf, axis=2), axis=1))[:, None, None], (b, 1, LANES))
    qspec = lambda a: pl.BlockSpec((1, tq, a.shape[2]), lambda i, j: (i, j, 0))
    kspec = lambda a: pl.BlockSpec((1,) + a.shape[1:], lambda i, j: (i, 0, 0))
    dh = wuv.shape[0] * wuv.shape[2]
    return pl.pallas_call(
        functools.partial(_dsa_kernel, tq, min(tq, l_true), sk, pos0, s_valid, n_sel),
        grid=(b, l // tq),
        in_specs=[qspec(qlat), qspec(qidx), pl.BlockSpec((1, wt.shape[1], tq), lambda i, j: (i, 0, j)), kspec(ka), kspec(c),
                  kspec(cmax), _const_spec(wuv.shape)],
        out_specs=pl.BlockSpec((1, tq, dh), lambda i, j: (i, j, 0)),
        out_shape=jax.ShapeDtypeStruct((b, l, dh), BF16),
        scratch_shapes=[pltpu.VMEM((s_pad, tq), I32), pltpu.VMEM((min(tq, l_true), s_pad), F32), pltpu.VMEM((8, tq), I32)],
        compiler_params=_params(("parallel", "arbitrary")),
        name="dsa",
    )(qlat, qidx, wt, ka, c, cmax, wuv)[:, :l_true]


_PEER_WIDTH = [PEER_TOPK // (r + 1) for r in range(PEER_TOPK)]


def _top_rows(s, k):
    row = lax.broadcasted_iota(I32, s.shape, 0)
    vals, idxs = [], []
    for _ in range(k):
        m = jnp.max(s, axis=0, keepdims=True)
        first = jnp.min(jnp.where(s == m, row, s.shape[0]), axis=0, keepdims=True)
        s = jnp.where(row == first, -jnp.inf, s)
        vals.append(m)
        idxs.append(first)
    return jnp.concatenate(vals, axis=0), jnp.concatenate(idxs, axis=0)


def _peer_route_kernel(h_ref, g_ref, wq_ref, k1_ref, k2_ref, xn_ref, i1_ref, i2_ref, gate_ref):
    tn = h_ref.shape[0]
    xn = _rms(h_ref[...], g_ref[...]).astype(BF16)
    xn_ref[...] = xn
    q = _dot(xn, wq_ref[...]).astype(BF16)
    oute, outg = [], []
    for hd in range(PEER_HEADS):
        qa = q[:, (2 * hd) * PEER_HALF:(2 * hd + 1) * PEER_HALF]
        qb = q[:, (2 * hd + 1) * PEER_HALF:(2 * hd + 2) * PEER_HALF]
        v1, i1 = _top_rows(_dot_nt(k1_ref[hd], qa), PEER_TOPK)
        v2, i2 = _top_rows(_dot_nt(k2_ref[hd], qb), PEER_TOPK)
        cand = jnp.concatenate([v1[r:r + 1] + v2[0:w] for r, w in enumerate(_PEER_WIDTH)], axis=0)
        cexp = jnp.concatenate([i1[r:r + 1] * PEER_NKEYS + i2[0:w] for r, w in enumerate(_PEER_WIDTH)], axis=0)
        row = lax.broadcasted_iota(I32, cand.shape, 0)
        tops, experts = [], []
        for _ in range(PEER_TOPK):
            m = jnp.max(cand, axis=0, keepdims=True)
            first = jnp.min(jnp.where(cand == m, row, cand.shape[0]), axis=0, keepdims=True)
            hit = row == first
            experts.append(jnp.sum(jnp.where(hit, cexp, 0), axis=0, keepdims=True))
            cand = jnp.where(hit, -jnp.inf, cand)
            tops.append(m)
        top = jnp.concatenate(tops, axis=0)
        ex = jnp.exp(top - top[0:1])
        outg.append(ex / jnp.sum(ex, axis=0, keepdims=True))
        oute.append(jnp.concatenate(experts, axis=0))
    expert = jnp.concatenate(oute, axis=0)
    as_rows = lambda a: pltpu.bitcast(pltpu.bitcast(a, F32).T, I32)
    i1_ref[...] = as_rows(expert >> 7)
    i2_ref[...] = as_rows(expert & (PEER_NKEYS - 1))
    gate_ref[...] = jnp.concatenate(outg, axis=0).T


def _peer_route(h, g, wq, k1, k2):
    n, d = h.shape
    tn = LANES
    slots = PEER_HEADS * PEER_TOPK
    row = lambda w: pl.BlockSpec((tn, w), lambda i: (i, 0))
    return pl.pallas_call(
        _peer_route_kernel,
        grid=(n // tn,),
        in_specs=[row(d), _const_spec(g.shape), _const_spec(wq.shape), _const_spec(k1.shape), _const_spec(k2.shape)],
        out_specs=[row(d), row(slots), row(slots), row(slots)],
        out_shape=[jax.ShapeDtypeStruct((n, d), BF16), jax.ShapeDtypeStruct((n, slots), I32),
                   jax.ShapeDtypeStruct((n, slots), I32), jax.ShapeDtypeStruct((n, slots), F32)],
        compiler_params=_params(("parallel",)),
        name="peer_route",
    )(h, g, wq, k1, k2)


_PLANE_PAD = 4


def _peer_gates_kernel(i1_ref, i2_ref, gate_ref, a_ref, planes_ref):
    tn, slots = i1_ref.shape
    plane = tn + _PLANE_PAD
    sub = lax.broadcasted_iota(I32, (PEER_NKEYS, slots), 0)

    def token(n, carry):
        r = pl.ds(n, 1)
        pt = jnp.where(sub == i1_ref[r, :], gate_ref[r, :], 0.0).astype(BF16)
        qt = jnp.where(sub == i2_ref[r, :], 1.0, 0.0).astype(BF16)
        planes_ref[pl.ds(n, PEER_NKEYS, stride=plane), :] = _dot_nt(pt, qt)
        return carry

    lax.fori_loop(0, tn, token, 0, unroll=True)
    for k in range(PEER_NKEYS):
        a_ref[:, k * PEER_NKEYS:(k + 1) * PEER_NKEYS] = planes_ref[pl.ds(k * plane, tn), :].astype(BF16)


def _peer_gates(i1, i2, gate):
    n, slots = i1.shape
    tn = _pick_tile(n, LANES)
    ne = PEER_NKEYS * PEER_NKEYS
    row = lambda w: pl.BlockSpec((tn, w), lambda i: (i, 0))
    return pl.pallas_call(
        _peer_gates_kernel,
        grid=(n // tn,),
        in_specs=[row(slots)] * 3,
        out_specs=row(ne),
        out_shape=jax.ShapeDtypeStruct((n, ne), BF16),
        scratch_shapes=[pltpu.VMEM((PEER_NKEYS * (tn + _PLANE_PAD), PEER_NKEYS), F32)],
        compiler_params=_params(("parallel",)),
        name="peer_gates",
    )(i1, i2, gate)


_PEER_SUB = 256
def _peer_apply_kernel(final_norm, xn_ref, a_ref, u_ref, v_ref, h_ref, g_ref, o_ref, acc_ref):
    j = pl.program_id(1)

    @pl.when(j == 0)
    def _():
        acc_ref[...] = jnp.zeros_like(acc_ref)

    xn = xn_ref[...]
    ws = []
    for s in range(u_ref.shape[0] // _PEER_SUB):
        sub = slice(s * _PEER_SUB, (s + 1) * _PEER_SUB)
        act = _gelu(_dot_nt(xn, u_ref[sub, :]))
        ws.append((act * a_ref[:, sub].astype(F32)).astype(BF16))
    acc_ref[...] += _dot(jnp.concatenate(ws, axis=1), v_ref[...])

    @pl.when(j == pl.num_programs(1) - 1)
    def _():
        y = h_ref[...] + acc_ref[...]
        o_ref[...] = _rms(y, g_ref[...]) if final_norm else y


def _peer_apply(xn, a, u, v, h, g, final_norm):
    n, d = xn.shape
    ne = u.shape[0]
    tn = _pick_tile(n, ROWS_NARROW)
    te = 8 * _PEER_SUB
    return pl.pallas_call(
        functools.partial(_peer_apply_kernel, final_norm),
        grid=(n // tn, ne // te),
        in_specs=[pl.BlockSpec((tn, d), lambda i, j: (i, 0)), pl.BlockSpec((tn, te), lambda i, j: (i, j)),
                  pl.BlockSpec((te, d), lambda i, j: (j, 0)), pl.BlockSpec((te, d), lambda i, j: (j, 0)),
                  pl.BlockSpec((tn, d), lambda i, j: (i, 0)), _const_spec(g.shape)],
        out_specs=pl.BlockSpec((tn, d), lambda i, j: (i, 0)),
        out_shape=jax.ShapeDtypeStruct((n, d), F32),
        scratch_shapes=[pltpu.VMEM((tn, d), F32)],
        compiler_params=_params(("parallel", "arbitrary")),
        name="peer_apply",
    )(xn, a, u, v, h, g)


def _split_w_in(w_in, d):
    widths = (d, DSA_HEADS * DSA_LATENT, DSA_LATENT, IDX_HEADS * IDX_DIM, IDX_DIM, IDX_HEADS,
              MEM_HEADS * MEM_HEAD_DIM, N_BRANCH * d)
    offs = np.cumsum((0,) + widths)
    wu, wq, wc, wqi, wki, wwi, wmq, wg = [w_in[:, offs[i]:offs[i + 1]] for i in range(8)]
    wkw = jnp.pad(jnp.concatenate([wki, wwi], axis=1), ((0, 0), (0, LANES - IDX_DIM - IDX_HEADS)))
    kw_scale = jnp.concatenate([jnp.ones((IDX_DIM,), F32), jnp.full((IDX_HEADS,), IDX_HEADS ** -0.5, F32),
                                jnp.zeros((LANES - IDX_DIM - IDX_HEADS,), F32)])[None, :]
    wqi = jnp.pad(wqi.reshape(-1, IDX_HEADS, IDX_DIM), ((0, 0), (0, 0), (0, LANES - IDX_DIM))).reshape(-1, IDX_HEADS * LANES)
    ws = [wu.T, wq * (DSA_LATENT ** -0.5 * LOG2E), wc, wqi, wkw, wmq * MEM_HEAD_DIM ** -0.5, wg]
    return [w.astype(BF16) for w in ws], kw_scale


def _pad_rows(a, n):
    return a if a.shape[0] == n else jnp.pad(a, ((0, n - a.shape[0]),) + ((0, 0),) * (a.ndim - 1))


def _pad_keys(a, s_pad):
    return jnp.pad(a, ((0, 0), (0, s_pad - a.shape[1]), (0, 0)))


def _token_mix(h, seq_shape, lw, mats, state, cache, mem_kv, pos0):
    b, l = seq_shape
    bf = lambda a: a.astype(BF16)
    seq = lambda a: a.reshape(b, l, -1)
    ut, q, c, c16, qi, kw, k16, mq, g = _inproj(h, lw["g_norm1"], lw["g_kv"], lw["kw_scale"], lw["w_in"])
    ki, wi = kw[:, :IDX_DIM], kw[:, IDX_DIM:IDX_DIM + IDX_HEADS]

    if l % SSM_BLOCK == 0:
        yst, s_re, s_im = _ssm_branch_long(ut, b, mats, state[0], state[1])
    else:
        ys, s_re, s_im = _ssm_branch(seq(bf(ut.T)), mats, state[0], state[1])
        yst = ys.reshape(b * l, -1).astype(F32).T

    c_all, k_all = seq(c16), seq(k16)
    if cache is not None:
        lane_pad = ((0, 0), (0, 0), (0, LANES - IDX_DIM))
        c_all = jnp.concatenate([bf(cache[0]), c_all], axis=1)
        k_all = jnp.concatenate([jnp.pad(bf(cache[1]), lane_pad), k_all], axis=1)
    s_all = c_all.shape[1]
    s_pad = -(-s_all // _DSA_SK) * _DSA_SK if cache is not None else s_all
    yd = _dsa(seq(q), seq(qi), seq(wi), _pad_keys(k_all, s_pad), _pad_keys(c_all, s_pad), lw["w_uv"], pos0, s_all,
              min(DSA_TOPK, s_all // 4), _pick_tile(max(l, LANES), _DSA_TQ))

    ym = _memattn(seq(mq), mem_kv[0], mem_kv[1])

    flat = lambda a: a.reshape(b * l, -1)
    h2 = _merge(h, yst, flat(yd), flat(ym), g, lw["w_glu"], lw["b_glu"], lw["w_br_ssm"], lw["w_br_dsa"],
                lw["w_br_mem"], lw["w_out"])
    return h2, seq(c), seq(ki), s_re, s_im


def _channel_mix(h2, lw, g_final, final_norm):
    n = h2.shape[0]
    h2 = _pad_rows(h2, -(-n // LANES) * LANES)
    xn, i1, i2, gate = _peer_route(h2, lw["g_norm2"], lw["w_peer_q"], lw["peer_sub_k1"], lw["peer_sub_k2"])
    a = _peer_gates(i1, i2, gate)
    return _peer_apply(xn, a, lw["peer_u"], lw["peer_v"], h2, g_final, final_norm)[:n]


def kernel(x_prompt, x_sample, mem_prompt, cache_dsa_latent, cache_dsa_idx_k, state_ssm_re, state_ssm_im, cache_mem_k, cache_mem_v, g_norm1, w_in, g_kv, w_uv, ssm_lam_re, ssm_lam_im, ssm_log_dt, ssm_b_re, ssm_b_im, ssm_c_re, ssm_c_im, ssm_d, w_glu, b_glu, g_mem, w_mem_kv, w_br_ssm, w_br_dsa, w_br_mem, w_out, g_norm2, w_peer_q, peer_sub_k1, peer_sub_k2, peer_u, peer_v, g_final):
    depth = w_in.shape[0]
    bp, lp, d = x_prompt.shape
    bs, ls, _ = x_sample.shape
    past = cache_dsa_latent.shape[2]
    mem_w = MEM_HEADS * MEM_HEAD_DIM
    n_mem = mem_prompt.shape[1]
    heads = (n_mem, MEM_HEADS, MEM_HEAD_DIM)
    bf = lambda a: a.astype(BF16)
    row = lambda a: a[None, :]

    hp, hs = x_prompt.reshape(bp * lp, d), x_sample.reshape(bs * ls, d)
    outs = [[] for _ in range(10)]
    for l in range(depth):
        ws, kw_scale = _split_w_in(w_in[l], d)
        lw = dict(g_norm1=row(g_norm1[l]), g_kv=row(g_kv[l]), kw_scale=kw_scale, w_in=ws, w_uv=bf(w_uv[l]),
                  w_glu=bf(w_glu[l]), b_glu=row(b_glu[l]), w_br_ssm=bf(w_br_ssm[l]), w_br_dsa=bf(w_br_dsa[l]),
                  w_br_mem=bf(w_br_mem[l]), w_out=bf(w_out[l]), g_norm2=row(g_norm2[l]), w_peer_q=bf(w_peer_q[l]),
                  peer_sub_k1=bf(peer_sub_k1[l]), peer_sub_k2=bf(peer_sub_k2[l]), peer_u=bf(peer_u[l]),
                  peer_v=bf(peer_v[l]))
        mats = _ssm_matrices(ssm_lam_re[l], ssm_lam_im[l], ssm_log_dt[l], ssm_b_re[l], ssm_b_im[l],
                             ssm_c_re[l], ssm_c_im[l], ssm_d[l], SSM_T)
        kv = _memkv(mem_prompt.reshape(bp * n_mem, d), row(g_mem[l]), bf(w_mem_kv[l]))
        mk_p, mv_p = kv[:, :mem_w].reshape(bp, n_mem, mem_w), kv[:, mem_w:].reshape(bp, n_mem, mem_w)
        zeros = jnp.zeros((bp,) + state_ssm_re.shape[2:], F32)

        h2p, c_p, ki_p, sre_p, sim_p = _token_mix(hp, (bp, lp), lw, mats, (zeros, zeros), None, (bf(mk_p), bf(mv_p)), 0)
        mem_s = (bf(cache_mem_k[l].reshape(bs, n_mem, mem_w)), bf(cache_mem_v[l].reshape(bs, n_mem, mem_w)))
        h2s, c_s, ki_s, sre_s, sim_s = _token_mix(hs, (bs, ls), lw, mats, (state_ssm_re[l], state_ssm_im[l]),
                                                  (cache_dsa_latent[l], cache_dsa_idx_k[l]), mem_s, past)
        final = l == depth - 1
        hp = _channel_mix(h2p, lw, row(g_final), final)
        hs = _channel_mix(h2s, lw, row(g_final), final)
        for lst, val in zip(outs, (c_p, ki_p, sre_p, sim_p, mk_p.reshape((bp,) + heads), mv_p.reshape((bp,) + heads),
                                   c_s, ki_s, sre_s, sim_s)):
            lst.append(val)
    return (hp.reshape(bp, lp, d), hs.reshape(bs, ls, d)) + tuple(jnp.stack(o) for o in outs)
```

```python
import functools
import math

import jax
import jax.numpy as jnp
import numpy as np
from jax import lax
from jax.experimental import pallas as pl
from jax.experimental.pallas import tpu as pltpu

F32 = jnp.float32
BF16 = jnp.bfloat16
I32 = jnp.int32

EPS = 1e-6
CHUNK = 64
SSM_GROUP = 16
SSM_STATE = 64
SSM_T = 32
DSA_HEADS = 8
DSA_LATENT = 256
DSA_HEAD_DIM = 128
IDX_HEADS = 8
IDX_DIM = 64
DSA_TOPK = 256
MEM_HEADS = 4
MEM_HEAD_DIM = 256
PEER_HEADS = 8
PEER_NKEYS = 128
PEER_HALF = 128
PEER_TOPK = 16
N_BRANCH = 3

LANES = 128
VMEM_LIMIT = 56 * 1024 * 1024
ROWS_WIDE = 256
ROWS_NARROW = 512
INT_MIN = -2 ** 31
INT_MAX = 2 ** 31 - 1
NEG_BIG = -1e30
LOG2E = 1.4426950408889634


def _pick_tile(n, target):
    if n <= target:
        return n
    for t in range(target, 7, -1):
        if n % t == 0 and t % 8 == 0:
            return t
    return n


def _params(sem):
    return pltpu.CompilerParams(dimension_semantics=sem, vmem_limit_bytes=VMEM_LIMIT)


def _const_spec(shape):
    nd = len(shape)
    return pl.BlockSpec(shape, lambda *_: (0,) * nd, pipeline_mode=pl.Buffered(1))


def _rms(x, g):
    return x * lax.rsqrt(jnp.mean(x * x, axis=-1, keepdims=True) + EPS) * g


def _gelu(x):
    return 0.5 * x * (1.0 + jnp.tanh(0.7978845608028654 * (x + 0.044715 * x * x * x)))


def _sigmoid(x):
    return 1.0 / (1.0 + jnp.exp(-x))


def _dot_nt(a, b):
    return lax.dot_general(a, b, (((1,), (1,)), ((), ())), preferred_element_type=F32)


def _dot(a, b):
    return jnp.dot(a, b, preferred_element_type=F32)


def _inproj_kernel(x_ref, g1_ref, gkv_ref, kws_ref, wu_ref, wq_ref, wc_ref, wqi_ref, wkw_ref, wmq_ref, wg_ref,
                   u_ref, q_ref, c_ref, c16_ref, qi_ref, kw_ref, k16_ref, mq_ref, g_ref):
    xn = _rms(x_ref[...], g1_ref[...]).astype(BF16)
    u_ref[...] = _dot_nt(wu_ref[...], xn)
    q_ref[...] = _dot(xn, wq_ref[...]).astype(BF16)
    c = _rms(_dot(xn, wc_ref[...]), gkv_ref[...])
    c_ref[...] = c
    c16_ref[...] = c.astype(BF16)
    qi_ref[...] = _dot(xn, wqi_ref[...]).astype(BF16)
    kw = _dot(xn, wkw_ref[...]) * kws_ref[...]
    kw_ref[...] = kw
    is_key = lax.broadcasted_iota(I32, kw.shape, 1) < IDX_DIM
    k16_ref[...] = jnp.where(is_key, kw, 0.0).astype(BF16)
    mq_ref[...] = _dot(xn, wmq_ref[...]).astype(BF16)
    g_ref[...] = _sigmoid(_dot(xn, wg_ref[...])).astype(BF16)


def _inproj(h, g1, gkv, kw_scale, ws):
    n, d = h.shape
    tm = _pick_tile(n, ROWS_WIDE)
    wu, wq, wc, wqi, wkw, wmq, wg = (w.shape[1] for w in ws)
    wu = ws[0].shape[0]
    outs = [(wq, BF16), (wc, F32), (wc, BF16), (wqi, BF16), (wkw, F32), (wkw, BF16), (wmq, BF16), (wg, BF16)]
    row = lambda w: pl.BlockSpec((tm, w), lambda i: (i, 0))
    return pl.pallas_call(
        _inproj_kernel,
        grid=(n // tm,),
        in_specs=[row(d), _const_spec(g1.shape), _const_spec(gkv.shape), _const_spec(kw_scale.shape)]
        + [_const_spec(w.shape) for w in ws],
        out_specs=[pl.BlockSpec((wu, tm), lambda i: (0, i))] + [row(w) for w, _ in outs],
        out_shape=[jax.ShapeDtypeStruct((wu, n), F32)] + [jax.ShapeDtypeStruct((n, w), dt) for w, dt in outs],
        compiler_params=_params(("parallel",)),
        name="inproj",
    )(h, g1, gkv, kw_scale, *ws)


def _memkv_kernel(x_ref, g_ref, w_ref, o_ref):
    xn = _rms(x_ref[...], g_ref[...]).astype(BF16)
    o_ref[...] = _dot(xn, w_ref[...])


def _memkv(mem, g, w):
    n, d = mem.shape
    tm = _pick_tile(n, ROWS_WIDE)
    return pl.pallas_call(
        _memkv_kernel,
        grid=(n // tm,),
        in_specs=[pl.BlockSpec((tm, d), lambda i: (i, 0)), _const_spec(g.shape), _const_spec(w.shape)],
        out_specs=pl.BlockSpec((tm, w.shape[1]), lambda i: (i, 0)),
        out_shape=jax.ShapeDtypeStruct((n, w.shape[1]), F32),
        compiler_params=_params(("parallel",)),
        name="memkv",
    )(mem, g, w)


def _memattn_kernel(q_ref, k_ref, v_ref, o_ref):
    for hd in range(MEM_HEADS):
        sl = slice(hd * MEM_HEAD_DIM, (hd + 1) * MEM_HEAD_DIM)
        logits = _dot_nt(q_ref[0, :, sl], k_ref[0, :, sl])
        m = jnp.max(logits, axis=-1, keepdims=True)
        p = jnp.exp(logits - m)
        l = jnp.sum(p, axis=-1, keepdims=True)
        o = _dot(p.astype(BF16), v_ref[0, :, sl]) / l
        o_ref[0, :, sl] = o.astype(BF16)


def _memattn(q, k, v):
    b, l, w = q.shape
    tl = _pick_tile(l, ROWS_NARROW)
    nm = k.shape[1]
    return pl.pallas_call(
        _memattn_kernel,
        grid=(b, l // tl),
        in_specs=[pl.BlockSpec((1, tl, w), lambda i, j: (i, j, 0)),
                  pl.BlockSpec((1, nm, w), lambda i, j: (i, 0, 0)),
                  pl.BlockSpec((1, nm, w), lambda i, j: (i, 0, 0))],
        out_specs=pl.BlockSpec((1, tl, w), lambda i, j: (i, j, 0)),
        out_shape=jax.ShapeDtypeStruct((b, l, w), BF16),
        compiler_params=_params(("parallel", "parallel")),
        name="memattn",
    )(q, k, v)


def _merge_kernel(h_ref, ys_ref, yd_ref, ym_ref, g_ref, wglu_ref, bglu_ref, wbs_ref, wbd_ref, wbm_ref, wout_ref,
                  o_ref):
    d = h_ref.shape[1]
    z = _gelu(ys_ref[...].T)
    gate = _sigmoid(_dot(z.astype(BF16), wglu_ref[...]) + bglu_ref[...])
    a = _dot((z * gate).astype(BF16), wbs_ref[...])
    b = _dot(yd_ref[...], wbd_ref[...])
    c = _dot(ym_ref[...], wbm_ref[...])
    g = g_ref[...].astype(F32)
    merged = g[:, 0:d] * a + g[:, d:2 * d] * b + g[:, 2 * d:3 * d] * c
    o_ref[...] = h_ref[...] + _dot(merged.astype(BF16), wout_ref[...])


def _merge(h, ys, yd, ym, g, wglu, bglu, wbs, wbd, wbm, wout):
    n, d = h.shape
    tm = _pick_tile(n, ROWS_NARROW)
    row = lambda w: pl.BlockSpec((tm, w), lambda i: (i, 0))
    consts = [wglu, bglu, wbs, wbd, wbm, wout]
    return pl.pallas_call(
        _merge_kernel,
        grid=(n // tm,),
        in_specs=[row(d), pl.BlockSpec((d, tm), lambda i: (0, i)), row(d), row(d), row(3 * d)]
        + [_const_spec(c.shape) for c in consts],
        out_specs=row(d),
        out_shape=jax.ShapeDtypeStruct((n, d), F32),
        compiler_params=_params(("parallel",)),
        name="merge",
    )(h, ys, yd, ym, g, *consts)


def _ssm_matrices(lam_re, lam_im, log_dt, b_re, b_im, c_re, c_im, d, t_len):
    hi = lax.Precision.HIGHEST
    g_n, p_n = lam_re.shape
    dt = jnp.exp(log_dt)[:, None]
    mag = jnp.exp(lam_re * dt)
    ar, ai = mag * jnp.cos(lam_im * dt), mag * jnp.sin(lam_im * dt)
    den = lam_re * lam_re + lam_im * lam_im
    nr, ni = ar - 1.0, ai
    kr = ((nr * lam_re + ni * lam_im) / den)[..., None]
    ki = ((ni * lam_re - nr * lam_im) / den)[..., None]
    bbr, bbi = kr * b_re - ki * b_im, kr * b_im + ki * b_re
    j = jnp.arange(t_len + 1, dtype=F32)[:, None, None]
    pmag = jnp.exp(j * (lam_re * dt))
    pr, pi = pmag * jnp.cos(j * (lam_im * dt)), pmag * jnp.sin(j * (lam_im * dt))
    mr = pr[:t_len, ..., None] * bbr - pi[:t_len, ..., None] * bbi
    mi = pr[:t_len, ..., None] * bbi + pi[:t_len, ..., None] * bbr
    kern = (jnp.einsum('gdp,jgpc->jgdc', c_re, mr, precision=hi)
            - jnp.einsum('gdp,jgpc->jgdc', c_im, mi, precision=hi))
    s_i = jnp.arange(t_len)[:, None]
    t_i = jnp.arange(t_len)[None, :]
    lag = t_i - s_i
    kg = jnp.where((lag >= 0)[:, :, None, None, None], kern[jnp.clip(lag, 0)], 0.0)
    tz = kg.transpose(2, 4, 0, 3, 1).reshape(g_n, t_len * SSM_GROUP, t_len * SSM_GROUP)
    pad = ((0, 0), (0, 0), (0, LANES - p_n))
    vr = jnp.pad(mr[::-1].transpose(1, 3, 0, 2).reshape(g_n, t_len * SSM_GROUP, p_n), pad)
    vi = jnp.pad(mi[::-1].transpose(1, 3, 0, 2).reshape(g_n, t_len * SSM_GROUP, p_n), pad)
    tzv = jnp.concatenate([tz, vr, vi], axis=-1).astype(BF16)
    pr1, pi1 = pr[1:].transpose(1, 2, 0), pi[1:].transpose(1, 2, 0)
    crt, cit = c_re.transpose(0, 2, 1), c_im.transpose(0, 2, 1)
    wre = crt[..., None] * pr1[:, :, None, :] - cit[..., None] * pi1[:, :, None, :]
    wim = -(crt[..., None] * pi1[:, :, None, :] + cit[..., None] * pr1[:, :, None, :])
    rpad = ((0, 0), (0, LANES - p_n), (0, 0))
    wre = jnp.pad(wre.reshape(g_n, p_n, -1), rpad).astype(BF16)
    wim = jnp.pad(wim.reshape(g_n, p_n, -1), rpad).astype(BF16)
    atr = jnp.pad(pr[t_len], ((0, 0), (0, LANES - p_n)))[:, None, :]
    ati = jnp.pad(pi[t_len], ((0, 0), (0, LANES - p_n)))[:, None, :]
    drow = jnp.repeat(d.reshape(g_n, SSM_GROUP), t_len, axis=1)[:, None, :]
    return tzv, wre, wim, atr, ati, drow


def _ssm_kernel(nk, nb, u_ref, tzv_ref, wre_ref, wim_ref, atr_ref, ati_ref, d_ref, ire_ref, iim_ref,
                y_ref, fre_ref, fim_ref, yi_ref, sr_ref, si_ref, xr_ref, xi_ref):
    tc = u_ref.shape[2]
    u = u_ref[0]
    full = _dot(u, tzv_ref[0])
    yi_ref[...] = full[:, :tc]
    sr_ref[...] = full[:, tc:tc + LANES]
    si_ref[...] = full[:, tc + LANES:]
    atr, ati = atr_ref[0], ati_ref[0]

    def step(k, carry):
        xr, xi = carry
        rows = pl.ds(pl.multiple_of(k * nb, nb), nb)
        xr_ref[rows, :] = xr
        xi_ref[rows, :] = xi
        return (atr * xr - ati * xi + sr_ref[rows, :], atr * xi + ati * xr + si_ref[rows, :])

    xr, xi = lax.fori_loop(0, nk, step, (ire_ref[0], iim_ref[0]))
    fre_ref[0] = xr
    fim_ref[0] = xi
    y = (yi_ref[...] + _dot(xr_ref[...].astype(BF16), wre_ref[0]) + _dot(xi_ref[...].astype(BF16), wim_ref[0])
         + d_ref[0] * u.astype(F32))
    y_ref[0] = y.astype(BF16)


def _ssm(u, mats, init_re, init_im, nk, nb):
    tzv, wre, wim, atr, ati, drow = mats
    g_n, r, tc = u.shape
    per_g = lambda a: pl.BlockSpec((1,) + a.shape[1:], lambda g: (g, 0, 0))
    ins = [u, tzv, wre, wim, atr, ati, drow, init_re, init_im]
    st = jax.ShapeDtypeStruct((g_n, nb, LANES), F32)
    return pl.pallas_call(
        functools.partial(_ssm_kernel, nk, nb),
        grid=(g_n,),
        in_specs=[per_g(a) for a in ins],
        out_specs=[per_g(u), per_g(init_re), per_g(init_re)],
        out_shape=[jax.ShapeDtypeStruct(u.shape, BF16), st, st],
        scratch_shapes=[pltpu.VMEM((r, tc), F32)] + [pltpu.VMEM((r, LANES), F32)] * 4,
        compiler_params=_params(("parallel",)),
        name="ssm",
    )(*ins)


def _ssm_branch(zu, mats, st_re, st_im):
    b, l, w = zu.shape
    g_n, p_n = st_re.shape[1], st_re.shape[2]
    nk = l // SSM_T
    ug = zu.reshape(b, nk, SSM_T, g_n, SSM_GROUP).transpose(3, 1, 0, 4, 2).reshape(g_n, nk * b, SSM_T * SSM_GROUP)
    pad = ((0, 0), (0, 0), (0, LANES - p_n))
    ire = jnp.pad(st_re.transpose(1, 0, 2), pad)
    iim = jnp.pad(st_im.transpose(1, 0, 2), pad)
    y, fre, fim = _ssm(ug, mats, ire, iim, nk, b)
    y = y.reshape(g_n, nk, b, SSM_GROUP, SSM_T).transpose(2, 1, 4, 0, 3).reshape(b, l, w)
    return y, fre[:, :, :p_n].transpose(1, 0, 2), fim[:, :, :p_n].transpose(1, 0, 2)


SSM_BLOCK = 128
_SSM_SUB = SSM_BLOCK // SSM_T
_SSM_PITCH = 8


def _ssm_long_kernel(nb, nk, u_ref, tzv_ref, wre_ref, wim_ref, atr_ref, ati_ref, d_ref, ire_ref, iim_ref,
                     y_ref, fre_ref, fim_ref, yi_ref, sr_ref, si_ref, xr_ref, xi_ref):
    tc = SSM_T * SSM_GROUP
    rows = nb * nk
    pitch = nk + _SSM_PITCH
    x3 = u_ref[...].reshape(SSM_GROUP, rows, SSM_BLOCK)
    tzv = tzv_ref[0]
    lhs = []
    for s in range(_SSM_SUB):
        lhs_s = jnp.concatenate([x3[c][:, s * SSM_T:(s + 1) * SSM_T] for c in range(SSM_GROUP)], axis=1)
        lhs.append(lhs_s)
        full = _dot(lhs_s.astype(BF16), tzv)
        yi_ref[s] = full[:, :tc]
        for b in range(nb):
            sr_ref[s, b * pitch:b * pitch + nk, :] = full[b * nk:(b + 1) * nk, tc:tc + LANES]
            si_ref[s, b * pitch:b * pitch + nk, :] = full[b * nk:(b + 1) * nk, tc + LANES:]
    atr, ati = atr_ref[0], ati_ref[0]

    def step(k, carry):
        xr, xi = carry
        across = pl.ds(k, nb, stride=pitch)
        for s in range(_SSM_SUB):
            xr_ref[s, across, :] = xr
            xi_ref[s, across, :] = xi
            xr, xi = (atr * xr - ati * xi + sr_ref[s, across, :], atr * xi + ati * xr + si_ref[s, across, :])
        return xr, xi

    xr, xi = lax.fori_loop(0, nk, step, (ire_ref[0], iim_ref[0]))
    fre_ref[0] = xr
    fim_ref[0] = xi
    ys = []
    for s in range(_SSM_SUB):
        unpitch = lambda ref: jnp.concatenate([ref[s, b * pitch:b * pitch + nk, :] for b in range(nb)], axis=0)
        ys.append(yi_ref[s] + _dot(unpitch(xr_ref).astype(BF16), wre_ref[0]) + _dot(unpitch(xi_ref).astype(BF16), wim_ref[0])
                  + d_ref[0] * lhs[s])
    y3 = jnp.stack([jnp.concatenate([y[:, c * SSM_T:(c + 1) * SSM_T] for y in ys], axis=1) for c in range(SSM_GROUP)])
    y_ref[...] = y3.reshape(SSM_GROUP, rows * SSM_BLOCK)


def _ssm_long(ut, mats, init_re, init_im, nb, nk):
    tzv, wre, wim, atr, ati, drow = mats
    g_n = tzv.shape[0]
    n = ut.shape[1]
    rows = nb * nk
    per_g = lambda a: pl.BlockSpec((1,) + a.shape[1:], lambda g: (g, 0, 0))
    chan = pl.BlockSpec((SSM_GROUP, n), lambda g: (g, 0))
    st = jax.ShapeDtypeStruct((g_n, nb, LANES), F32)
    prows = nb * (nk + _SSM_PITCH)
    return pl.pallas_call(
        functools.partial(_ssm_long_kernel, nb, nk),
        grid=(g_n,),
        in_specs=[chan] + [per_g(a) for a in (tzv, wre, wim, atr, ati, drow, init_re, init_im)],
        out_specs=[chan, per_g(init_re), per_g(init_re)],
        out_shape=[jax.ShapeDtypeStruct(ut.shape, F32), st, st],
        scratch_shapes=[pltpu.VMEM((_SSM_SUB, rows, SSM_T * SSM_GROUP), F32)]
        + [pltpu.VMEM((_SSM_SUB, prows, LANES), F32)] * 4,
        compiler_params=_params(("parallel",)),
        name="ssm_long",
    )(ut, tzv, wre, wim, atr, ati, drow, init_re, init_im)


def _ssm_branch_long(ut, nb, mats, st_re, st_im):
    p_n = st_re.shape[2]
    pad = ((0, 0), (0, 0), (0, LANES - p_n))
    ire = jnp.pad(st_re.transpose(1, 0, 2), pad)
    iim = jnp.pad(st_im.transpose(1, 0, 2), pad)
    yt, fre, fim = _ssm_long(ut, mats, ire, iim, nb, ut.shape[1] // (nb * SSM_BLOCK))
    return yt, fre[:, :, :p_n].transpose(1, 0, 2), fim[:, :, :p_n].transpose(1, 0, 2)


_DSA_TQ = 256
_DSA_VALUE_STEPS = 24
_DSA_SK = 1024
_DSA_MIN_SUM = 2.0 ** -80


def _dsa_kernel(tq, tv, sk, pos0, s_valid, n_sel, qlat_ref, qidx_ref, w_ref, ka_ref, c_ref, cmax_ref, wuv_ref,
                o_ref, key_ref, bias_ref, j_ref):
    s_pad = key_ref.shape[0]
    q0 = pos0 + pl.program_id(1) * tq
    qpos = q0 + lax.broadcasted_iota(I32, (1, tq), 1)
    vis = jnp.minimum((qpos // CHUNK + 1) * CHUNK, s_valid)
    vis_max = jnp.minimum(((q0 + tq - 1) // CHUNK + 1) * CHUNK, s_valid)
    nch = (vis_max + sk - 1) // sk
    kpos = lax.broadcasted_iota(I32, (sk, tq), 0)
    kslice = lambda j: pl.ds(pl.multiple_of(j * sk, sk), sk)
    fold = lambda a: a.reshape(sk // 8, 8, tq)

    qs = jnp.concatenate([qidx_ref[0, :, hd * LANES:(hd + 1) * LANES] for hd in range(IDX_HEADS)], axis=0)

    def score_chunk(j, carry):
        kmin, kmax = carry
        rel = jnp.maximum(_dot_nt(ka_ref[0, kslice(j), :], qs), 0.0)
        acc = jnp.zeros((sk, tq), F32)
        for hd in range(IDX_HEADS):
            acc = acc + w_ref[0, hd:hd + 1, :] * rel[:, hd * tq:(hd + 1) * tq]
        bits = pltpu.bitcast(acc, I32)
        key = bits ^ ((bits >> 31) & 0x7FFFFFFF)
        visible = j * sk + kpos < vis
        key_ref[kslice(j), :] = jnp.where(visible, key, INT_MIN)
        kmin = jnp.minimum(kmin, jnp.min(fold(jnp.where(visible, key, INT_MAX)), axis=0))
        kmax = jnp.maximum(kmax, jnp.max(fold(jnp.where(visible, key, INT_MIN)), axis=0))
        return kmin, kmax

    kmin, kmax = lax.fori_loop(0, nch, score_chunk,
                               (jnp.full((8, tq), INT_MAX, I32), jnp.full((8, tq), INT_MIN, I32)))
    kmin = jnp.min(kmin, axis=0, keepdims=True)
    kmax = jnp.max(kmax, axis=0, keepdims=True)

    def count(pred):
        def body(j, c):
            hit = jnp.where(pred(key_ref[kslice(j), :], j * sk + kpos), 1.0, 0.0)
            return c + jnp.sum(fold(hit), axis=0)
        return jnp.sum(lax.fori_loop(0, nch, body, jnp.zeros((8, tq), F32)), axis=0, keepdims=True)

    key_of = lambda v: (lambda b: b ^ ((b >> 31) & 0x7FFFFFFF))(pltpu.bitcast(v, I32))
    val_of = lambda k: pltpu.bitcast(k ^ ((k >> 31) & 0x7FFFFFFF), F32)

    def settled(lo, hi, cnt):
        return (cnt <= n_sel) | (hi - 1 <= lo)

    def halve(it, lo, hi, cnt):
        mid_v = key_of(0.5 * val_of(lo) + 0.5 * val_of(hi))
        mid_k = (lo >> 1) + (hi >> 1) + (lo & hi & 1)
        mid = jnp.where((mid_v > lo) & (mid_v < hi) & (it < _DSA_VALUE_STEPS), mid_v, mid_k)
        c = count(lambda k, col: k >= mid)
        live = jnp.logical_not(settled(lo, hi, cnt))
        up = live & (c >= n_sel)
        return jnp.where(up, mid, lo), jnp.where(live & (c < n_sel), mid, hi), jnp.where(up, c, cnt)

    def halve_twice(state):
        it, lo, hi, cnt, _ = state
        lo, hi, cnt = halve(it, lo, hi, cnt)
        lo, hi, cnt = halve(it + 1, lo, hi, cnt)
        return it + 2, lo, hi, cnt, jnp.max(jnp.where(settled(lo, hi, cnt), 0, 1))

    c_pos, c_nn = count(lambda k, col: k >= 1), count(lambda k, col: k >= 0)
    pos, zero = c_pos >= n_sel, c_nn >= n_sel
    lo0 = jnp.where(pos, 1, jnp.where(zero, 0, kmin))
    hi0 = jnp.where(pos, jnp.where(kmax == INT_MAX, INT_MAX, kmax + 1), jnp.where(zero, 1, 0))
    cnt0 = jnp.where(pos, c_pos, jnp.where(zero, c_nn, vis.astype(F32)))
    state = (jnp.int32(0), lo0, hi0, cnt0, jnp.max(jnp.where(settled(lo0, hi0, cnt0), 0, 1)))
    _, thr, _, cnt, _ = lax.while_loop(lambda st: (st[0] < _DSA_VALUE_STEPS + 34) & (st[4] > 0), halve_twice, state)
    thr = jnp.maximum(thr, INT_MIN + 1)
    excess = cnt > n_sel

    j_ref[...] = jnp.full(j_ref.shape, s_pad, I32)

    @pl.when(jnp.max(jnp.where(excess, 1.0, 0.0)) > 0.0)
    def _():
        need = n_sel - count(lambda k, col: k > thr)

        def idx_step(_, lohi):
            lo, hi = lohi
            mid = (lo + hi) >> 1
            ok = count(lambda k, col: (k == thr) & (col < mid)) >= need
            return jnp.where(ok, lo, mid), jnp.where(ok, mid, hi)

        steps = int(math.ceil(math.log2(s_pad))) + 1
        _, hi = lax.fori_loop(0, steps, idx_step, (jnp.zeros((1, tq), I32), jnp.full((1, tq), s_pad, I32)))
        j_ref[...] = jnp.broadcast_to(jnp.where(excess, hi, s_pad), j_ref.shape)

    jlim = j_ref[0:1, :]

    rows = DSA_HEADS * tv
    q = jnp.concatenate([qlat_ref[0, :tv, hd * DSA_LATENT:(hd + 1) * DSA_LATENT] for hd in range(DSA_HEADS)], axis=0)

    wb = min(sk, 2 * LANES)

    kpos_wb = lax.broadcasted_iota(I32, (wb, tq), 0)

    def logits(j, i, first):
        blk = pl.ds(pl.multiple_of(j * sk + i * wb, wb), wb)
        if first:
            k = key_ref[blk, :]
            sel = (k > thr) | ((k == thr) & (j * sk + i * wb + kpos_wb < jlim))
            bias = jnp.where(sel, 0.0, NEG_BIG).T[:tv]
            bias_ref[:, blk] = bias
        else:
            bias = bias_ref[:, blk]
        s = _dot_nt(q, c_ref[0, blk, :])
        return (s.reshape(DSA_HEADS, tv, wb) + bias[None]).reshape(rows, wb)

    def attend(m, first):
        half = rows // 2

        def run(lo, build):
            qh, mh = q[lo:lo + half], m[lo:lo + half]

            def acc_chunk(j, carry):
                l, acc = carry
                ps = []
                for i in range(sk // wb):
                    blk = pl.ds(pl.multiple_of(j * sk + i * wb, wb), wb)
                    if build:
                        k = key_ref[blk, :]
                        sel = (k > thr) | ((k == thr) & (j * sk + i * wb + kpos_wb < jlim))
                        bias = jnp.where(sel, 0.0, NEG_BIG).T[:tv]
                        bias_ref[:, blk] = bias
                    else:
                        bias = bias_ref[:, blk]
                    s = _dot_nt(qh, c_ref[0, blk, :])
                    s = (s.reshape(DSA_HEADS // 2, tv, wb) + bias[None]).reshape(half, wb)
                    for t in range(wb // LANES):
                        p = jnp.exp2(s[:, t * LANES:(t + 1) * LANES] - mh)
                        l = l + p
                        ps.append(p.astype(BF16))
                return l, acc + _dot(jnp.concatenate(ps, axis=1), c_ref[0, kslice(j), :])

            return lax.fori_loop(0, nch, acc_chunk, (jnp.zeros((half, LANES), F32), jnp.zeros((half, DSA_LATENT), F32)))

        l0, a0 = run(0, first)
        l1, a1 = run(half, False)
        return jnp.sum(jnp.concatenate([l0, l1], axis=0), axis=1, keepdims=True), jnp.concatenate([a0, a1], axis=0)

    def emit(l, acc):
        o = (acc / l).astype(BF16)
        for hd in range(DSA_HEADS):
            o_ref[0, :tv, hd * DSA_HEAD_DIM:(hd + 1) * DSA_HEAD_DIM] = _dot(o[hd * tv:(hd + 1) * tv], wuv_ref[hd]).astype(BF16)
        if tv < tq:
            o_ref[0, tv:, :] = jnp.zeros((tq - tv, o_ref.shape[2]), BF16)

    qf = q.astype(F32)
    bound = jnp.sqrt(jnp.sum(qf * qf, axis=1, keepdims=True)) * cmax_ref[0, 0:1, 0:1]
    l, acc = attend(jnp.broadcast_to(bound, (rows, LANES)), True)
    healthy = jnp.min(l) > _DSA_MIN_SUM

    @pl.when(healthy)
    def _():
        emit(l, acc)

    @pl.when(jnp.logical_not(healthy))
    def _():
        def max_chunk(j, mx):
            for i in range(sk // wb):
                s = logits(j, i, False)
                for t in range(wb // LANES):
                    mx = jnp.maximum(mx, s[:, t * LANES:(t + 1) * LANES])
            return mx

        mx = lax.fori_loop(0, nch, max_chunk, jnp.full((rows, LANES), NEG_BIG, F32))
        emit(*attend(jnp.broadcast_to(jnp.max(mx, axis=1, keepdims=True), (rows, LANES)), False))


def _dsa(qlat, qidx, w, ka, c, wuv, pos0, s_valid, n_sel, tq):
    b, l_true, _ = qlat.shape
    l = -(-l_true // tq) * tq
    qlat, qidx, w = (jnp.pad(a, ((0, 0), (0, l - l_true), (0, 0))) for a in (qlat, qidx, w))
    wt = w.transpose(0, 2, 1)
    s_pad = c.shape[1]
    sk = min(_DSA_SK, s_pad)
    cf = c.astype(F32)
    cmax = jnp.broadcast_to(jnp.sqrt(jnp.max(jnp.sum(cf * cf, axis=2), axis=1))[:, None, None], (b, 1, LANES))
    qspec = lambda a: pl.BlockSpec((1, tq, a.shape[2]), lambda i, j: (i, j, 0))
    kspec = lambda a: pl.BlockSpec((1,) + a.shape[1:], lambda i, j: (i, 0, 0))
    dh = wuv.shape[0] * wuv.shape[2]
    return pl.pallas_call(
        functools.partial(_dsa_kernel, tq, min(tq, l_true), sk, pos0, s_valid, n_sel),
        grid=(b, l // tq),
        in_specs=[qspec(qlat), qspec(qidx), pl.BlockSpec((1, wt.shape[1], tq), lambda i, j: (i, 0, j)), kspec(ka), kspec(c),
                  kspec(cmax), _const_spec(wuv.shape)],
        out_specs=pl.BlockSpec((1, tq, dh), lambda i, j: (i, j, 0)),
        out_shape=jax.ShapeDtypeStruct((b, l, dh), BF16),
        scratch_shapes=[pltpu.VMEM((s_pad, tq), I32), pltpu.VMEM((min(tq, l_true), s_pad), F32), pltpu.VMEM((8, tq), I32)],
        compiler_params=_params(("parallel", "arbitrary")),
        name="dsa",
    )(qlat, qidx, wt, ka, c, cmax, wuv)[:, :l_true]


_PEER_WIDTH = [PEER_TOPK // (r + 1) for r in range(PEER_TOPK)]


def _top_rows(s, k):
    row = lax.broadcasted_iota(I32, s.shape, 0)
    vals, idxs = [], []
    for _ in range(k):
        m = jnp.max(s, axis=0, keepdims=True)
        first = jnp.min(jnp.where(s == m, row, s.shape[0]), axis=0, keepdims=True)
        s = jnp.where(row == first, -jnp.inf, s)
        vals.append(m)
        idxs.append(first)
    return jnp.concatenate(vals, axis=0), jnp.concatenate(idxs, axis=0)


def _peer_route_kernel(h_ref, g_ref, wq_ref, k1_ref, k2_ref, xn_ref, i1_ref, i2_ref, gate_ref):
    tn = h_ref.shape[0]
    xn = _rms(h_ref[...], g_ref[...]).astype(BF16)
    xn_ref[...] = xn
    q = _dot(xn, wq_ref[...]).astype(BF16)
    oute, outg = [], []
    for hd in range(PEER_HEADS):
        qa = q[:, (2 * hd) * PEER_HALF:(2 * hd + 1) * PEER_HALF]
        qb = q[:, (2 * hd + 1) * PEER_HALF:(2 * hd + 2) * PEER_HALF]
        v1, i1 = _top_rows(_dot_nt(k1_ref[hd], qa), PEER_TOPK)
        v2, i2 = _top_rows(_dot_nt(k2_ref[hd], qb), PEER_TOPK)
        cand = jnp.concatenate([v1[r:r + 1] + v2[0:w] for r, w in enumerate(_PEER_WIDTH)], axis=0)
        cexp = jnp.concatenate([i1[r:r + 1] * PEER_NKEYS + i2[0:w] for r, w in enumerate(_PEER_WIDTH)], axis=0)
        row = lax.broadcasted_iota(I32, cand.shape, 0)
        tops, experts = [], []
        for _ in range(PEER_TOPK):
            m = jnp.max(cand, axis=0, keepdims=True)
            first = jnp.min(jnp.where(cand == m, row, cand.shape[0]), axis=0, keepdims=True)
            hit = row == first
            experts.append(jnp.sum(jnp.where(hit, cexp, 0), axis=0, keepdims=True))
            cand = jnp.where(hit, -jnp.inf, cand)
            tops.append(m)
        top = jnp.concatenate(tops, axis=0)
        ex = jnp.exp(top - top[0:1])
        outg.append(ex / jnp.sum(ex, axis=0, keepdims=True))
        oute.append(jnp.concatenate(experts, axis=0))
    expert = jnp.concatenate(oute, axis=0)
    as_rows = lambda a: pltpu.bitcast(pltpu.bitcast(a, F32).T, I32)
    i1_ref[...] = as_rows(expert >> 7)
    i2_ref[...] = as_rows(expert & (PEER_NKEYS - 1))
    gate_ref[...] = jnp.concatenate(outg, axis=0).T


def _peer_route(h, g, wq, k1, k2):
    n, d = h.shape
    tn = LANES
    slots = PEER_HEADS * PEER_TOPK
    row = lambda w: pl.BlockSpec((tn, w), lambda i: (i, 0))
    return pl.pallas_call(
        _peer_route_kernel,
        grid=(n // tn,),
        in_specs=[row(d), _const_spec(g.shape), _const_spec(wq.shape), _const_spec(k1.shape), _const_spec(k2.shape)],
        out_specs=[row(d), row(slots), row(slots), row(slots)],
        out_shape=[jax.ShapeDtypeStruct((n, d), BF16), jax.ShapeDtypeStruct((n, slots), I32),
                   jax.ShapeDtypeStruct((n, slots), I32), jax.ShapeDtypeStruct((n, slots), F32)],
        compiler_params=_params(("parallel",)),
        name="peer_route",
    )(h, g, wq, k1, k2)


_PLANE_PAD = 4


def _peer_gates_kernel(i1_ref, i2_ref, gate_ref, a_ref, planes_ref):
    tn, slots = i1_ref.shape
    plane = tn + _PLANE_PAD
    sub = lax.broadcasted_iota(I32, (PEER_NKEYS, slots), 0)

    def token(n, carry):
        r = pl.ds(n, 1)
        pt = jnp.where(sub == i1_ref[r, :], gate_ref[r, :], 0.0).astype(BF16)
        qt = jnp.where(sub == i2_ref[r, :], 1.0, 0.0).astype(BF16)
        planes_ref[pl.ds(n, PEER_NKEYS, stride=plane), :] = _dot_nt(pt, qt)
        return carry

    lax.fori_loop(0, tn, token, 0, unroll=True)
    for k in range(PEER_NKEYS):
        a_ref[:, k * PEER_NKEYS:(k + 1) * PEER_NKEYS] = planes_ref[pl.ds(k * plane, tn), :].astype(BF16)


def _peer_gates(i1, i2, gate):
    n, slots = i1.shape
    tn = _pick_tile(n, LANES)
    ne = PEER_NKEYS * PEER_NKEYS
    row = lambda w: pl.BlockSpec((tn, w), lambda i: (i, 0))
    return pl.pallas_call(
        _peer_gates_kernel,
        grid=(n // tn,),
        in_specs=[row(slots)] * 3,
        out_specs=row(ne),
        out_shape=jax.ShapeDtypeStruct((n, ne), BF16),
        scratch_shapes=[pltpu.VMEM((PEER_NKEYS * (tn + _PLANE_PAD), PEER_NKEYS), F32)],
        compiler_params=_params(("parallel",)),
        name="peer_gates",
    )(i1, i2, gate)


_PEER_SUB = 256
def _peer_apply_kernel(final_norm, xn_ref, a_ref, u_ref, v_ref, h_ref, g_ref, o_ref, acc_ref):
    j = pl.program_id(1)

    @pl.when(j == 0)
    def _():
        acc_ref[...] = jnp.zeros_like(acc_ref)

    xn = xn_ref[...]
    ws = []
    for s in range(u_ref.shape[0] // _PEER_SUB):
        sub = slice(s * _PEER_SUB, (s + 1) * _PEER_SUB)
        act = _gelu(_dot_nt(xn, u_ref[sub, :]))
        ws.append((act * a_ref[:, sub].astype(F32)).astype(BF16))
    acc_ref[...] += _dot(jnp.concatenate(ws, axis=1), v_ref[...])

    @pl.when(j == pl.num_programs(1) - 1)
    def _():
        y = h_ref[...] + acc_ref[...]
        o_ref[...] = _rms(y, g_ref[...]) if final_norm else y


def _peer_apply(xn, a, u, v, h, g, final_norm):
    n, d = xn.shape
    ne = u.shape[0]
    tn = _pick_tile(n, ROWS_NARROW)
    te = 8 * _PEER_SUB
    return pl.pallas_call(
        functools.partial(_peer_apply_kernel, final_norm),
        grid=(n // tn, ne // te),
        in_specs=[pl.BlockSpec((tn, d), lambda i, j: (i, 0)), pl.BlockSpec((tn, te), lambda i, j: (i, j)),
                  pl.BlockSpec((te, d), lambda i, j: (j, 0)), pl.BlockSpec((te, d), lambda i, j: (j, 0)),
                  pl.BlockSpec((tn, d), lambda i, j: (i, 0)), _const_spec(g.shape)],
        out_specs=pl.BlockSpec((tn, d), lambda i, j: (i, 0)),
        out_shape=jax.ShapeDtypeStruct((n, d), F32),
        scratch_shapes=[pltpu.VMEM((tn, d), F32)],
        compiler_params=_params(("parallel", "arbitrary")),
        name="peer_apply",
    )(xn, a, u, v, h, g)


def _split_w_in(w_in, d):
    widths = (d, DSA_HEADS * DSA_LATENT, DSA_LATENT, IDX_HEADS * IDX_DIM, IDX_DIM, IDX_HEADS,
              MEM_HEADS * MEM_HEAD_DIM, N_BRANCH * d)
    offs = np.cumsum((0,) + widths)
    wu, wq, wc, wqi, wki, wwi, wmq, wg = [w_in[:, offs[i]:offs[i + 1]] for i in range(8)]
    wkw = jnp.pad(jnp.concatenate([wki, wwi], axis=1), ((0, 0), (0, LANES - IDX_DIM - IDX_HEADS)))
    kw_scale = jnp.concatenate([jnp.ones((IDX_DIM,), F32), jnp.full((IDX_HEADS,), IDX_HEADS ** -0.5, F32),
                                jnp.zeros((LANES - IDX_DIM - IDX_HEADS,), F32)])[None, :]
    wqi = jnp.pad(wqi.reshape(-1, IDX_HEADS, IDX_DIM), ((0, 0), (0, 0), (0, LANES - IDX_DIM))).reshape(-1, IDX_HEADS * LANES)
    ws = [wu.T, wq * (DSA_LATENT ** -0.5 * LOG2E), wc, wqi, wkw, wmq * MEM_HEAD_DIM ** -0.5, wg]
    return [w.astype(BF16) for w in ws], kw_scale


def _pad_rows(a, n):
    return a if a.shape[0] == n else jnp.pad(a, ((0, n - a.shape[0]),) + ((0, 0),) * (a.ndim - 1))


def _pad_keys(a, s_pad):
    return jnp.pad(a, ((0, 0), (0, s_pad - a.shape[1]), (0, 0)))


def _token_mix(h, seq_shape, lw, mats, state, cache, mem_kv, pos0):
    b, l = seq_shape
    bf = lambda a: a.astype(BF16)
    seq = lambda a: a.reshape(b, l, -1)
    ut, q, c, c16, qi, kw, k16, mq, g = _inproj(h, lw["g_norm1"], lw["g_kv"], lw["kw_scale"], lw["w_in"])
    ki, wi = kw[:, :IDX_DIM], kw[:, IDX_DIM:IDX_DIM + IDX_HEADS]

    if l % SSM_BLOCK == 0:
        yst, s_re, s_im = _ssm_branch_long(ut, b, mats, state[0], state[1])
    else:
        ys, s_re, s_im = _ssm_branch(seq(bf(ut.T)), mats, state[0], state[1])
        yst = ys.reshape(b * l, -1).astype(F32).T

    c_all, k_all = seq(c16), seq(k16)
    if cache is not None:
        lane_pad = ((0, 0), (0, 0), (0, LANES - IDX_DIM))
        c_all = jnp.concatenate([bf(cache[0]), c_all], axis=1)
        k_all = jnp.concatenate([jnp.pad(bf(cache[1]), lane_pad), k_all], axis=1)
    s_all = c_all.shape[1]
    s_pad = -(-s_all // _DSA_SK) * _DSA_SK if cache is not None else s_all
    yd = _dsa(seq(q), seq(qi), seq(wi), _pad_keys(k_all, s_pad), _pad_keys(c_all, s_pad), lw["w_uv"], pos0, s_all,
              min(DSA_TOPK, s_all // 4), _pick_tile(max(l, LANES), _DSA_TQ))

    ym = _memattn(seq(mq), mem_kv[0], mem_kv[1])

    flat = lambda a: a.reshape(b * l, -1)
    h2 = _merge(h, yst, flat(yd), flat(ym), g, lw["w_glu"], lw["b_glu"], lw["w_br_ssm"], lw["w_br_dsa"],
                lw["w_br_mem"], lw["w_out"])
    return h2, seq(c), seq(ki), s_re, s_im


def _channel_mix(h2, lw, g_final, final_norm):
    n = h2.shape[0]
    h2 = _pad_rows(h2, -(-n // LANES) * LANES)
    xn, i1, i2, gate = _peer_route(h2, lw["g_norm2"], lw["w_peer_q"], lw["peer_sub_k1"], lw["peer_sub_k2"])
    a = _peer_gates(i1, i2, gate)
    return _peer_apply(xn, a, lw["peer_u"], lw["peer_v"], h2, g_final, final_norm)[:n]


def kernel(x_prompt, x_sample, mem_prompt, cache_dsa_latent, cache_dsa_idx_k, state_ssm_re, state_ssm_im, cache_mem_k, cache_mem_v, g_norm1, w_in, g_kv, w_uv, ssm_lam_re, ssm_lam_im, ssm_log_dt, ssm_b_re, ssm_b_im, ssm_c_re, ssm_c_im, ssm_d, w_glu, b_glu, g_mem, w_mem_kv, w_br_ssm, w_br_dsa, w_br_mem, w_out, g_norm2, w_peer_q, peer_sub_k1, peer_sub_k2, peer_u, peer_v, g_final):
    depth = w_in.shape[0]
    bp, lp, d = x_prompt.shape
    bs, ls, _ = x_sample.shape
    past = cache_dsa_latent.shape[2]
    mem_w = MEM_HEADS * MEM_HEAD_DIM
    n_mem = mem_prompt.shape[1]
    heads = (n_mem, MEM_HEADS, MEM_HEAD_DIM)
    bf = lambda a: a.astype(BF16)
    row = lambda a: a[None, :]

    hp, hs = x_prompt.reshape(bp * lp, d), x_sample.reshape(bs * ls, d)
    outs = [[] for _ in range(10)]
    for l in range(depth):
        ws, kw_scale = _split_w_in(w_in[l], d)
        lw = dict(g_norm1=row(g_norm1[l]), g_kv=row(g_kv[l]), kw_scale=kw_scale, w_in=ws, w_uv=bf(w_uv[l]),
                  w_glu=bf(w_glu[l]), b_glu=row(b_glu[l]), w_br_ssm=bf(w_br_ssm[l]), w_br_dsa=bf(w_br_dsa[l]),
                  w_br_mem=bf(w_br_mem[l]), w_out=bf(w_out[l]), g_norm2=row(g_norm2[l]), w_peer_q=bf(w_peer_q[l]),
                  peer_sub_k1=bf(peer_sub_k1[l]), peer_sub_k2=bf(peer_sub_k2[l]), peer_u=bf(peer_u[l]),
                  peer_v=bf(peer_v[l]))
        mats = _ssm_matrices(ssm_lam_re[l], ssm_lam_im[l], ssm_log_dt[l], ssm_b_re[l], ssm_b_im[l],
                             ssm_c_re[l], ssm_c_im[l], ssm_d[l], SSM_T)
        kv = _memkv(mem_prompt.reshape(bp * n_mem, d), row(g_mem[l]), bf(w_mem_kv[l]))
        mk_p, mv_p = kv[:, :mem_w].reshape(bp, n_mem, mem_w), kv[:, mem_w:].reshape(bp, n_mem, mem_w)
        zeros = jnp.zeros((bp,) + state_ssm_re.shape[2:], F32)

        h2p, c_p, ki_p, sre_p, sim_p = _token_mix(hp, (bp, lp), lw, mats, (zeros, zeros), None, (bf(mk_p), bf(mv_p)), 0)
        mem_s = (bf(cache_mem_k[l].reshape(bs, n_mem, mem_w)), bf(cache_mem_v[l].reshape(bs, n_mem, mem_w)))
        h2s, c_s, ki_s, sre_s, sim_s = _token_mix(hs, (bs, ls), lw, mats, (state_ssm_re[l], state_ssm_im[l]),
                                                  (cache_dsa_latent[l], cache_dsa_idx_k[l]), mem_s, past)
        final = l == depth - 1
        hp = _channel_mix(h2p, lw, row(g_final), final)
        hs = _channel_mix(h2s, lw, row(g_final), final)
        for lst, val in zip(outs, (c_p, ki_p, sre_p, sim_p, mk_p.reshape((bp,) + heads), mv_p.reshape((bp,) + heads),
                                   c_s, ki_s, sre_s, sim_s)):
            lst.append(val)
    return (hp.reshape(bp, lp, d), hs.reshape(bs, ls, d)) + tuple(jnp.stack(o) for o in outs)
```
